```python
import math
import jax, jax.numpy as jnp
from jax import lax
import numpy as np

D_MODEL = 4096
BATCH = 4
SEQ = 2048
DEPTH = 1
DEC_BATCH = 128
DEC_SEQ = 4
PAST_LEN = 16384
PAGE_SIZE = 128

N_META = 16
D_MIX = D_MODEL
D_RNN = D_MIX // 2
D_GLA = D_MIX - D_RNN
RG_BLOCKS = 16
RG_BLOCK = D_RNN // RG_BLOCKS
CONV_W = 4
RG_C = 8.0
GLA_HEADS = 16
GLA_DK = D_GLA // GLA_HEADS
GLA_DV = D_GLA // GLA_HEADS
GLA_RANK = 16
GLA_GATE_TAU = 16.0
CHUNK = 64
D_FF = 3 * D_MODEL
FFN_CONV_W = 3
EPS = 1e-6
IN_COLS = 2 * D_RNN + 4 * D_GLA + GLA_RANK

kernel_name = "hymba_rglru_gla_convffn_step"


def rmsnorm(x, g):
    xf = x.astype(jnp.float32)
    y = xf * lax.rsqrt(jnp.mean(xf * xf, axis=-1, keepdims=True) + EPS)
    return (y * g.astype(jnp.float32)).astype(x.dtype)


def causal_dwconv(x_ext, w, b):
    width = w.shape[0]
    t = x_ext.shape[1] - width + 1
    y = b
    for i in range(width):
        y = y + x_ext[:, i:i + t] * w[i]
    return y


def rglru(xc, h0, wa, ba, wx, bx, lam, reset_first):
    f32 = jnp.float32
    b, t, _ = xc.shape
    xf = xc.astype(f32)
    xblk = xf.reshape(b, t, RG_BLOCKS, RG_BLOCK)
    r = jax.nn.sigmoid(jnp.einsum('btnc,ncd->btnd', xblk, wa.astype(f32)).reshape(b, t, D_RNN) + ba.astype(f32))
    i = jax.nn.sigmoid(jnp.einsum('btnc,ncd->btnd', xblk, wx.astype(f32)).reshape(b, t, D_RNN) + bx.astype(f32))
    log_a = -RG_C * r * jax.nn.softplus(-lam.astype(f32))
    a = jnp.exp(log_a)
    mult = jnp.sqrt(-jnp.expm1(2.0 * log_a))
    if reset_first:
        mult = mult.at[:, 0].set(1.0)
    u = mult * i * xf

    def step(h, au):
        a_t, u_t = au
        h = a_t * h + u_t
        return h, h

    h_last, hs = lax.scan(step, h0.astype(f32), (jnp.swapaxes(a, 0, 1), jnp.swapaxes(u, 0, 1)))
    return jnp.swapaxes(hs, 0, 1), h_last


def gla_chunked(q, k, v, g, s0, chunk):
    b, t, h, dk = q.shape
    dv = v.shape[-1]
    n = t // chunk

    def to_chunks(z):
        return z.reshape(b, n, chunk, h, z.shape[-1]).transpose(1, 0, 3, 2, 4)

    mask = jnp.tril(jnp.ones((chunk, chunk), dtype=bool))[:, :, None]

    def step(s, inp):
        qc, kc, vc, gc = inp
        cum = jnp.cumsum(gc, axis=2)
        o_inter = jnp.einsum('bhid,bhdv->bhiv', qc * jnp.exp(cum), s)
        diff = cum[:, :, :, None, :] - cum[:, :, None, :, :]
        decay = jnp.exp(jnp.where(mask, diff, -jnp.inf))
        att = jnp.einsum('bhid,bhjd,bhijd->bhij', qc, kc, decay)
        o_intra = jnp.einsum('bhij,bhjv->bhiv', att, vc)
        last = cum[:, :, -1:, :]
        s_new = jnp.exp(last[:, :, 0, :])[..., None] * s + jnp.einsum('bhjd,bhjv->bhdv', kc * jnp.exp(last - cum), vc)
        return s_new, o_inter + o_intra

    s_last, o = lax.scan(step, s0, (to_chunks(q), to_chunks(k), to_chunks(v), to_chunks(g)))
    o = o.transpose(1, 0, 3, 2, 4).reshape(b, t, h, dv)
    return o, s_last


def gla_mix(q, k, v, z, go, s0, wa2, ba2, head_gain, segments):
    f32 = jnp.float32
    b, t, _ = q.shape

    def heads(y):
        return y.reshape(b, t, GLA_HEADS, -1).astype(f32)

    qh = heads(q) * (GLA_DK ** -0.5)
    kh = heads(k)
    vh = heads(v)
    log_alpha = jax.nn.log_sigmoid((z @ wa2 + ba2).astype(f32)) / GLA_GATE_TAU
    gh = heads(log_alpha)
    s = s0.astype(f32)
    outs = []
    start = 0
    for length, chunk in segments:
        sl = slice(start, start + length)
        o, s = gla_chunked(qh[:, sl], kh[:, sl], vh[:, sl], gh[:, sl], s, chunk)
        outs.append(o)
        start += length
    o = jnp.concatenate(outs, axis=1)
    o = o * lax.rsqrt(jnp.mean(o * o, axis=-1, keepdims=True) + EPS)
    o = o * head_gain.astype(f32).reshape(GLA_HEADS, GLA_DV)
    o = o.reshape(b, t, D_GLA) * jax.nn.silu(go.astype(f32))
    return o.astype(q.dtype), s


def layer(x, conv_buf, h0, s0, ffn_buf, segments, reset_first,
          norm_mix, w_in, rg_conv_w, rg_conv_b, rg_wa, rg_ba, rg_wx, rg_bx, rg_lambda, rg_out_norm,
          gla_wa2, gla_ba2, gla_head_norm, w_out, norm_ffn, w_gate, w_up, ffn_conv_w, ffn_conv_b, w_down):
    xn = rmsnorm(x, norm_mix)
    proj = xn @ w_in
    cuts = np.cumsum([D_RNN, D_RNN, D_GLA, D_GLA, D_GLA, D_GLA]).tolist()
    xr, yr, q, k, v, go, z = jnp.split(proj, cuts, axis=-1)
    xr_ext = jnp.concatenate([conv_buf.astype(xr.dtype), xr], axis=1)
    xc = causal_dwconv(xr_ext, rg_conv_w, rg_conv_b)
    hs, h_last = rglru(xc, h0, rg_wa, rg_ba, rg_wx, rg_bx, rg_lambda, reset_first)
    rnn_out = rmsnorm((hs * jax.nn.gelu(yr.astype(jnp.float32))).astype(x.dtype), rg_out_norm)
    new_conv = xr_ext[:, -(CONV_W - 1):]
    gla_out, s_last = gla_mix(q, k, v, z, go, s0, gla_wa2, gla_ba2, gla_head_norm, segments)
    x = x + (jnp.concatenate([rnn_out, gla_out.astype(x.dtype)], axis=-1) @ w_out).astype(x.dtype)
    xn2 = rmsnorm(x, norm_ffn)
    gate = xn2 @ w_gate
    up = xn2 @ w_up
    gate_ext = jnp.concatenate([ffn_buf.astype(gate.dtype), gate], axis=1)
    gate_c = causal_dwconv(gate_ext, ffn_conv_w, ffn_conv_b)
    x = x + ((jax.nn.gelu(gate_c) * up) @ w_down).astype(x.dtype)
    new_ffn = gate_ext[:, -(FFN_CONV_W - 1):]
    return x, new_conv, h_last.astype(h0.dtype), s_last.astype(s0.dtype), new_ffn


def setup_inputs(seed: int = 0) -> dict:
    key = jax.random.key(seed)
    ks = jax.random.split(key, 32)
    f32 = jnp.float32
    nrm = lambda k, shape, s: jax.random.normal(k, shape, f32) * s
    u = jax.random.uniform(ks[12], (DEPTH, D_RNN), f32, minval=0.9, maxval=0.999)
    a_base = u ** (1.0 / RG_C)
    rg_lambda = jnp.log(a_base) - jnp.log1p(-a_base)
    return {
        "x_prompt": nrm(ks[0], (BATCH, SEQ, D_MODEL), 1.0),
        "x_sample": nrm(ks[1], (DEC_BATCH, DEC_SEQ, D_MODEL), 1.0),
        "state_rglru_conv": nrm(ks[2], (DEPTH, DEC_BATCH, CONV_W - 1, D_RNN), 1.0),
        "state_rglru_h": nrm(ks[3], (DEPTH, DEC_BATCH, D_RNN), 0.5),
        "state_gla_S": nrm(ks[4], (DEPTH, DEC_BATCH, GLA_HEADS, GLA_DK, GLA_DV), 1.0),
        "state_ffn_conv": nrm(ks[5], (DEPTH, DEC_BATCH, FFN_CONV_W - 1, D_FF), 1.0),
        "meta_tokens": nrm(ks[6], (N_META, D_MODEL), 1.0),
        "norm_mix": 1.0 + nrm(ks[7], (DEPTH, D_MODEL), 0.02),
        "w_in": nrm(ks[8], (DEPTH, D_MODEL, IN_COLS), D_MODEL ** -0.5),
        "rg_conv_w": nrm(ks[9], (DEPTH, CONV_W, D_RNN), CONV_W ** -0.5),
        "rg_conv_b": nrm(ks[10], (DEPTH, D_RNN), 0.02),
        "rg_wa": nrm(ks[11], (DEPTH, RG_BLOCKS, RG_BLOCK, RG_BLOCK), RG_BLOCK ** -0.5),
        "rg_ba": nrm(ks[13], (DEPTH, D_RNN), 0.02),
        "rg_wx": nrm(ks[14], (DEPTH, RG_BLOCKS, RG_BLOCK, RG_BLOCK), RG_BLOCK ** -0.5),
        "rg_bx": nrm(ks[15], (DEPTH, D_RNN), 0.02),
        "rg_lambda": rg_lambda,
        "rg_out_norm": 1.0 + nrm(ks[16], (DEPTH, D_RNN), 0.02),
        "gla_wa2": nrm(ks[17], (DEPTH, GLA_RANK, D_GLA), GLA_RANK ** -0.5),
        "gla_ba2": nrm(ks[18], (DEPTH, D_GLA), 0.1),
        "gla_head_norm": 1.0 + nrm(ks[19], (DEPTH, D_GLA), 0.02),
        "w_out": nrm(ks[20], (DEPTH, D_MIX, D_MODEL), D_MIX ** -0.5),
        "norm_ffn": 1.0 + nrm(ks[21], (DEPTH, D_MODEL), 0.02),
        "w_gate": nrm(ks[22], (DEPTH, D_MODEL, D_FF), D_MODEL ** -0.5),
        "w_up": nrm(ks[23], (DEPTH, D_MODEL, D_FF), D_MODEL ** -0.5),
        "ffn_conv_w": nrm(ks[24], (DEPTH, FFN_CONV_W, D_FF), FFN_CONV_W ** -0.5),
        "ffn_conv_b": nrm(ks[25], (DEPTH, D_FF), 0.02),
        "w_down": nrm(ks[26], (DEPTH, D_FF, D_MODEL), D_FF ** -0.5),
        "final_norm": 1.0 + nrm(ks[27], (D_MODEL,), 0.02),
    }


def reference(x_prompt, x_sample, state_rglru_conv, state_rglru_h, state_gla_S, state_ffn_conv,
              meta_tokens, norm_mix, w_in, rg_conv_w, rg_conv_b, rg_wa, rg_ba, rg_wx, rg_bx, rg_lambda,
              rg_out_norm, gla_wa2, gla_ba2, gla_head_norm, w_out, norm_ffn, w_gate, w_up,
              ffn_conv_w, ffn_conv_b, w_down, final_norm):

    def run(x, conv_bufs, h0s, s0s, ffn_bufs, segments, reset_first):
        new_conv, new_h, new_s, new_ffn = [], [], [], []
        for l in range(DEPTH):
            x, c, hl, sl, f = layer(
                x, conv_bufs[l], h0s[l], s0s[l], ffn_bufs[l], segments, reset_first,
                norm_mix[l], w_in[l], rg_conv_w[l], rg_conv_b[l], rg_wa[l], rg_ba[l], rg_wx[l], rg_bx[l],
                rg_lambda[l], rg_out_norm[l], gla_wa2[l], gla_ba2[l], gla_head_norm[l], w_out[l],
                norm_ffn[l], w_gate[l], w_up[l], ffn_conv_w[l], ffn_conv_b[l], w_down[l])
            new_conv.append(c)
            new_h.append(hl)
            new_s.append(sl)
            new_ffn.append(f)
        return (rmsnorm(x, final_norm), jnp.stack(new_conv, 0), jnp.stack(new_h, 0),
                jnp.stack(new_s, 0), jnp.stack(new_ffn, 0))

    b = x_prompt.shape[0]
    dt = x_prompt.dtype
    meta = jnp.broadcast_to(meta_tokens.astype(dt)[None], (b, N_META, D_MODEL))
    xp = jnp.concatenate([meta, x_prompt], axis=1)
    zc = jnp.zeros((DEPTH, b, CONV_W - 1, D_RNN), dt)
    zh = jnp.zeros((DEPTH, b, D_RNN), dt)
    zs = jnp.zeros((DEPTH, b, GLA_HEADS, GLA_DK, GLA_DV), dt)
    zf = jnp.zeros((DEPTH, b, FFN_CONV_W - 1, D_FF), dt)
    seg_prompt = [(N_META, N_META), (x_prompt.shape[1], CHUNK)]
    yp, p_rg_conv, p_rg_h, p_gla_S, p_ffn_conv = run(xp, zc, zh, zs, zf, seg_prompt, True)
    y_prompt = yp[:, N_META:]

    seg_sample = [(x_sample.shape[1], x_sample.shape[1])]
    y_sample, s_rg_conv, s_rg_h, s_gla_S, s_ffn_conv = run(
        x_sample, state_rglru_conv, state_rglru_h, state_gla_S, state_ffn_conv, seg_sample, False)

    return (y_prompt, y_sample, p_rg_conv, p_rg_h, p_gla_S, p_ffn_conv,
            s_rg_conv, s_rg_h, s_gla_S, s_ffn_conv)
```

```python
import functools
import math

import jax
import jax.numpy as jnp
from jax import lax
from jax.experimental import pallas as pl
from jax.experimental.pallas import tpu as pltpu

F32 = jnp.float32
BF16 = jnp.bfloat16

D_MODEL = 4096
N_META = 16
D_RNN = 2048
D_GLA = 2048
RG_BLOCKS = 16
RG_BLOCK = 128
CONV_W = 4
RG_C = 8.0
GLA_HEADS = 16
GLA_DK = 128
GLA_RANK = 16
GLA_GATE_TAU = 16.0
D_FF = 3 * D_MODEL
FFN_CONV_W = 3
EPS = 1e-6
PROJ_COLS = 2 * D_RNN + 4 * D_GLA

LANES = 128
SUB = 8
VMEM_LIMIT = 56 * 1024 * 1024
GLA_SUB = 16
GLA_G = 4


def _params(sem):
    return pltpu.CompilerParams(dimension_semantics=sem, vmem_limit_bytes=VMEM_LIMIT)


def _gelu(x):
    c = math.sqrt(2.0 / math.pi)
    return 0.5 * x * (1.0 + jnp.tanh(c * (x + 0.044715 * (x * x * x))))


def _softplus(x):
    return jnp.maximum(x, 0.0) + jnp.log1p(jnp.exp(-jnp.abs(x)))


def _sigmoid(x):
    return 1.0 / (1.0 + jnp.exp(-x))


def _rmsnorm_kernel(x_ref, g_ref, o_ref):
    x = x_ref[...]
    ms = jnp.mean(x * x, axis=-1, keepdims=True)
    o_ref[...] = (x * lax.rsqrt(ms + EPS) * g_ref[...]).astype(o_ref.dtype)


def rmsnorm(x, g, out_dtype, tm):
    m, d = x.shape
    return pl.pallas_call(
        _rmsnorm_kernel,
        out_shape=jax.ShapeDtypeStruct((m, d), out_dtype),
        grid=(m // tm,),
        in_specs=[pl.BlockSpec((tm, d), lambda i: (i, 0)),
                  pl.BlockSpec((1, d), lambda i: (0, 0))],
        out_specs=pl.BlockSpec((tm, d), lambda i: (i, 0)),
        compiler_params=_params(("parallel",)),
        name="rmsnorm",
    )(x, g.reshape(1, d))


def _mm_kernel(x_ref, w_ref, *rest, nk, has_res):
    if has_res:
        r_ref, o_ref = rest
    else:
        (o_ref,) = rest
    part = jnp.dot(x_ref[...], w_ref[...], preferred_element_type=F32)
    if nk == 1:
        o_ref[...] = part + r_ref[...] if has_res else part
        return
    k = pl.program_id(2)

    @pl.when(k == 0)
    def _():
        o_ref[...] = part + r_ref[...] if has_res else part

    @pl.when(k > 0)
    def _():
        o_ref[...] += part


def matmul(x, w, *, tm, tn, tk, n_cols=None, res=None):
    m, kdim = x.shape
    n = w.shape[1] if n_cols is None else n_cols
    nk = kdim // tk
    in_specs = [pl.BlockSpec((tm, tk), lambda j, i, k: (i, k)),
                pl.BlockSpec((tk, tn), lambda j, i, k: (k, j))]
    args = [x, w]
    if res is not None:
        in_specs.append(pl.BlockSpec((tm, tn), lambda j, i, k: (i, j)))
        args.append(res)
    return pl.pallas_call(
        functools.partial(_mm_kernel, nk=nk, has_res=res is not None),
        out_shape=jax.ShapeDtypeStruct((m, n), F32),
        grid=(n // tn, m // tm, nk),
        in_specs=in_specs,
        out_specs=pl.BlockSpec((tm, tn), lambda j, i, k: (i, j)),
        compiler_params=_params(("parallel", "parallel", "arbitrary")),
        name="matmul",
    )(*args)


def _rg_gates(xc, n, wax_ref, ba_ref, bx_ref, sp_row):
    cs = slice(n * RG_BLOCK, (n + 1) * RG_BLOCK)
    pre = jnp.dot(xc.astype(BF16), wax_ref[n], preferred_element_type=F32)
    r = _sigmoid(pre[:, :RG_BLOCK] + ba_ref[:, cs])
    i = _sigmoid(pre[:, RG_BLOCK:] + bx_ref[:, cs])
    log_a = -RG_C * r * sp_row[:, cs]
    a = jnp.exp(log_a)
    th = jnp.tanh(log_a)
    mult = jnp.sqrt(-2.0 * th / (1.0 - th))
    return a, mult, i


def _rg_long_kernel(xr_ref, yr_ref, cb_ref, h0_ref, cw_ref, cbias_ref, wax_ref, ba_ref, bx_ref,
                    lam_ref, gn_ref, out_ref, nconv_ref, hlast_ref,
                    ext_scr, a_scr, u_scr, h_scr, *, tt, reset_first):
    t = pl.program_id(1)
    nt = pl.num_programs(1)
    pad = SUB
    tail = CONV_W - 1

    @pl.when(t == 0)
    def _():
        ext_scr[pad - tail:pad, :] = cb_ref[0]
        h_scr[...] = h0_ref[0]

    ext_scr[pad:pad + tt, :] = xr_ref[...]
    sp_row = _softplus(-lam_ref[...])
    row = lax.broadcasted_iota(jnp.int32, (tt, RG_BLOCK), 0)
    first_row = jnp.where(t == 0, 0, -1)
    for n in range(RG_BLOCKS):
        cs = slice(n * RG_BLOCK, (n + 1) * RG_BLOCK)
        xc = cbias_ref[:, cs]
        for i in range(CONV_W):
            lo = pad - tail + i
            xc = xc + ext_scr[lo:lo + tt, cs] * cw_ref[i:i + 1, cs]
        a, mult, gate_i = _rg_gates(xc, n, wax_ref, ba_ref, bx_ref, sp_row)
        if reset_first:
            mult = jnp.where(row == first_row, 1.0, mult)
        a_scr[:, cs] = a
        u_scr[:, cs] = mult * gate_i * xc

    row8 = lax.broadcasted_iota(jnp.int32, (SUB, 512), 0)
    ncol = D_RNN // 512

    def scan_body(j, hs):
        r0 = pl.multiple_of(j * SUB, SUB)
        new = []
        for c in range(ncol):
            cs = slice(c * 512, (c + 1) * 512)
            a = a_scr[pl.ds(r0, SUB), cs]
            u = u_scr[pl.ds(r0, SUB), cs]
            for s in (1, 2, 4):
                a_sh = jnp.where(row8 >= s, pltpu.roll(a, s, 0), 1.0)
                u_sh = jnp.where(row8 >= s, pltpu.roll(u, s, 0), 0.0)
                u = a * u_sh + u
                a = a * a_sh
            h = a * hs[c] + u
            u_scr[pl.ds(r0, SUB), cs] = h
            new.append(h[SUB - 1:SUB, :])
        return tuple(new)

    h_in = tuple(h_scr[:, c * 512:(c + 1) * 512] for c in range(ncol))
    h_fin = lax.fori_loop(0, tt // SUB, scan_body, h_in)
    for c in range(ncol):
        h_scr[:, c * 512:(c + 1) * 512] = h_fin[c]

    y = u_scr[...] * _gelu(yr_ref[...])
    ms = jnp.mean(y * y, axis=-1, keepdims=True)
    out_ref[...] = (y * lax.rsqrt(ms + EPS) * gn_ref[...]).astype(out_ref.dtype)
    ext_scr[pad - tail:pad, :] = ext_scr[pad + tt - tail:pad + tt, :]

    @pl.when(t == nt - 1)
    def _():
        nconv_ref[0] = ext_scr[pad + tt - tail:pad + tt, :]
        hlast_ref[0] = h_scr[...]


def rg_long(proj, conv_buf, h0, wts, *, batch, seq, tt, reset_first):
    nt = seq // tt
    bsel = (lambda b: b) if conv_buf.shape[0] == batch else (lambda b: 0)
    vec = lambda r: pl.BlockSpec((r, D_RNN), lambda b, t: (0, 0))
    return pl.pallas_call(
        functools.partial(_rg_long_kernel, tt=tt, reset_first=reset_first),
        out_shape=(jax.ShapeDtypeStruct((batch * seq, D_RNN), BF16),
                   jax.ShapeDtypeStruct((batch, CONV_W - 1, D_RNN), F32),
                   jax.ShapeDtypeStruct((batch, 1, D_RNN), F32)),
        grid=(batch, nt),
        in_specs=[pl.BlockSpec((tt, D_RNN), lambda b, t: (b * nt + t, 0)),
                  pl.BlockSpec((tt, D_RNN), lambda b, t: (b * nt + t, 1)),
                  pl.BlockSpec((1, CONV_W - 1, D_RNN), lambda b, t: (bsel(b), 0, 0)),
                  pl.BlockSpec((1, 1, D_RNN), lambda b, t: (bsel(b), 0, 0)),
                  vec(CONV_W), vec(1),
                  pl.BlockSpec((RG_BLOCKS, RG_BLOCK, 2 * RG_BLOCK), lambda b, t: (0, 0, 0)),
                  vec(1), vec(1), vec(1), vec(1)],
        out_specs=(pl.BlockSpec((tt, D_RNN), lambda b, t: (b * nt + t, 0)),
                   pl.BlockSpec((1, CONV_W - 1, D_RNN), lambda b, t: (b, 0, 0)),
                   pl.BlockSpec((1, 1, D_RNN), lambda b, t: (b, 0, 0))),
        scratch_shapes=[pltpu.VMEM((tt + SUB, D_RNN), F32),
                        pltpu.VMEM((tt, D_RNN), F32),
                        pltpu.VMEM((tt, D_RNN), F32),
                        pltpu.VMEM((1, D_RNN), F32)],
        compiler_params=_params(("parallel", "arbitrary")),
        name="rg_long",
    )(proj, proj, conv_buf, h0, wts["rg_conv_w"], wts["rg_conv_b"], wts["rg_wax"],
      wts["rg_ba"], wts["rg_bx"], wts["rg_lambda"], wts["rg_out_norm"])


def _rg_short_kernel(xr_ref, yr_ref, cb_ref, h0_ref, cw_ref, cbias_ref, wax_ref, ba_ref, bx_ref,
                     lam_ref, gn_ref, out_ref, nconv_ref, hlast_ref, y_scr, *, nb, seq):
    sp_row = _softplus(-lam_ref[...])
    tail = CONV_W - 1
    for n in range(RG_BLOCKS):
        cs = slice(n * RG_BLOCK, (n + 1) * RG_BLOCK)
        ext = [cb_ref[i, :, cs] for i in range(tail)]
        ext += [xr_ref[t * nb:(t + 1) * nb, cs] for t in range(seq)]
        h = h0_ref[:, cs]
        for t in range(seq):
            xc = cbias_ref[:, cs]
            for i in range(CONV_W):
                xc = xc + ext[t + i] * cw_ref[i:i + 1, cs]
            a, mult, gate_i = _rg_gates(xc, n, wax_ref, ba_ref, bx_ref, sp_row)
            h = a * h + mult * gate_i * xc
            y_scr[t, :, cs] = h * _gelu(yr_ref[t * nb:(t + 1) * nb, cs])
        hlast_ref[:, cs] = h
        for i in range(tail):
            nconv_ref[i, :, cs] = ext[seq + i]
    for t in range(seq):
        y = y_scr[t]
        ms = jnp.mean(y * y, axis=-1, keepdims=True)
        out_ref[t * nb:(t + 1) * nb, :] = (y * lax.rsqrt(ms + EPS) * gn_ref[...]).astype(out_ref.dtype)


def rg_short(proj, conv_buf_t, h0, wts, *, nb, seq):
    rows = nb * seq
    vec = lambda r: pl.BlockSpec((r, D_RNN), lambda i: (0, 0))
    return pl.pallas_call(
        functools.partial(_rg_short_kernel, nb=nb, seq=seq),
        out_shape=(jax.ShapeDtypeStruct((rows, D_RNN), BF16),
                   jax.ShapeDtypeStruct((CONV_W - 1, nb, D_RNN), F32),
                   jax.ShapeDtypeStruct((nb, D_RNN), F32)),
        grid=(1,),
        in_specs=[pl.BlockSpec((rows, D_RNN), lambda i: (0, 0)),
                  pl.BlockSpec((rows, D_RNN), lambda i: (0, 1)),
                  pl.BlockSpec((CONV_W - 1, nb, D_RNN), lambda i: (0, 0, 0)),
                  pl.BlockSpec((nb, D_RNN), lambda i: (0, 0)),
                  vec(CONV_W), vec(1),
                  pl.BlockSpec((RG_BLOCKS, RG_BLOCK, 2 * RG_BLOCK), lambda i: (0, 0, 0)),
                  vec(1), vec(1), vec(1), vec(1)],
        out_specs=(pl.BlockSpec((rows, D_RNN), lambda i: (0, 0)),
                   pl.BlockSpec((CONV_W - 1, nb, D_RNN), lambda i: (0, 0, 0)),
                   pl.BlockSpec((nb, D_RNN), lambda i: (0, 0))),
        scratch_shapes=[pltpu.VMEM((seq, nb, D_RNN), F32)],
        compiler_params=_params(("arbitrary",)),
        name="rg_short",
    )(proj, proj, conv_buf_t, h0, wts["rg_conv_w"], wts["rg_conv_b"], wts["rg_wax"],
      wts["rg_ba"], wts["rg_bx"], wts["rg_lambda"], wts["rg_out_norm"])


def _gla_consts():
    r = lax.broadcasted_iota(jnp.int32, (GLA_SUB, GLA_SUB), 0)
    c = lax.broadcasted_iota(jnp.int32, (GLA_SUB, GLA_SUB), 1)
    tri = jnp.where(c <= r, 1.0, 0.0).astype(BF16)
    er = lax.broadcasted_iota(jnp.int32, (LANES, LANES), 0)
    ec = lax.broadcasted_iota(jnp.int32, (LANES, LANES), 1)
    eye = jnp.where(er == ec, 1.0, 0.0).astype(BF16)
    ones = jnp.ones((LANES, LANES), BF16)
    row = lax.broadcasted_iota(jnp.int32, (GLA_SUB, LANES), 0)
    pr = lax.broadcasted_iota(jnp.int32, (2 * GLA_SUB, LANES), 0)
    pick = jnp.where((pr == GLA_SUB) | (pr == GLA_SUB + 1), 1.0, 0.0).astype(BF16)
    return tri, eye, ones, row, pick


def _gla_block(q, k, v, g, s, consts, j_range):
    tri, eye, ones, row, pick = consts
    g_hi = g.astype(BF16)
    g_lo = (g - g_hi.astype(F32)).astype(BF16)
    cum = (jnp.dot(tri, g_hi, preferred_element_type=F32)
           + jnp.dot(tri, g_lo, preferred_element_type=F32))
    last = cum[GLA_SUB - 1:GLA_SUB, :]
    qe = (q * jnp.exp(cum)).astype(BF16)
    o = jnp.dot(qe, s.astype(BF16), preferred_element_type=F32)
    ps = []
    for j in j_range:
        w = jnp.exp(jnp.where(row >= j, cum - cum[j:j + 1, :], -jnp.inf))
        ps.append((q * w * k[j:j + 1, :]).astype(BF16))
    att = jnp.dot(jnp.concatenate(ps, axis=0), ones, preferred_element_type=F32)
    for idx, j in enumerate(j_range):
        o = o + att[idx * GLA_SUB:(idx + 1) * GLA_SUB, :] * v[j:j + 1, :]
    ke = k * jnp.exp(last - cum)
    el = jnp.exp(last)
    e_hi = el.astype(BF16).astype(F32)
    e_lo = el - e_hi
    extra = jnp.where(row == 0, e_hi, jnp.where(row == 1, e_lo, 0.0))
    x = jnp.concatenate([ke, extra], axis=0).astype(BF16)
    xt = lax.dot_general(eye, x, (((1,), (1,)), ((), ())),
                         preferred_element_type=F32).astype(BF16)
    vpad = jnp.concatenate([v, jnp.zeros_like(v)], axis=0).astype(BF16)
    kv = jnp.dot(xt, vpad, preferred_element_type=F32)
    dec = jnp.dot(xt, pick, preferred_element_type=F32)
    return o, dec * s + kv


def _gla_log_alpha(z_ref, wa2_ref, ba2_ref):
    zz = jnp.dot(z_ref[...].astype(BF16), wa2_ref[...], preferred_element_type=F32) + ba2_ref[...]
    return (jnp.minimum(zz, 0.0) - jnp.log1p(jnp.exp(-jnp.abs(zz)))) * (1.0 / GLA_GATE_TAU)


def _gla_finish(o, go, gain):
    ms = jnp.mean(o * o, axis=-1, keepdims=True)
    on = o * lax.rsqrt(ms + EPS) * gain
    return on * (go * _sigmoid(go))


def _gla_long_kernel(q_ref, k_ref, v_ref, go_ref, z_ref, wa2_ref, ba2_ref, gain_ref, s0_ref,
                     out_ref, sout_ref, s_scr, gl_scr, o_scr, *, tt):
    t = pl.program_id(2)
    nt = pl.num_programs(2)
    consts = _gla_consts()
    scale = GLA_DK ** -0.5

    @pl.when(t == 0)
    def _():
        s_scr[...] = s0_ref[0]

    gl_scr[...] = _gla_log_alpha(z_ref, wa2_ref, ba2_ref)

    def body(i, carry):
        r0 = pl.multiple_of(i * GLA_SUB, GLA_SUB)
        rows = pl.ds(r0, GLA_SUB)
        for h in range(GLA_G):
            cs = slice(h * LANES, (h + 1) * LANES)
            o, s_new = _gla_block(q_ref[rows, cs] * scale, k_ref[rows, cs], v_ref[rows, cs],
                                  gl_scr[rows, cs], s_scr[h], consts, range(GLA_SUB))
            s_scr[h] = s_new
            o_scr[rows, cs] = o
        return carry

    lax.fori_loop(0, tt // GLA_SUB, body, 0)
    for h in range(GLA_G):
        cs = slice(h * LANES, (h + 1) * LANES)
        out_ref[:, cs] = _gla_finish(o_scr[:, cs], go_ref[:, cs], gain_ref[:, cs]).astype(out_ref.dtype)

    @pl.when(t == nt - 1)
    def _():
        sout_ref[0] = s_scr[...]


def gla_long(proj, z, s0, wts, *, batch, seq, tt):
    nt = seq // tt
    gw = GLA_G * LANES
    base = 2 * D_RNN // gw
    per = D_GLA // gw
    bsel = (lambda b: b) if s0.shape[0] == batch else (lambda b: 0)
    col = lambda which: pl.BlockSpec((tt, gw), lambda b, h, t: (b * nt + t, base + which * per + h))
    return pl.pallas_call(
        functools.partial(_gla_long_kernel, tt=tt),
        out_shape=(jax.ShapeDtypeStruct((batch * seq, D_GLA), BF16),
                   jax.ShapeDtypeStruct((batch, GLA_HEADS, GLA_DK, GLA_DK), F32)),
        grid=(batch, GLA_HEADS // GLA_G, nt),
        in_specs=[col(0), col(1), col(2), col(3),
                  pl.BlockSpec((tt, LANES), lambda b, h, t: (b * nt + t, 0)),
                  pl.BlockSpec((LANES, gw), lambda b, h, t: (0, h)),
                  pl.BlockSpec((1, gw), lambda b, h, t: (0, h)),
                  pl.BlockSpec((1, gw), lambda b, h, t: (0, h)),
                  pl.BlockSpec((1, GLA_G, GLA_DK, GLA_DK), lambda b, h, t: (bsel(b), h, 0, 0))],
        out_specs=(pl.BlockSpec((tt, gw), lambda b, h, t: (b * nt + t, h)),
                   pl.BlockSpec((1, GLA_G, GLA_DK, GLA_DK), lambda b, h, t: (b, h, 0, 0))),
        scratch_shapes=[pltpu.VMEM((GLA_G, GLA_DK, GLA_DK), F32),
                        pltpu.VMEM((tt, gw), F32),
                        pltpu.VMEM((tt, gw), F32)],
        compiler_params=_params(("parallel", "parallel", "arbitrary")),
        name="gla_long",
    )(proj, proj, proj, proj, z, wts["gla_wa2"], wts["gla_ba2"], wts["gla_head_norm"], s0)


def _gla_short_kernel(q_ref, k_ref, v_ref, go_ref, z_ref, wa2_ref, ba2_ref, gain_ref, s0_ref,
                      out_ref, sout_ref, *, seq):
    consts = _gla_consts()
    row = consts[3]
    scale = GLA_DK ** -0.5
    gl = _gla_log_alpha(z_ref, wa2_ref, ba2_ref)
    for h in range(GLA_G):
        cs = slice(h * LANES, (h + 1) * LANES)
        q = q_ref[:, cs] * scale
        k = k_ref[:, cs]
        v = v_ref[:, cs]
        g = gl[:, cs]
        o = jnp.zeros((GLA_SUB, LANES), F32)
        for b in range(GLA_SUB // seq):
            mine = (row >= b * seq) & (row < (b + 1) * seq)
            zero = lambda a: jnp.where(mine, a, 0.0)
            o_b, s_new = _gla_block(zero(q), zero(k), zero(v), zero(g), s0_ref[b, h], consts,
                                    range(b * seq, (b + 1) * seq))
            sout_ref[b, h] = s_new
            o = o + o_b
        out_ref[:, cs] = _gla_finish(o, go_ref[:, cs], gain_ref[:, cs]).astype(out_ref.dtype)


def gla_short(proj, z, s0, wts, *, nb, seq):
    gw = GLA_G * LANES
    per = D_GLA // gw
    bpb = GLA_SUB // seq
    col = lambda which: pl.BlockSpec((GLA_SUB, gw), lambda i, h: (i, which * per + h))
    return pl.pallas_call(
        functools.partial(_gla_short_kernel, seq=seq),
        out_shape=(jax.ShapeDtypeStruct((nb * seq, D_GLA), BF16),
                   jax.ShapeDtypeStruct((nb, GLA_HEADS, GLA_DK, GLA_DK), F32)),
        grid=(nb // bpb, GLA_HEADS // GLA_G),
        in_specs=[col(0), col(1), col(2), col(3),
                  pl.BlockSpec((GLA_SUB, LANES), lambda i, h: (i, 0)),
                  pl.BlockSpec((LANES, gw), lambda i, h: (0, h)),
                  pl.BlockSpec((1, gw), lambda i, h: (0, h)),
                  pl.BlockSpec((1, gw), lambda i, h: (0, h)),
                  pl.BlockSpec((bpb, GLA_G, GLA_DK, GLA_DK), lambda i, h: (i, h, 0, 0))],
        out_specs=(pl.BlockSpec((GLA_SUB, gw), lambda i, h: (i, h)),
                   pl.BlockSpec((bpb, GLA_G, GLA_DK, GLA_DK), lambda i, h: (i, h, 0, 0))),
        compiler_params=_params(("parallel", "parallel")),
        name="gla_short",
    )(proj, proj, proj, proj, z, wts["gla_wa2"], wts["gla_ba2"], wts["gla_head_norm"], s0)


def _ffn_long_kernel(g_ref, u_ref, buf_ref, cw_ref, cb_ref, out_ref, nbuf_ref, ext_scr, *, tt):
    t = pl.program_id(2)
    nt = pl.num_programs(2)
    pad = SUB
    tail = FFN_CONV_W - 1

    @pl.when(t == 0)
    def _():
        ext_scr[pad - tail:pad, :] = buf_ref[0]

    ext_scr[pad:pad + tt, :] = g_ref[...]
    gc = cb_ref[...]
    for i in range(FFN_CONV_W):
        lo = pad - tail + i
        gc = gc + ext_scr[lo:lo + tt, :] * cw_ref[i:i + 1, :]
    out_ref[...] = (_gelu(gc) * u_ref[...]).astype(out_ref.dtype)
    ext_scr[pad - tail:pad, :] = ext_scr[pad + tt - tail:pad + tt, :]

    @pl.when(t == nt - 1)
    def _():
        nbuf_ref[0] = ext_scr[pad + tt - tail:pad + tt, :]


def ffn_long(gate, up, buf, wts, *, batch, seq, tt, tc):
    nt = seq // tt
    bsel = (lambda b: b) if buf.shape[0] == batch else (lambda b: 0)
    tail = FFN_CONV_W - 1
    return pl.pallas_call(
        functools.partial(_ffn_long_kernel, tt=tt),
        out_shape=(jax.ShapeDtypeStruct((batch * seq, D_FF), BF16),
                   jax.ShapeDtypeStruct((batch, tail, D_FF), F32)),
        grid=(batch, D_FF // tc, nt),
        in_specs=[pl.BlockSpec((tt, tc), lambda b, c, t: (b * nt + t, c)),
                  pl.BlockSpec((tt, tc), lambda b, c, t: (b * nt + t, c)),
                  pl.BlockSpec((1, tail, tc), lambda b, c, t: (bsel(b), 0, c)),
                  pl.BlockSpec((FFN_CONV_W, tc), lambda b, c, t: (0, c)),
                  pl.BlockSpec((1, tc), lambda b, c, t: (0, c))],
        out_specs=(pl.BlockSpec((tt, tc), lambda b, c, t: (b * nt + t, c)),
                   pl.BlockSpec((1, tail, tc), lambda b, c, t: (b, 0, c))),
        scratch_shapes=[pltpu.VMEM((tt + SUB, tc), F32)],
        compiler_params=_params(("parallel", "parallel", "arbitrary")),
        name="ffn_long",
    )(gate, up, buf, wts["ffn_conv_w"], wts["ffn_conv_b"])


def _ffn_short_kernel(g_ref, u_ref, buf_ref, cw_ref, cb_ref, out_ref, nbuf_ref, *, nb, seq):
    tail = FFN_CONV_W - 1
    ext = [buf_ref[i] for i in range(tail)]
    ext += [g_ref[t * nb:(t + 1) * nb, :] for t in range(seq)]
    for t in range(seq):
        gc = cb_ref[...]
        for i in range(FFN_CONV_W):
            gc = gc + ext[t + i] * cw_ref[i:i + 1, :]
        out_ref[t * nb:(t + 1) * nb, :] = (_gelu(gc) * u_ref[t * nb:(t + 1) * nb, :]).astype(out_ref.dtype)
    for i in range(tail):
        nbuf_ref[i] = ext[seq + i]


def ffn_short(gate, up, buf_t, wts, *, nb, seq, tc):
    rows = nb * seq
    tail = FFN_CONV_W - 1
    return pl.pallas_call(
        functools.partial(_ffn_short_kernel, nb=nb, seq=seq),
        out_shape=(jax.ShapeDtypeStruct((rows, D_FF), BF16),
                   jax.ShapeDtypeStruct((tail, nb, D_FF), F32)),
        grid=(D_FF // tc,),
        in_specs=[pl.BlockSpec((rows, tc), lambda c: (0, c)),
                  pl.BlockSpec((rows, tc), lambda c: (0, c)),
                  pl.BlockSpec((tail, nb, tc), lambda c: (0, 0, c)),
                  pl.BlockSpec((FFN_CONV_W, tc), lambda c: (0, c)),
                  pl.BlockSpec((1, tc), lambda c: (0, c))],
        out_specs=(pl.BlockSpec((rows, tc), lambda c: (0, c)),
                   pl.BlockSpec((tail, nb, tc), lambda c: (0, 0, c))),
        compiler_params=_params(("parallel",)),
        name="ffn_short",
    )(gate, up, buf_t, wts["ffn_conv_w"], wts["ffn_conv_b"])


def _layer(x, wts, mixers, *, tm):
    xn = rmsnorm(x, wts["norm_mix"], BF16, min(tm, 256))
    proj = matmul(xn, wts["w_in"], tm=tm, tn=1024, tk=D_MODEL, n_cols=PROJ_COLS)
    z = matmul(xn, wts["w_z"], tm=tm, tn=LANES, tk=D_MODEL)
    mix, ffn_fn, states = mixers(proj, z)
    x1 = matmul(mix, wts["w_out"], tm=tm, tn=1024, tk=D_MODEL, res=x)
    xn2 = rmsnorm(x1, wts["norm_ffn"], BF16, min(tm, 256))
    gate = matmul(xn2, wts["w_gate"], tm=tm, tn=1024, tk=D_MODEL)
    up = matmul(xn2, wts["w_up"], tm=tm, tn=1024, tk=D_MODEL)
    hid, new_ffn = ffn_fn(gate, up)
    x2 = matmul(hid, wts["w_down"], tm=tm, tn=1024, tk=2048, res=x1)
    y = rmsnorm(x2, wts["final_norm"], F32, min(tm, 256))
    return y, states, new_ffn


def kernel(x_prompt, x_sample, state_rglru_conv, state_rglru_h, state_gla_S, state_ffn_conv,
           meta_tokens, norm_mix, w_in, rg_conv_w, rg_conv_b, rg_wa, rg_ba, rg_wx, rg_bx, rg_lambda,
           rg_out_norm, gla_wa2, gla_ba2, gla_head_norm, w_out, norm_ffn, w_gate, w_up,
           ffn_conv_w, ffn_conv_b, w_down, final_norm):
    batch, seq, _ = x_prompt.shape
    nb, sseq, _ = x_sample.shape
    row = lambda a: a.reshape(1, -1)
    w_in_b = w_in[0].astype(BF16)
    wts = {
        "norm_mix": norm_mix[0], "norm_ffn": norm_ffn[0], "final_norm": final_norm,
        "w_in": w_in_b,
        "w_z": jnp.pad(w_in_b[:, PROJ_COLS:], ((0, 0), (0, LANES - GLA_RANK))),
        "w_out": w_out[0].astype(BF16), "w_gate": w_gate[0].astype(BF16),
        "w_up": w_up[0].astype(BF16), "w_down": w_down[0].astype(BF16),
        "rg_conv_w": rg_conv_w[0], "rg_conv_b": row(rg_conv_b[0]),
        "rg_wax": jnp.concatenate([rg_wa[0], rg_wx[0]], axis=-1).astype(BF16),
        "rg_ba": row(rg_ba[0]), "rg_bx": row(rg_bx[0]), "rg_lambda": row(rg_lambda[0]),
        "rg_out_norm": row(rg_out_norm[0]),
        "gla_wa2": jnp.pad(gla_wa2[0].astype(BF16), ((0, LANES - GLA_RANK), (0, 0))),
        "gla_ba2": row(gla_ba2[0]), "gla_head_norm": row(gla_head_norm[0]),
        "ffn_conv_w": ffn_conv_w[0], "ffn_conv_b": row(ffn_conv_b[0]),
    }

    def long_mixers(nbatch, nseq, tt, conv_buf, h0, s0, ffn_buf, reset_first):
        def run(proj, z):
            rnn, nconv, hlast = rg_long(proj, conv_buf, h0, wts, batch=nbatch, seq=nseq, tt=tt,
                                        reset_first=reset_first)
            gla, s_last = gla_long(proj, z, s0, wts, batch=nbatch, seq=nseq, tt=tt)
            ffn_fn = lambda gate, up: ffn_long(gate, up, ffn_buf, wts, batch=nbatch, seq=nseq,
                                               tt=tt, tc=2048)
            return jnp.concatenate([rnn, gla], axis=-1), ffn_fn, (nconv, hlast, s_last)
        return run

    zeros = lambda *s: jnp.zeros(s, F32)
    _, (m_conv, m_h, m_s), m_ffn = _layer(
        meta_tokens, wts,
        long_mixers(1, N_META, N_META, zeros(1, CONV_W - 1, D_RNN), zeros(1, 1, D_RNN),
                    zeros(1, GLA_HEADS, GLA_DK, GLA_DK), zeros(1, FFN_CONV_W - 1, D_FF), True),
        tm=N_META)

    y_p, (p_conv, p_h, p_s), p_ffn = _layer(
        x_prompt.reshape(batch * seq, D_MODEL), wts,
        long_mixers(batch, seq, 256, m_conv, m_h, m_s, m_ffn, False), tm=1024)

    def to_batch_major(a):
        return jnp.swapaxes(a.reshape(sseq, nb, -1), 0, 1).reshape(nb * sseq, -1)

    def to_time_major(a):
        return jnp.swapaxes(a.reshape(nb, sseq, -1), 0, 1).reshape(nb * sseq, -1)

    def short_mixers(proj, z):
        rnn, nconv_t, hlast = rg_short(proj, jnp.swapaxes(state_rglru_conv[0], 0, 1),
                                       state_rglru_h[0], wts, nb=nb, seq=sseq)
        gla_b, s_last = gla_short(to_batch_major(proj[:, 2 * D_RNN:]), to_batch_major(z),
                                  state_gla_S[0], wts, nb=nb, seq=sseq)

        def ffn_fn(gate, up):
            hid, nbuf_t = ffn_short(gate, up, jnp.swapaxes(state_ffn_conv[0], 0, 1), wts,
                                    nb=nb, seq=sseq, tc=1024)
            return hid, jnp.swapaxes(nbuf_t, 0, 1)
        mix = jnp.concatenate([rnn, to_time_major(gla_b)], axis=-1)
        return mix, ffn_fn, (jnp.swapaxes(nconv_t, 0, 1), hlast, s_last)

    y_s, (s_conv, s_h, s_s), s_ffn = _layer(to_time_major(x_sample), wts, short_mixers,
                                            tm=nb * sseq)
    y_s = to_batch_major(y_s)

    return (y_p.reshape(batch, seq, D_MODEL), y_s.reshape(nb, sseq, D_MODEL),
            p_conv[None], p_h.reshape(1, batch, D_RNN), p_s[None], p_ffn[None],
            s_conv[None], s_h[None], s_s[None], s_ffn[None])
```

```python
import functools
import math

import jax
import jax.numpy as jnp
from jax import lax
from jax.experimental import pallas as pl
from jax.experimental.pallas import tpu as pltpu

F32 = jnp.float32
BF16 = jnp.bfloat16

D_MODEL = 4096
N_META = 16
D_RNN = 2048
D_GLA = 2048
RG_BLOCKS = 16
RG_BLOCK = 128
CONV_W = 4
RG_C = 8.0
GLA_HEADS = 16
GLA_DK = 128
GLA_RANK = 16
GLA_GATE_TAU = 16.0
D_FF = 3 * D_MODEL
FFN_CONV_W = 3
EPS = 1e-6
PROJ_COLS = 2 * D_RNN + 4 * D_GLA

LANES = 128
SUB = 8
VMEM_LIMIT = 56 * 1024 * 1024
GLA_SUB = 16
GLA_G = 4


def _params(sem):
    return pltpu.CompilerParams(dimension_semantics=sem, vmem_limit_bytes=VMEM_LIMIT)


def _gelu(x):
    c = math.sqrt(2.0 / math.pi)
    return 0.5 * x * (1.0 + jnp.tanh(c * (x + 0.044715 * (x * x * x))))


def _softplus(x):
    return jnp.maximum(x, 0.0) + jnp.log1p(jnp.exp(-jnp.abs(x)))


def _sigmoid(x):
    return 1.0 / (1.0 + jnp.exp(-x))


def _rmsnorm_kernel(x_ref, g_ref, o_ref):
    x = x_ref[...]
    ms = jnp.mean(x * x, axis=-1, keepdims=True)
    o_ref[...] = (x * lax.rsqrt(ms + EPS) * g_ref[...]).astype(o_ref.dtype)


def rmsnorm(x, g, out_dtype, tm):
    m, d = x.shape
    return pl.pallas_call(
        _rmsnorm_kernel,
        out_shape=jax.ShapeDtypeStruct((m, d), out_dtype),
        grid=(m // tm,),
        in_specs=[pl.BlockSpec((tm, d), lambda i: (i, 0)),
                  pl.BlockSpec((1, d), lambda i: (0, 0))],
        out_specs=pl.BlockSpec((tm, d), lambda i: (i, 0)),
        compiler_params=_params(("parallel",)),
        name="rmsnorm",
    )(x, g.reshape(1, d))


def _mm_kernel(x_ref, w_ref, *rest, nk, has_res):
    if has_res:
        r_ref, o_ref = rest
    else:
        (o_ref,) = rest
    part = jnp.dot(x_ref[...], w_ref[...], preferred_element_type=F32)
    if nk == 1:
        o_ref[...] = part + r_ref[...] if has_res else part
        return
    k = pl.program_id(2)

    @pl.when(k == 0)
    def _():
        o_ref[...] = part + r_ref[...] if has_res else part

    @pl.when(k > 0)
    def _():
        o_ref[...] += part


def matmul(x, w, *, tm, tn, tk, n_cols=None, res=None):
    m, kdim = x.shape
    n = w.shape[1] if n_cols is None else n_cols
    nk = kdim // tk
    in_specs = [pl.BlockSpec((tm, tk), lambda j, i, k: (i, k)),
                pl.BlockSpec((tk, tn), lambda j, i, k: (k, j))]
    args = [x, w]
    if res is not None:
        in_specs.append(pl.BlockSpec((tm, tn), lambda j, i, k: (i, j)))
        args.append(res)
    return pl.pallas_call(
        functools.partial(_mm_kernel, nk=nk, has_res=res is not None),
        out_shape=jax.ShapeDtypeStruct((m, n), F32),
        grid=(n // tn, m // tm, nk),
        in_specs=in_specs,
        out_specs=pl.BlockSpec((tm, tn), lambda j, i, k: (i, j)),
        compiler_params=_params(("parallel", "parallel", "arbitrary")),
        name="matmul",
    )(*args)


def _rg_gates(xc, n, wax_ref, ba_ref, bx_ref, sp_row):
    cs = slice(n * RG_BLOCK, (n + 1) * RG_BLOCK)
    pre = jnp.dot(xc.astype(BF16), wax_ref[n], preferred_element_type=F32)
    r = _sigmoid(pre[:, :RG_BLOCK] + ba_ref[:, cs])
    i = _sigmoid(pre[:, RG_BLOCK:] + bx_ref[:, cs])
    log_a = -RG_C * r * sp_row[:, cs]
    a = jnp.exp(log_a)
    th = jnp.tanh(log_a)
    mult = jnp.sqrt(-2.0 * th / (1.0 - th))
    return a, mult, i


def _rg_long_kernel(xr_ref, yr_ref, cb_ref, h0_ref, cw_ref, cbias_ref, wax_ref, ba_ref, bx_ref,
                    lam_ref, gn_ref, out_ref, nconv_ref, hlast_ref,
                    ext_scr, a_scr, u_scr, h_scr, *, tt, reset_first):
    t = pl.program_id(1)
    nt = pl.num_programs(1)
    pad = SUB
    tail = CONV_W - 1

    @pl.when(t == 0)
    def _():
        ext_scr[pad - tail:pad, :] = cb_ref[0]
        h_scr[...] = h0_ref[0]

    ext_scr[pad:pad + tt, :] = xr_ref[...]
    sp_row = _softplus(-lam_ref[...])
    row = lax.broadcasted_iota(jnp.int32, (tt, RG_BLOCK), 0)
    first_row = jnp.where(t == 0, 0, -1)
    for n in range(RG_BLOCKS):
        cs = slice(n * RG_BLOCK, (n + 1) * RG_BLOCK)
        xc = cbias_ref[:, cs]
        for i in range(CONV_W):
            lo = pad - tail + i
            xc = xc + ext_scr[lo:lo + tt, cs] * cw_ref[i:i + 1, cs]
        a, mult, gate_i = _rg_gates(xc, n, wax_ref, ba_ref, bx_ref, sp_row)
        if reset_first:
            mult = jnp.where(row == first_row, 1.0, mult)
        a_scr[:, cs] = a
        u_scr[:, cs] = mult * gate_i * xc

    row8 = lax.broadcasted_iota(jnp.int32, (SUB, 512), 0)
    ncol = D_RNN // 512

    def scan_body(j, hs):
        r0 = pl.multiple_of(j * SUB, SUB)
        new = []
        for c in range(ncol):
            cs = slice(c * 512, (c + 1) * 512)
            a = a_scr[pl.ds(r0, SUB), cs]
            u = u_scr[pl.ds(r0, SUB), cs]
            for s in (1, 2, 4):
                a_sh = jnp.where(row8 >= s, pltpu.roll(a, s, 0), 1.0)
                u_sh = jnp.where(row8 >= s, pltpu.roll(u, s, 0), 0.0)
                u = a * u_sh + u
                a = a * a_sh
            h = a * hs[c] + u
            u_scr[pl.ds(r0, SUB), cs] = h
            new.append(h[SUB - 1:SUB, :])
        return tuple(new)

    h_in = tuple(h_scr[:, c * 512:(c + 1) * 512] for c in range(ncol))
    h_fin = lax.fori_loop(0, tt // SUB, scan_body, h_in)
    for c in range(ncol):
        h_scr[:, c * 512:(c + 1) * 512] = h_fin[c]

    y = u_scr[...] * _gelu(yr_ref[...])
    ms = jnp.mean(y * y, axis=-1, keepdims=True)
    out_ref[...] = (y * lax.rsqrt(ms + EPS) * gn_ref[...]).astype(out_ref.dtype)
    ext_scr[pad - tail:pad, :] = ext_scr[pad + tt - tail:pad + tt, :]

    @pl.when(t == nt - 1)
    def _():
        nconv_ref[0] = ext_scr[pad + tt - tail:pad + tt, :]
        hlast_ref[0] = h_scr[...]


def rg_long(proj, conv_buf, h0, wts, *, batch, seq, tt, reset_first):
    nt = seq // tt
    bsel = (lambda b: b) if conv_buf.shape[0] == batch else (lambda b: 0)
    vec = lambda r: pl.BlockSpec((r, D_RNN), lambda b, t: (0, 0))
    return pl.pallas_call(
        functools.partial(_rg_long_kernel, tt=tt, reset_first=reset_first),
        out_shape=(jax.ShapeDtypeStruct((batch * seq, D_RNN), BF16),
                   jax.ShapeDtypeStruct((batch, CONV_W - 1, D_RNN), F32),
                   jax.ShapeDtypeStruct((batch, 1, D_RNN), F32)),
        grid=(batch, nt),
        in_specs=[pl.BlockSpec((tt, D_RNN), lambda b, t: (b * nt + t, 0)),
                  pl.BlockSpec((tt, D_RNN), lambda b, t: (b * nt + t, 1)),
                  pl.BlockSpec((1, CONV_W - 1, D_RNN), lambda b, t: (bsel(b), 0, 0)),
                  pl.BlockSpec((1, 1, D_RNN), lambda b, t: (bsel(b), 0, 0)),
                  vec(CONV_W), vec(1),
                  pl.BlockSpec((RG_BLOCKS, RG_BLOCK, 2 * RG_BLOCK), lambda b, t: (0, 0, 0)),
                  vec(1), vec(1), vec(1), vec(1)],
        out_specs=(pl.BlockSpec((tt, D_RNN), lambda b, t: (b * nt + t, 0)),
                   pl.BlockSpec((1, CONV_W - 1, D_RNN), lambda b, t: (b, 0, 0)),
                   pl.BlockSpec((1, 1, D_RNN), lambda b, t: (b, 0, 0))),
        scratch_shapes=[pltpu.VMEM((tt + SUB, D_RNN), F32),
                        pltpu.VMEM((tt, D_RNN), F32),
                        pltpu.VMEM((tt, D_RNN), F32),
                        pltpu.VMEM((1, D_RNN), F32)],
        compiler_params=_params(("parallel", "arbitrary")),
        name="rg_long",
    )(proj, proj, conv_buf, h0, wts["rg_conv_w"], wts["rg_conv_b"], wts["rg_wax"],
      wts["rg_ba"], wts["rg_bx"], wts["rg_lambda"], wts["rg_out_norm"])


def _rg_short_kernel(xr_ref, yr_ref, cb_ref, h0_ref, cw_ref, cbias_ref, wax_ref, ba_ref, bx_ref,
                     lam_ref, gn_ref, out_ref, nconv_ref, hlast_ref, y_scr, *, nb, seq):
    sp_row = _softplus(-lam_ref[...])
    tail = CONV_W - 1
    for n in range(RG_BLOCKS):
        cs = slice(n * RG_BLOCK, (n + 1) * RG_BLOCK)
        ext = [cb_ref[i, :, cs] for i in range(tail)]
        ext += [xr_ref[t * nb:(t + 1) * nb, cs] for t in range(seq)]
        h = h0_ref[:, cs]
        for t in range(seq):
            xc = cbias_ref[:, cs]
            for i in range(CONV_W):
                xc = xc + ext[t + i] * cw_ref[i:i + 1, cs]
            a, mult, gate_i = _rg_gates(xc, n, wax_ref, ba_ref, bx_ref, sp_row)
            h = a * h + mult * gate_i * xc
            y_scr[t, :, cs] = h * _gelu(yr_ref[t * nb:(t + 1) * nb, cs])
        hlast_ref[:, cs] = h
        for i in range(tail):
            nconv_ref[i, :, cs] = ext[seq + i]
    for t in range(seq):
        y = y_scr[t]
        ms = jnp.mean(y * y, axis=-1, keepdims=True)
        out_ref[t * nb:(t + 1) * nb, :] = (y * lax.rsqrt(ms + EPS) * gn_ref[...]).astype(out_ref.dtype)


def rg_short(proj, conv_buf_t, h0, wts, *, nb, seq):
    rows = nb * seq
    vec = lambda r: pl.BlockSpec((r, D_RNN), lambda i: (0, 0))
    return pl.pallas_call(
        functools.partial(_rg_short_kernel, nb=nb, seq=seq),
        out_shape=(jax.ShapeDtypeStruct((rows, D_RNN), BF16),
                   jax.ShapeDtypeStruct((CONV_W - 1, nb, D_RNN), F32),
                   jax.ShapeDtypeStruct((nb, D_RNN), F32)),
        grid=(1,),
        in_specs=[pl.BlockSpec((rows, D_RNN), lambda i: (0, 0)),
                  pl.BlockSpec((rows, D_RNN), lambda i: (0, 1)),
                  pl.BlockSpec((CONV_W - 1, nb, D_RNN), lambda i: (0, 0, 0)),
                  pl.BlockSpec((nb, D_RNN), lambda i: (0, 0)),
                  vec(CONV_W), vec(1),
                  pl.BlockSpec((RG_BLOCKS, RG_BLOCK, 2 * RG_BLOCK), lambda i: (0, 0, 0)),
                  vec(1), vec(1), vec(1), vec(1)],
        out_specs=(pl.BlockSpec((rows, D_RNN), lambda i: (0, 0)),
                   pl.BlockSpec((CONV_W - 1, nb, D_RNN), lambda i: (0, 0, 0)),
                   pl.BlockSpec((nb, D_RNN), lambda i: (0, 0))),
        scratch_shapes=[pltpu.VMEM((seq, nb, D_RNN), F32)],
        compiler_params=_params(("arbitrary",)),
        name="rg_short",
    )(proj, proj, conv_buf_t, h0, wts["rg_conv_w"], wts["rg_conv_b"], wts["rg_wax"],
      wts["rg_ba"], wts["rg_bx"], wts["rg_lambda"], wts["rg_out_norm"])


def _gla_consts():
    er = lax.broadcasted_iota(jnp.int32, (LANES, LANES), 0)
    ec = lax.broadcasted_iota(jnp.int32, (LANES, LANES), 1)
    eye = jnp.where(er == ec, 1.0, 0.0).astype(BF16)
    ones = jnp.ones((LANES, LANES), BF16)
    row = lax.broadcasted_iota(jnp.int32, (GLA_SUB, LANES), 0)
    row8 = lax.broadcasted_iota(jnp.int32, (SUB, LANES), 0)
    pr = lax.broadcasted_iota(jnp.int32, (2 * GLA_SUB, LANES), 0)
    pick = jnp.where((pr == GLA_SUB) | (pr == GLA_SUB + 1), 1.0, 0.0).astype(BF16)
    return eye, ones, row, row8, pick


def _cumsum_rows(g, row8):
    def scan8(x):
        for s in (1, 2, 4):
            x = x + jnp.where(row8 >= s, pltpu.roll(x, s, 0), 0.0)
        return x
    lo = scan8(g[:SUB])
    hi = scan8(g[SUB:]) + lo[SUB - 1:SUB, :]
    return jnp.concatenate([lo, hi], axis=0)


def _gla_local(units, consts):
    eye, ones, row, row8, pick = consts
    assert len(units) == GLA_G
    qes, ps, xs = [], [], []
    for q, k, v, g, j_range in units:
        cum = _cumsum_rows(g, row8)
        last = cum[GLA_SUB - 1:GLA_SUB, :]
        qes.append((q * jnp.exp(cum)).astype(BF16))
        for j in j_range:
            w = jnp.exp(jnp.where(row >= j, cum - cum[j:j + 1, :], -jnp.inf))
            ps.append((q * w * k[j:j + 1, :]).astype(BF16))
        ke = k * jnp.exp(last - cum)
        el = jnp.exp(last)
        e_hi = el.astype(BF16).astype(F32)
        e_lo = el - e_hi
        extra = jnp.where(row == 0, e_hi, jnp.where(row == 1, e_lo, 0.0))
        xs.append(jnp.concatenate([ke, extra], axis=0).astype(BF16))
    att = jnp.dot(jnp.concatenate(ps, axis=0), ones, preferred_element_type=F32)
    xt = lax.dot_general(eye, jnp.concatenate(xs, axis=0), (((1,), (1,)), ((), ())),
                         preferred_element_type=F32).astype(BF16)
    zblk = jnp.zeros((2 * GLA_SUB, LANES), BF16)
    wrows = []
    for u, (_, _, v, _, _) in enumerate(units):
        vpad = jnp.concatenate([v, jnp.zeros_like(v)], axis=0).astype(BF16)
        wrows.append(jnp.concatenate([zblk] * (2 * u) + [vpad, pick]
                                     + [zblk] * (2 * (GLA_G - 1 - u)), axis=1))
    kd = jnp.dot(xt, jnp.concatenate(wrows, axis=0), preferred_element_type=F32)
    out = []
    base = 0
    for u, (_, _, v, _, j_range) in enumerate(units):
        o = None
        for idx, j in enumerate(j_range):
            term = att[(base + idx) * GLA_SUB:(base + idx + 1) * GLA_SUB, :] * v[j:j + 1, :]
            o = term if o is None else o + term
        base += len(j_range)
        kv = kd[:, (2 * u) * LANES:(2 * u + 1) * LANES]
        dec = kd[:, (2 * u + 1) * LANES:(2 * u + 2) * LANES]
        out.append((qes[u], o, kv, dec))
    return out


def _gla_log_alpha(z_ref, wa2_ref, ba2_ref):
    zz = jnp.dot(z_ref[...].astype(BF16), wa2_ref[...], preferred_element_type=F32) + ba2_ref[...]
    return (jnp.minimum(zz, 0.0) - jnp.log1p(jnp.exp(-jnp.abs(zz)))) * (1.0 / GLA_GATE_TAU)


def _gla_finish(o, go, gain):
    ms = jnp.mean(o * o, axis=-1, keepdims=True)
    on = o * lax.rsqrt(ms + EPS) * gain
    return on * (go * _sigmoid(go))


def _gla_long_kernel(q_ref, k_ref, v_ref, go_ref, z_ref, wa2_ref, ba2_ref, gain_ref, s0_ref,
                     out_ref, sout_ref, s_scr, gl_scr, o_scr, qe_scr, kv_scr, dec_scr, sall_scr,
                     *, tt):
    t = pl.program_id(2)
    nt = pl.num_programs(2)
    nsub = tt // GLA_SUB
    consts = _gla_consts()
    scale = GLA_DK ** -0.5

    @pl.when(t == 0)
    def _():
        s_scr[...] = s0_ref[0]

    gl_scr[...] = _gla_log_alpha(z_ref, wa2_ref, ba2_ref)

    def local_body(i, carry):
        rows = pl.ds(pl.multiple_of(i * GLA_SUB, GLA_SUB), GLA_SUB)
        heads = [slice(h * LANES, (h + 1) * LANES) for h in range(GLA_G)]
        units = [(q_ref[rows, cs] * scale, k_ref[rows, cs], v_ref[rows, cs], gl_scr[rows, cs],
                  range(GLA_SUB)) for cs in heads]
        for h, (qe, o, kv, dec) in enumerate(_gla_local(units, consts)):
            cs = heads[h]
            qe_scr[rows, cs] = qe
            o_scr[rows, cs] = o
            kv_scr[i, h] = kv
            dec_scr[i, h] = dec
        return carry

    lax.fori_loop(0, nsub, local_body, 0, unroll=min(4, nsub))

    for h in range(GLA_G):
        def state_body(i, s, h=h):
            sall_scr[i, h] = s.astype(BF16)
            return dec_scr[i, h] * s + kv_scr[i, h]
        s_scr[h] = lax.fori_loop(0, nsub, state_body, s_scr[h])

    def inter_body(i, carry):
        rows = pl.ds(pl.multiple_of(i * GLA_SUB, GLA_SUB), GLA_SUB)
        for h in range(GLA_G):
            cs = slice(h * LANES, (h + 1) * LANES)
            o_scr[rows, cs] += jnp.dot(qe_scr[rows, cs], sall_scr[i, h], preferred_element_type=F32)
        return carry

    lax.fori_loop(0, nsub, inter_body, 0, unroll=min(8, nsub))
    for h in range(GLA_G):
        cs = slice(h * LANES, (h + 1) * LANES)
        out_ref[:, cs] = _gla_finish(o_scr[:, cs], go_ref[:, cs], gain_ref[:, cs]).astype(out_ref.dtype)

    @pl.when(t == nt - 1)
    def _():
        sout_ref[0] = s_scr[...]


def gla_long(proj, z, s0, wts, *, batch, seq, tt):
    nt = seq // tt
    gw = GLA_G * LANES
    base = 2 * D_RNN // gw
    per = D_GLA // gw
    bsel = (lambda b: b) if s0.shape[0] == batch else (lambda b: 0)
    col = lambda which: pl.BlockSpec((tt, gw), lambda b, h, t: (b * nt + t, base + which * per + h))
    return pl.pallas_call(
        functools.partial(_gla_long_kernel, tt=tt),
        out_shape=(jax.ShapeDtypeStruct((batch * seq, D_GLA), BF16),
                   jax.ShapeDtypeStruct((batch, GLA_HEADS, GLA_DK, GLA_DK), F32)),
        grid=(batch, GLA_HEADS // GLA_G, nt),
        in_specs=[col(0), col(1), col(2), col(3),
                  pl.BlockSpec((tt, LANES), lambda b, h, t: (b * nt + t, 0)),
                  pl.BlockSpec((LANES, gw), lambda b, h, t: (0, h)),
                  pl.BlockSpec((1, gw), lambda b, h, t: (0, h)),
                  pl.BlockSpec((1, gw), lambda b, h, t: (0, h)),
                  pl.BlockSpec((1, GLA_G, GLA_DK, GLA_DK), lambda b, h, t: (bsel(b), h, 0, 0))],
        out_specs=(pl.BlockSpec((tt, gw), lambda b, h, t: (b * nt + t, h)),
                   pl.BlockSpec((1, GLA_G, GLA_DK, GLA_DK), lambda b, h, t: (b, h, 0, 0))),
        scratch_shapes=[pltpu.VMEM((GLA_G, GLA_DK, GLA_DK), F32),
                        pltpu.VMEM((tt, gw), F32),
                        pltpu.VMEM((tt, gw), F32),
                        pltpu.VMEM((tt, gw), BF16),
                        pltpu.VMEM((tt // GLA_SUB, GLA_G, GLA_DK, GLA_DK), F32),
                        pltpu.VMEM((tt // GLA_SUB, GLA_G, GLA_DK, GLA_DK), F32),
                        pltpu.VMEM((tt // GLA_SUB, GLA_G, GLA_DK, GLA_DK), BF16)],
        compiler_params=_params(("parallel", "parallel", "arbitrary")),
        name="gla_long",
    )(proj, proj, proj, proj, z, wts["gla_wa2"], wts["gla_ba2"], wts["gla_head_norm"], s0)


def _gla_short_kernel(q_ref, k_ref, v_ref, go_ref, z_ref, wa2_ref, ba2_ref, gain_ref, s0_ref,
                      out_ref, sout_ref, *, seq):
    consts = _gla_consts()
    row = consts[2]
    scale = GLA_DK ** -0.5
    gl = _gla_log_alpha(z_ref, wa2_ref, ba2_ref)
    nseq = GLA_SUB // seq
    assert nseq == GLA_G
    local = {}
    for h in range(GLA_G):
        cs = slice(h * LANES, (h + 1) * LANES)
        q = q_ref[:, cs] * scale
        k = k_ref[:, cs]
        v = v_ref[:, cs]
        g = gl[:, cs]
        units = []
        for b in range(nseq):
            mine = (row >= b * seq) & (row < (b + 1) * seq)
            zero = lambda a, mine=mine: jnp.where(mine, a, 0.0)
            units.append((zero(q), zero(k), zero(v), zero(g), range(b * seq, (b + 1) * seq)))
        for b, res in enumerate(_gla_local(units, consts)):
            local[b, h] = res
    for h in range(GLA_G):
        cs = slice(h * LANES, (h + 1) * LANES)
        o = jnp.zeros((GLA_SUB, LANES), F32)
        for b in range(nseq):
            qe, o_b, kv, dec = local[b, h]
            s = s0_ref[b, h]
            o = o + o_b + jnp.dot(qe, s.astype(BF16), preferred_element_type=F32)
            sout_ref[b, h] = dec * s + kv
        out_ref[:, cs] = _gla_finish(o, go_ref[:, cs], gain_ref[:, cs]).astype(out_ref.dtype)


def gla_short(proj, z, s0, wts, *, nb, seq):
    gw = GLA_G * LANES
    per = D_GLA // gw
    bpb = GLA_SUB // seq
    col = lambda which: pl.BlockSpec((GLA_SUB, gw), lambda i, h: (i, which * per + h))
    return pl.pallas_call(
        functools.partial(_gla_short_kernel, seq=seq),
        out_shape=(jax.ShapeDtypeStruct((nb * seq, D_GLA), BF16),
                   jax.ShapeDtypeStruct((nb, GLA_HEADS, GLA_DK, GLA_DK), F32)),
        grid=(nb // bpb, GLA_HEADS // GLA_G),
        in_specs=[col(0), col(1), col(2), col(3),
                  pl.BlockSpec((GLA_SUB, LANES), lambda i, h: (i, 0)),
                  pl.BlockSpec((LANES, gw), lambda i, h: (0, h)),
                  pl.BlockSpec((1, gw), lambda i, h: (0, h)),
                  pl.BlockSpec((1, gw), lambda i, h: (0, h)),
                  pl.BlockSpec((bpb, GLA_G, GLA_DK, GLA_DK), lambda i, h: (i, h, 0, 0))],
        out_specs=(pl.BlockSpec((GLA_SUB, gw), lambda i, h: (i, h)),
                   pl.BlockSpec((bpb, GLA_G, GLA_DK, GLA_DK), lambda i, h: (i, h, 0, 0))),
        compiler_params=_params(("parallel", "parallel")),
        name="gla_short",
    )(proj, proj, proj, proj, z, wts["gla_wa2"], wts["gla_ba2"], wts["gla_head_norm"], s0)


def _ffn_long_kernel(g_ref, u_ref, buf_ref, cw_ref, cb_ref, out_ref, nbuf_ref, ext_scr, *, tt):
    t = pl.program_id(2)
    nt = pl.num_programs(2)
    pad = SUB
    tail = FFN_CONV_W - 1

    @pl.when(t == 0)
    def _():
        ext_scr[pad - tail:pad, :] = buf_ref[0]

    ext_scr[pad:pad + tt, :] = g_ref[...]
    gc = cb_ref[...]
    for i in range(FFN_CONV_W):
        lo = pad - tail + i
        gc = gc + ext_scr[lo:lo + tt, :] * cw_ref[i:i + 1, :]
    out_ref[...] = (_gelu(gc) * u_ref[...]).astype(out_ref.dtype)
    ext_scr[pad - tail:pad, :] = ext_scr[pad + tt - tail:pad + tt, :]

    @pl.when(t == nt - 1)
    def _():
        nbuf_ref[0] = ext_scr[pad + tt - tail:pad + tt, :]


def ffn_long(gate, up, buf, wts, *, batch, seq, tt, tc):
    nt = seq // tt
    bsel = (lambda b: b) if buf.shape[0] == batch else (lambda b: 0)
    tail = FFN_CONV_W - 1
    return pl.pallas_call(
        functools.partial(_ffn_long_kernel, tt=tt),
        out_shape=(jax.ShapeDtypeStruct((batch * seq, D_FF), BF16),
                   jax.ShapeDtypeStruct((batch, tail, D_FF), F32)),
        grid=(batch, D_FF // tc, nt),
        in_specs=[pl.BlockSpec((tt, tc), lambda b, c, t: (b * nt + t, c)),
                  pl.BlockSpec((tt, tc), lambda b, c, t: (b * nt + t, c)),
                  pl.BlockSpec((1, tail, tc), lambda b, c, t: (bsel(b), 0, c)),
                  pl.BlockSpec((FFN_CONV_W, tc), lambda b, c, t: (0, c)),
                  pl.BlockSpec((1, tc), lambda b, c, t: (0, c))],
        out_specs=(pl.BlockSpec((tt, tc), lambda b, c, t: (b * nt + t, c)),
                   pl.BlockSpec((1, tail, tc), lambda b, c, t: (b, 0, c))),
        scratch_shapes=[pltpu.VMEM((tt + SUB, tc), F32)],
        compiler_params=_params(("parallel", "parallel", "arbitrary")),
        name="ffn_long",
    )(gate, up, buf, wts["ffn_conv_w"], wts["ffn_conv_b"])


def _ffn_short_kernel(g_ref, u_ref, buf_ref, cw_ref, cb_ref, out_ref, nbuf_ref, *, nb, seq):
    tail = FFN_CONV_W - 1
    ext = [buf_ref[i] for i in range(tail)]
    ext += [g_ref[t * nb:(t + 1) * nb, :] for t in range(seq)]
    for t in range(seq):
        gc = cb_ref[...]
        for i in range(FFN_CONV_W):
            gc = gc + ext[t + i] * cw_ref[i:i + 1, :]
        out_ref[t * nb:(t + 1) * nb, :] = (_gelu(gc) * u_ref[t * nb:(t + 1) * nb, :]).astype(out_ref.dtype)
    for i in range(tail):
        nbuf_ref[i] = ext[seq + i]


def ffn_short(gate, up, buf_t, wts, *, nb, seq, tc):
    rows = nb * seq
    tail = FFN_CONV_W - 1
    return pl.pallas_call(
        functools.partial(_ffn_short_kernel, nb=nb, seq=seq),
        out_shape=(jax.ShapeDtypeStruct((rows, D_FF), BF16),
                   jax.ShapeDtypeStruct((tail, nb, D_FF), F32)),
        grid=(D_FF // tc,),
        in_specs=[pl.BlockSpec((rows, tc), lambda c: (0, c)),
                  pl.BlockSpec((rows, tc), lambda c: (0, c)),
                  pl.BlockSpec((tail, nb, tc), lambda c: (0, 0, c)),
                  pl.BlockSpec((FFN_CONV_W, tc), lambda c: (0, c)),
                  pl.BlockSpec((1, tc), lambda c: (0, c))],
        out_specs=(pl.BlockSpec((rows, tc), lambda c: (0, c)),
                   pl.BlockSpec((tail, nb, tc), lambda c: (0, 0, c))),
        compiler_params=_params(("parallel",)),
        name="ffn_short",
    )(gate, up, buf_t, wts["ffn_conv_w"], wts["ffn_conv_b"])


def _layer(x, wts, mixers, *, tm):
    xn = rmsnorm(x, wts["norm_mix"], BF16, min(tm, 256))
    proj = matmul(xn, wts["w_in"], tm=tm, tn=1024, tk=D_MODEL, n_cols=PROJ_COLS)
    z = matmul(xn, wts["w_z"], tm=tm, tn=LANES, tk=D_MODEL)
    mix, ffn_fn, states = mixers(proj, z)
    x1 = matmul(mix, wts["w_out"], tm=tm, tn=1024, tk=D_MODEL, res=x)
    xn2 = rmsnorm(x1, wts["norm_ffn"], BF16, min(tm, 256))
    gate = matmul(xn2, wts["w_gate"], tm=tm, tn=1024, tk=D_MODEL)
    up = matmul(xn2, wts["w_up"], tm=tm, tn=1024, tk=D_MODEL)
    hid, new_ffn = ffn_fn(gate, up)
    x2 = matmul(hid, wts["w_down"], tm=tm, tn=1024, tk=2048, res=x1)
    y = rmsnorm(x2, wts["final_norm"], F32, min(tm, 256))
    return y, states, new_ffn


def kernel(x_prompt, x_sample, state_rglru_conv, state_rglru_h, state_gla_S, state_ffn_conv,
           meta_tokens, norm_mix, w_in, rg_conv_w, rg_conv_b, rg_wa, rg_ba, rg_wx, rg_bx, rg_lambda,
           rg_out_norm, gla_wa2, gla_ba2, gla_head_norm, w_out, norm_ffn, w_gate, w_up,
           ffn_conv_w, ffn_conv_b, w_down, final_norm):
    batch, seq, _ = x_prompt.shape
    nb, sseq, _ = x_sample.shape
    row = lambda a: a.reshape(1, -1)
    w_in_b = w_in[0].astype(BF16)
    wts = {
        "norm_mix": norm_mix[0], "norm_ffn": norm_ffn[0], "final_norm": final_norm,
        "w_in": w_in_b,
        "w_z": jnp.pad(w_in_b[:, PROJ_COLS:], ((0, 0), (0, LANES - GLA_RANK))),
        "w_out": w_out[0].astype(BF16), "w_gate": w_gate[0].astype(BF16),
        "w_up": w_up[0].astype(BF16), "w_down": w_down[0].astype(BF16),
        "rg_conv_w": rg_conv_w[0], "rg_conv_b": row(rg_conv_b[0]),
        "rg_wax": jnp.concatenate([rg_wa[0], rg_wx[0]], axis=-1).astype(BF16),
        "rg_ba": row(rg_ba[0]), "rg_bx": row(rg_bx[0]), "rg_lambda": row(rg_lambda[0]),
        "rg_out_norm": row(rg_out_norm[0]),
        "gla_wa2": jnp.pad(gla_wa2[0].astype(BF16), ((0, LANES - GLA_RANK), (0, 0))),
        "gla_ba2": row(gla_ba2[0]), "gla_head_norm": row(gla_head_norm[0]),
        "ffn_conv_w": ffn_conv_w[0], "ffn_conv_b": row(ffn_conv_b[0]),
    }

    def long_mixers(nbatch, nseq, tt, conv_buf, h0, s0, ffn_buf, reset_first):
        def run(proj, z):
            rnn, nconv, hlast = rg_long(proj, conv_buf, h0, wts, batch=nbatch, seq=nseq, tt=tt,
                                        reset_first=reset_first)
            gla, s_last = gla_long(proj, z, s0, wts, batch=nbatch, seq=nseq, tt=tt)
            ffn_fn = lambda gate, up: ffn_long(gate, up, ffn_buf, wts, batch=nbatch, seq=nseq,
                                               tt=tt, tc=2048)
            return jnp.concatenate([rnn, gla], axis=-1), ffn_fn, (nconv, hlast, s_last)
        return run

    zeros = lambda *s: jnp.zeros(s, F32)
    _, (m_conv, m_h, m_s), m_ffn = _layer(
        meta_tokens, wts,
        long_mixers(1, N_META, N_META, zeros(1, CONV_W - 1, D_RNN), zeros(1, 1, D_RNN),
                    zeros(1, GLA_HEADS, GLA_DK, GLA_DK), zeros(1, FFN_CONV_W - 1, D_FF), True),
        tm=N_META)

    y_p, (p_conv, p_h, p_s), p_ffn = _layer(
        x_prompt.reshape(batch * seq, D_MODEL), wts,
        long_mixers(batch, seq, 256, m_conv, m_h, m_s, m_ffn, False), tm=1024)

    def to_batch_major(a):
        return jnp.swapaxes(a.reshape(sseq, nb, -1), 0, 1).reshape(nb * sseq, -1)

    def to_time_major(a):
        return jnp.swapaxes(a.reshape(nb, sseq, -1), 0, 1).reshape(nb * sseq, -1)

    def short_mixers(proj, z):
        rnn, nconv_t, hlast = rg_short(proj, jnp.swapaxes(state_rglru_conv[0], 0, 1),
                                       state_rglru_h[0], wts, nb=nb, seq=sseq)
        gla_b, s_last = gla_short(to_batch_major(proj[:, 2 * D_RNN:]), to_batch_major(z),
                                  state_gla_S[0], wts, nb=nb, seq=sseq)

        def ffn_fn(gate, up):
            hid, nbuf_t = ffn_short(gate, up, jnp.swapaxes(state_ffn_conv[0], 0, 1), wts,
                                    nb=nb, seq=sseq, tc=1024)
            return hid, jnp.swapaxes(nbuf_t, 0, 1)
        mix = jnp.concatenate([rnn, to_time_major(gla_b)], axis=-1)
        return mix, ffn_fn, (jnp.swapaxes(nconv_t, 0, 1), hlast, s_last)

    y_s, (s_conv, s_h, s_s), s_ffn = _layer(to_time_major(x_sample), wts, short_mixers,
                                            tm=nb * sseq)
    y_s = to_batch_major(y_s)

    return (y_p.reshape(batch, seq, D_MODEL), y_s.reshape(nb, sseq, D_MODEL),
            p_conv[None], p_h.reshape(1, batch, D_RNN), p_s[None], p_ffn[None],
            s_conv[None], s_h[None], s_s[None], s_ffn[None])
```

```python
import functools
import math

import jax
import jax.numpy as jnp
from jax import lax
from jax.experimental import pallas as pl
from jax.experimental.pallas import tpu as pltpu

F32 = jnp.float32
BF16 = jnp.bfloat16

D_MODEL = 4096
N_META = 16
D_RNN = 2048
D_GLA = 2048
RG_BLOCKS = 16
RG_BLOCK = 128
CONV_W = 4
RG_C = 8.0
GLA_HEADS = 16
GLA_DK = 128
GLA_RANK = 16
GLA_GATE_TAU = 16.0
D_FF = 3 * D_MODEL
FFN_CONV_W = 3
EPS = 1e-6
PROJ_COLS = 2 * D_RNN + 4 * D_GLA

LANES = 128
SUB = 8
VMEM_LIMIT = 56 * 1024 * 1024
GLA_SUB = 16
GLA_G = 4


def _params(sem):
    return pltpu.CompilerParams(dimension_semantics=sem, vmem_limit_bytes=VMEM_LIMIT)


def _gelu(x):
    c = math.sqrt(2.0 / math.pi)
    return 0.5 * x * (1.0 + jnp.tanh(c * (x + 0.044715 * (x * x * x))))


def _softplus(x):
    return jnp.maximum(x, 0.0) + jnp.log1p(jnp.exp(-jnp.abs(x)))


def _sigmoid(x):
    return 1.0 / (1.0 + jnp.exp(-x))


def _rmsnorm_kernel(x_ref, g_ref, o_ref):
    x = x_ref[...]
    ms = jnp.mean(x * x, axis=-1, keepdims=True)
    o_ref[...] = (x * lax.rsqrt(ms + EPS) * g_ref[...]).astype(o_ref.dtype)


def rmsnorm(x, g, out_dtype, tm):
    m, d = x.shape
    return pl.pallas_call(
        _rmsnorm_kernel,
        out_shape=jax.ShapeDtypeStruct((m, d), out_dtype),
        grid=(m // tm,),
        in_specs=[pl.BlockSpec((tm, d), lambda i: (i, 0)),
                  pl.BlockSpec((1, d), lambda i: (0, 0))],
        out_specs=pl.BlockSpec((tm, d), lambda i: (i, 0)),
        compiler_params=_params(("parallel",)),
        name="rmsnorm",
    )(x, g.reshape(1, d))


def _mm_kernel(x_ref, w_ref, *rest, nk, has_res):
    if has_res:
        r_ref, o_ref = rest
    else:
        (o_ref,) = rest
    part = jnp.dot(x_ref[...], w_ref[...], preferred_element_type=F32)
    if nk == 1:
        o_ref[...] = part + r_ref[...] if has_res else part
        return
    k = pl.program_id(2)

    @pl.when(k == 0)
    def _():
        o_ref[...] = part + r_ref[...] if has_res else part

    @pl.when(k > 0)
    def _():
        o_ref[...] += part


def matmul(x, w, *, tm, tn, tk, n_cols=None, res=None):
    m, kdim = x.shape
    n = w.shape[1] if n_cols is None else n_cols
    nk = kdim // tk
    in_specs = [pl.BlockSpec((tm, tk), lambda j, i, k: (i, k)),
                pl.BlockSpec((tk, tn), lambda j, i, k: (k, j))]
    args = [x, w]
    if res is not None:
        in_specs.append(pl.BlockSpec((tm, tn), lambda j, i, k: (i, j)))
        args.append(res)
    return pl.pallas_call(
        functools.partial(_mm_kernel, nk=nk, has_res=res is not None),
        out_shape=jax.ShapeDtypeStruct((m, n), F32),
        grid=(n // tn, m // tm, nk),
        in_specs=in_specs,
        out_specs=pl.BlockSpec((tm, tn), lambda j, i, k: (i, j)),
        compiler_params=_params(("parallel", "parallel", "arbitrary")),
        name="matmul",
    )(*args)


def _group_steps(groups):
    steps, first = [], 0
    for rows, tm in groups:
        steps.append((first, rows // tm))
        first += rows // tm
    return steps, first


def _tile_index(i, first, count):
    return jnp.clip(i - first, 0, count - 1)


def _gmm_kernel(*refs, layout, steps, nk, cast_w):
    i = pl.program_id(1)
    pos = 0
    g_in = []
    for n_parts, has_res in layout:
        xs = refs[pos:pos + n_parts]
        pos += n_parts
        r = refs[pos] if has_res else None
        pos += int(has_res)
        g_in.append((xs, r))
    w_ref = refs[pos]
    outs = refs[pos + 1:pos + 1 + len(layout)]
    if cast_w:
        wsrc = refs[pos + 1 + len(layout)]

        @pl.when(i == 0)
        def _():
            wsrc[...] = w_ref[...].astype(BF16)
    else:
        wsrc = w_ref
    for (xs, r), o_ref, (first, count) in zip(g_in, outs, steps):
        @pl.when((i >= first) & (i < first + count))
        def _(xs=xs, r=r, o_ref=o_ref):
            acc, off = None, 0
            for x_ref in xs:
                kk = x_ref.shape[1]
                part = jnp.dot(x_ref[...], wsrc[off:off + kk, :], preferred_element_type=F32)
                acc = part if acc is None else acc + part
                off += kk
            if r is not None:
                first_val = acc + r[...]
            else:
                first_val = acc
            if nk == 1:
                o_ref[...] = first_val
            else:
                k = pl.program_id(2)

                @pl.when(k == 0)
                def _():
                    o_ref[...] = first_val

                @pl.when(k > 0)
                def _():
                    o_ref[...] += acc


def grouped_matmul(groups, w, *, tn, n_cols, tk=None, cast_w=False):
    kdim = sum(p.shape[1] for p in groups[0]["parts"])
    nk = 1 if tk is None else kdim // tk
    assert not (cast_w and nk > 1)
    steps, ni = _group_steps([(g["parts"][0].shape[0], g["tm"]) for g in groups])
    in_specs, args, layout = [], [], []
    for g, (first, count) in zip(groups, steps):
        assert nk == 1 or len(g["parts"]) == 1
        last_k = nk - 1

        def kidx(i, k, first=first, count=count):
            if nk == 1:
                return 0
            return jnp.where(i < first, 0, jnp.where(i < first + count, k, last_k))
        once = dict(pipeline_mode=pl.Buffered(1)) if (count == 1 and nk == 1) else {}
        for p in g["parts"]:
            kp = p.shape[1] if nk == 1 else tk
            in_specs.append(pl.BlockSpec(
                (g["tm"], kp),
                lambda j, i, k, first=first, count=count, kidx=kidx:
                    (_tile_index(i, first, count), kidx(i, k)), **once))
            args.append(p)
        if g.get("res") is not None:
            in_specs.append(pl.BlockSpec(
                (g["tm"], tn),
                lambda j, i, k, first=first, count=count: (_tile_index(i, first, count), j)))
            args.append(g["res"])
        layout.append((len(g["parts"]), g.get("res") is not None))
    kblk = kdim if nk == 1 else tk
    if cast_w:
        in_specs.append(pl.BlockSpec((None, kblk, tn), lambda j, i, k: (0, k, j)))
        scratch = [pltpu.VMEM((kblk, tn), BF16)]
    else:
        in_specs.append(pl.BlockSpec((kblk, tn), lambda j, i, k: (k, j)))
        scratch = []
    args.append(w)
    out_specs = [pl.BlockSpec((g["tm"], tn),
                              lambda j, i, k, first=first, count=count:
                                  (_tile_index(i, first, count), j))
                 for g, (first, count) in zip(groups, steps)]
    return pl.pallas_call(
        functools.partial(_gmm_kernel, layout=tuple(layout), steps=tuple(steps), nk=nk,
                          cast_w=cast_w),
        out_shape=[jax.ShapeDtypeStruct((g["parts"][0].shape[0], n_cols), F32) for g in groups],
        grid=(n_cols // tn, ni, nk),
        in_specs=in_specs,
        out_specs=out_specs,
        scratch_shapes=scratch,
        compiler_params=_params(("parallel", "arbitrary", "arbitrary")),
        name="grouped_matmul",
    )(*args)


def _rg_gates(xc, n, wax_ref, ba_ref, bx_ref, sp_row):
    cs = slice(n * RG_BLOCK, (n + 1) * RG_BLOCK)
    pre = jnp.dot(xc.astype(BF16), wax_ref[n], preferred_element_type=F32)
    r = _sigmoid(pre[:, :RG_BLOCK] + ba_ref[:, cs])
    i = _sigmoid(pre[:, RG_BLOCK:] + bx_ref[:, cs])
    log_a = -RG_C * r * sp_row[:, cs]
    a = jnp.exp(log_a)
    th = jnp.tanh(log_a)
    mult = jnp.sqrt(-2.0 * th / (1.0 - th))
    return a, mult, i


def _rg_long_kernel(xr_ref, yr_ref, cb_ref, h0_ref, cw_ref, cbias_ref, wax_ref, ba_ref, bx_ref,
                    lam_ref, gn_ref, out_ref, nconv_ref, hlast_ref,
                    ext_scr, a_scr, u_scr, h_scr, *, tt, reset_first):
    t = pl.program_id(1)
    nt = pl.num_programs(1)
    pad = SUB
    tail = CONV_W - 1

    @pl.when(t == 0)
    def _():
        ext_scr[pad - tail:pad, :] = cb_ref[0]
        h_scr[...] = h0_ref[0]

    ext_scr[pad:pad + tt, :] = xr_ref[...]
    sp_row = _softplus(-lam_ref[...])
    row = lax.broadcasted_iota(jnp.int32, (tt, RG_BLOCK), 0)
    first_row = jnp.where(t == 0, 0, -1)
    for n in range(RG_BLOCKS):
        cs = slice(n * RG_BLOCK, (n + 1) * RG_BLOCK)
        xc = cbias_ref[:, cs]
        for i in range(CONV_W):
            lo = pad - tail + i
            xc = xc + ext_scr[lo:lo + tt, cs] * cw_ref[i:i + 1, cs]
        a, mult, gate_i = _rg_gates(xc, n, wax_ref, ba_ref, bx_ref, sp_row)
        if reset_first:
            mult = jnp.where(row == first_row, 1.0, mult)
        a_scr[:, cs] = a
        u_scr[:, cs] = mult * gate_i * xc

    row8 = lax.broadcasted_iota(jnp.int32, (SUB, 512), 0)
    ncol = D_RNN // 512

    def scan_body(j, hs):
        r0 = pl.multiple_of(j * SUB, SUB)
        new = []
        for c in range(ncol):
            cs = slice(c * 512, (c + 1) * 512)
            a = a_scr[pl.ds(r0, SUB), cs]
            u = u_scr[pl.ds(r0, SUB), cs]
            for s in (1, 2, 4):
                a_sh = jnp.where(row8 >= s, pltpu.roll(a, s, 0), 1.0)
                u_sh = jnp.where(row8 >= s, pltpu.roll(u, s, 0), 0.0)
                u = a * u_sh + u
                a = a * a_sh
            h = a * hs[c] + u
            u_scr[pl.ds(r0, SUB), cs] = h
            new.append(h[SUB - 1:SUB, :])
        return tuple(new)

    h_in = tuple(h_scr[:, c * 512:(c + 1) * 512] for c in range(ncol))
    h_fin = lax.fori_loop(0, tt // SUB, scan_body, h_in)
    for c in range(ncol):
        h_scr[:, c * 512:(c + 1) * 512] = h_fin[c]

    y = u_scr[...] * _gelu(yr_ref[...])
    ms = jnp.mean(y * y, axis=-1, keepdims=True)
    out_ref[...] = (y * lax.rsqrt(ms + EPS) * gn_ref[...]).astype(out_ref.dtype)
    ext_scr[pad - tail:pad, :] = ext_scr[pad + tt - tail:pad + tt, :]

    @pl.when(t == nt - 1)
    def _():
        nconv_ref[0] = ext_scr[pad + tt - tail:pad + tt, :]
        hlast_ref[0] = h_scr[...]


def rg_long(proj, conv_buf, h0, wts, *, batch, seq, tt, reset_first):
    nt = seq // tt
    bsel = (lambda b: b) if conv_buf.shape[0] == batch else (lambda b: 0)
    vec = lambda r: pl.BlockSpec((r, D_RNN), lambda b, t: (0, 0))
    return pl.pallas_call(
        functools.partial(_rg_long_kernel, tt=tt, reset_first=reset_first),
        out_shape=(jax.ShapeDtypeStruct((batch * seq, D_RNN), BF16),
                   jax.ShapeDtypeStruct((batch, CONV_W - 1, D_RNN), F32),
                   jax.ShapeDtypeStruct((batch, 1, D_RNN), F32)),
        grid=(batch, nt),
        in_specs=[pl.BlockSpec((tt, D_RNN), lambda b, t: (b * nt + t, 0)),
                  pl.BlockSpec((tt, D_RNN), lambda b, t: (b * nt + t, 1)),
                  pl.BlockSpec((1, CONV_W - 1, D_RNN), lambda b, t: (bsel(b), 0, 0)),
                  pl.BlockSpec((1, 1, D_RNN), lambda b, t: (bsel(b), 0, 0)),
                  vec(CONV_W), vec(1),
                  pl.BlockSpec((RG_BLOCKS, RG_BLOCK, 2 * RG_BLOCK), lambda b, t: (0, 0, 0)),
                  vec(1), vec(1), vec(1), vec(1)],
        out_specs=(pl.BlockSpec((tt, D_RNN), lambda b, t: (b * nt + t, 0)),
                   pl.BlockSpec((1, CONV_W - 1, D_RNN), lambda b, t: (b, 0, 0)),
                   pl.BlockSpec((1, 1, D_RNN), lambda b, t: (b, 0, 0))),
        scratch_shapes=[pltpu.VMEM((tt + SUB, D_RNN), F32),
                        pltpu.VMEM((tt, D_RNN), F32),
                        pltpu.VMEM((tt, D_RNN), F32),
                        pltpu.VMEM((1, D_RNN), F32)],
        compiler_params=_params(("parallel", "arbitrary")),
        name="rg_long",
    )(proj, proj, conv_buf, h0, wts["rg_conv_w"], wts["rg_conv_b"], wts["rg_wax"],
      wts["rg_ba"], wts["rg_bx"], wts["rg_lambda"], wts["rg_out_norm"])


def _rg_short_kernel(xr_ref, yr_ref, cb_ref, h0_ref, cw_ref, cbias_ref, wax_ref, ba_ref, bx_ref,
                     lam_ref, gn_ref, out_ref, nconv_ref, hlast_ref, y_scr, *, nb, seq):
    sp_row = _softplus(-lam_ref[...])
    tail = CONV_W - 1
    for n in range(RG_BLOCKS):
        cs = slice(n * RG_BLOCK, (n + 1) * RG_BLOCK)
        ext = [cb_ref[i, :, cs] for i in range(tail)]
        ext += [xr_ref[t * nb:(t + 1) * nb, cs] for t in range(seq)]
        h = h0_ref[:, cs]
        for t in range(seq):
            xc = cbias_ref[:, cs]
            for i in range(CONV_W):
                xc = xc + ext[t + i] * cw_ref[i:i + 1, cs]
            a, mult, gate_i = _rg_gates(xc, n, wax_ref, ba_ref, bx_ref, sp_row)
            h = a * h + mult * gate_i * xc
            y_scr[t, :, cs] = h * _gelu(yr_ref[t * nb:(t + 1) * nb, cs])
        hlast_ref[:, cs] = h
        for i in range(tail):
            nconv_ref[i, :, cs] = ext[seq + i]
    for t in range(seq):
        y = y_scr[t]
        ms = jnp.mean(y * y, axis=-1, keepdims=True)
        out_ref[t * nb:(t + 1) * nb, :] = (y * lax.rsqrt(ms + EPS) * gn_ref[...]).astype(out_ref.dtype)


def rg_short(proj, conv_buf_t, h0, wts, *, nb, seq):
    rows = nb * seq
    vec = lambda r: pl.BlockSpec((r, D_RNN), lambda i: (0, 0))
    return pl.pallas_call(
        functools.partial(_rg_short_kernel, nb=nb, seq=seq),
        out_shape=(jax.ShapeDtypeStruct((rows, D_RNN), BF16),
                   jax.ShapeDtypeStruct((CONV_W - 1, nb, D_RNN), F32),
                   jax.ShapeDtypeStruct((nb, D_RNN), F32)),
        grid=(1,),
        in_specs=[pl.BlockSpec((rows, D_RNN), lambda i: (0, 0)),
                  pl.BlockSpec((rows, D_RNN), lambda i: (0, 1)),
                  pl.BlockSpec((CONV_W - 1, nb, D_RNN), lambda i: (0, 0, 0)),
                  pl.BlockSpec((nb, D_RNN), lambda i: (0, 0)),
                  vec(CONV_W), vec(1),
                  pl.BlockSpec((RG_BLOCKS, RG_BLOCK, 2 * RG_BLOCK), lambda i: (0, 0, 0)),
                  vec(1), vec(1), vec(1), vec(1)],
        out_specs=(pl.BlockSpec((rows, D_RNN), lambda i: (0, 0)),
                   pl.BlockSpec((CONV_W - 1, nb, D_RNN), lambda i: (0, 0, 0)),
                   pl.BlockSpec((nb, D_RNN), lambda i: (0, 0))),
        scratch_shapes=[pltpu.VMEM((seq, nb, D_RNN), F32)],
        compiler_params=_params(("arbitrary",)),
        name="rg_short",
    )(proj, proj, conv_buf_t, h0, wts["rg_conv_w"], wts["rg_conv_b"], wts["rg_wax"],
      wts["rg_ba"], wts["rg_bx"], wts["rg_lambda"], wts["rg_out_norm"])


def _gla_consts():
    er = lax.broadcasted_iota(jnp.int32, (LANES, LANES), 0)
    ec = lax.broadcasted_iota(jnp.int32, (LANES, LANES), 1)
    eye = jnp.where(er == ec, 1.0, 0.0).astype(BF16)
    ones = jnp.ones((LANES, LANES), BF16)
    row = lax.broadcasted_iota(jnp.int32, (GLA_SUB, LANES), 0)
    row8 = lax.broadcasted_iota(jnp.int32, (SUB, LANES), 0)
    pr = lax.broadcasted_iota(jnp.int32, (2 * GLA_SUB, LANES), 0)
    pick = jnp.where((pr == GLA_SUB) | (pr == GLA_SUB + 1), 1.0, 0.0).astype(BF16)
    return eye, ones, row, row8, pick


def _cumsum_rows(g, row8):
    def scan8(x):
        for s in (1, 2, 4):
            x = x + jnp.where(row8 >= s, pltpu.roll(x, s, 0), 0.0)
        return x
    lo = scan8(g[:SUB])
    hi = scan8(g[SUB:]) + lo[SUB - 1:SUB, :]
    return jnp.concatenate([lo, hi], axis=0)


def _gla_local(units, consts):
    eye, ones, row, row8, pick = consts
    assert len(units) == GLA_G
    qes, ps, xs = [], [], []
    for q, k, v, g, j_range in units:
        cum = _cumsum_rows(g, row8)
        last = cum[GLA_SUB - 1:GLA_SUB, :]
        qes.append((q * jnp.exp(cum)).astype(BF16))
        for j in j_range:
            w = jnp.exp(jnp.where(row >= j, cum - cum[j:j + 1, :], -jnp.inf))
            ps.append((q * w * k[j:j + 1, :]).astype(BF16))
        ke = k * jnp.exp(last - cum)
        el = jnp.exp(last)
        e_hi = el.astype(BF16).astype(F32)
        e_lo = el - e_hi
        extra = jnp.where(row == 0, e_hi, jnp.where(row == 1, e_lo, 0.0))
        xs.append(jnp.concatenate([ke, extra], axis=0).astype(BF16))
    att = jnp.dot(jnp.concatenate(ps, axis=0), ones, preferred_element_type=F32)
    xt = lax.dot_general(eye, jnp.concatenate(xs, axis=0), (((1,), (1,)), ((), ())),
                         preferred_element_type=F32).astype(BF16)
    zblk = jnp.zeros((2 * GLA_SUB, LANES), BF16)
    wrows = []
    for u, (_, _, v, _, _) in enumerate(units):
        vpad = jnp.concatenate([v, jnp.zeros_like(v)], axis=0).astype(BF16)
        wrows.append(jnp.concatenate([zblk] * (2 * u) + [vpad, pick]
                                     + [zblk] * (2 * (GLA_G - 1 - u)), axis=1))
    kd = jnp.dot(xt, jnp.concatenate(wrows, axis=0), preferred_element_type=F32)
    out = []
    base = 0
    for u, (_, _, v, _, j_range) in enumerate(units):
        o = None
        for idx, j in enumerate(j_range):
            term = att[(base + idx) * GLA_SUB:(base + idx + 1) * GLA_SUB, :] * v[j:j + 1, :]
            o = term if o is None else o + term
        base += len(j_range)
        kv = kd[:, (2 * u) * LANES:(2 * u + 1) * LANES]
        dec = kd[:, (2 * u + 1) * LANES:(2 * u + 2) * LANES]
        out.append((qes[u], o, kv, dec))
    return out


def _gla_log_alpha(z_ref, wa2_ref, ba2_ref):
    zz = jnp.dot(z_ref[...].astype(BF16), wa2_ref[...], preferred_element_type=F32) + ba2_ref[...]
    return (jnp.minimum(zz, 0.0) - jnp.log1p(jnp.exp(-jnp.abs(zz)))) * (1.0 / GLA_GATE_TAU)


def _gla_finish(o, go, gain):
    ms = jnp.mean(o * o, axis=-1, keepdims=True)
    on = o * lax.rsqrt(ms + EPS) * gain
    return on * (go * _sigmoid(go))


def _gla_long_kernel(q_ref, k_ref, v_ref, go_ref, z_ref, wa2_ref, ba2_ref, gain_ref, s0_ref,
                     out_ref, sout_ref, s_scr, gl_scr, o_scr, qe_scr, kv_scr, dec_scr, sall_scr,
                     *, tt):
    t = pl.program_id(2)
    nt = pl.num_programs(2)
    nsub = tt // GLA_SUB
    consts = _gla_consts()
    scale = GLA_DK ** -0.5

    @pl.when(t == 0)
    def _():
        s_scr[...] = s0_ref[0]

    gl_scr[...] = _gla_log_alpha(z_ref, wa2_ref, ba2_ref)

    def local_body(i, carry):
        rows = pl.ds(pl.multiple_of(i * GLA_SUB, GLA_SUB), GLA_SUB)
        heads = [slice(h * LANES, (h + 1) * LANES) for h in range(GLA_G)]
        units = [(q_ref[rows, cs] * scale, k_ref[rows, cs], v_ref[rows, cs], gl_scr[rows, cs],
                  range(GLA_SUB)) for cs in heads]
        for h, (qe, o, kv, dec) in enumerate(_gla_local(units, consts)):
            cs = heads[h]
            qe_scr[rows, cs] = qe
            o_scr[rows, cs] = o
            kv_scr[i, h] = kv
            dec_scr[i, h] = dec
        return carry

    lax.fori_loop(0, nsub, local_body, 0, unroll=min(4, nsub))

    for h in range(GLA_G):
        def state_body(i, s, h=h):
            sall_scr[i, h] = s.astype(BF16)
            return dec_scr[i, h] * s + kv_scr[i, h]
        s_scr[h] = lax.fori_loop(0, nsub, state_body, s_scr[h])

    def inter_body(i, carry):
        rows = pl.ds(pl.multiple_of(i * GLA_SUB, GLA_SUB), GLA_SUB)
        for h in range(GLA_G):
            cs = slice(h * LANES, (h + 1) * LANES)
            o_scr[rows, cs] += jnp.dot(qe_scr[rows, cs], sall_scr[i, h], preferred_element_type=F32)
        return carry

    lax.fori_loop(0, nsub, inter_body, 0, unroll=min(8, nsub))
    for h in range(GLA_G):
        cs = slice(h * LANES, (h + 1) * LANES)
        out_ref[:, cs] = _gla_finish(o_scr[:, cs], go_ref[:, cs], gain_ref[:, cs]).astype(out_ref.dtype)

    @pl.when(t == nt - 1)
    def _():
        sout_ref[0] = s_scr[...]


def gla_long(proj, z, s0, wts, *, batch, seq, tt):
    nt = seq // tt
    gw = GLA_G * LANES
    base = 2 * D_RNN // gw
    per = D_GLA // gw
    bsel = (lambda b: b) if s0.shape[0] == batch else (lambda b: 0)
    col = lambda which: pl.BlockSpec((tt, gw), lambda b, h, t: (b * nt + t, base + which * per + h))
    return pl.pallas_call(
        functools.partial(_gla_long_kernel, tt=tt),
        out_shape=(jax.ShapeDtypeStruct((batch * seq, D_GLA), BF16),
                   jax.ShapeDtypeStruct((batch, GLA_HEADS, GLA_DK, GLA_DK), F32)),
        grid=(batch, GLA_HEADS // GLA_G, nt),
        in_specs=[col(0), col(1), col(2), col(3),
                  pl.BlockSpec((tt, LANES), lambda b, h, t: (b * nt + t, 0)),
                  pl.BlockSpec((LANES, gw), lambda b, h, t: (0, h)),
                  pl.BlockSpec((1, gw), lambda b, h, t: (0, h)),
                  pl.BlockSpec((1, gw), lambda b, h, t: (0, h)),
                  pl.BlockSpec((1, GLA_G, GLA_DK, GLA_DK), lambda b, h, t: (bsel(b), h, 0, 0))],
        out_specs=(pl.BlockSpec((tt, gw), lambda b, h, t: (b * nt + t, h)),
                   pl.BlockSpec((1, GLA_G, GLA_DK, GLA_DK), lambda b, h, t: (b, h, 0, 0))),
        scratch_shapes=[pltpu.VMEM((GLA_G, GLA_DK, GLA_DK), F32),
                        pltpu.VMEM((tt, gw), F32),
                        pltpu.VMEM((tt, gw), F32),
                        pltpu.VMEM((tt, gw), BF16),
                        pltpu.VMEM((tt // GLA_SUB, GLA_G, GLA_DK, GLA_DK), F32),
                        pltpu.VMEM((tt // GLA_SUB, GLA_G, GLA_DK, GLA_DK), F32),
                        pltpu.VMEM((tt // GLA_SUB, GLA_G, GLA_DK, GLA_DK), BF16)],
        compiler_params=_params(("parallel", "parallel", "arbitrary")),
        name="gla_long",
    )(proj, proj, proj, proj, z, wts["gla_wa2"], wts["gla_ba2"], wts["gla_head_norm"], s0)


def _gla_short_kernel(q_ref, k_ref, v_ref, go_ref, z_ref, wa2_ref, ba2_ref, gain_ref, s0_ref,
                      out_ref, sout_ref, *, seq):
    consts = _gla_consts()
    row = consts[2]
    scale = GLA_DK ** -0.5
    gl = _gla_log_alpha(z_ref, wa2_ref, ba2_ref)
    nseq = GLA_SUB // seq
    assert nseq == GLA_G
    local = {}
    for h in range(GLA_G):
        cs = slice(h * LANES, (h + 1) * LANES)
        q = q_ref[:, cs] * scale
        k = k_ref[:, cs]
        v = v_ref[:, cs]
        g = gl[:, cs]
        units = []
        for b in range(nseq):
            mine = (row >= b * seq) & (row < (b + 1) * seq)
            zero = lambda a, mine=mine: jnp.where(mine, a, 0.0)
            units.append((zero(q), zero(k), zero(v), zero(g), range(b * seq, (b + 1) * seq)))
        for b, res in enumerate(_gla_local(units, consts)):
            local[b, h] = res
    for h in range(GLA_G):
        cs = slice(h * LANES, (h + 1) * LANES)
        o = jnp.zeros((GLA_SUB, LANES), F32)
        for b in range(nseq):
            qe, o_b, kv, dec = local[b, h]
            s = s0_ref[b, h]
            o = o + o_b + jnp.dot(qe, s.astype(BF16), preferred_element_type=F32)
            sout_ref[b, h] = dec * s + kv
        out_ref[:, cs] = _gla_finish(o, go_ref[:, cs], gain_ref[:, cs]).astype(out_ref.dtype)


def gla_short(proj, z, s0, wts, *, nb, seq):
    gw = GLA_G * LANES
    per = D_GLA // gw
    bpb = GLA_SUB // seq
    col = lambda which: pl.BlockSpec((GLA_SUB, gw), lambda i, h: (i, which * per + h))
    return pl.pallas_call(
        functools.partial(_gla_short_kernel, seq=seq),
        out_shape=(jax.ShapeDtypeStruct((nb * seq, D_GLA), BF16),
                   jax.ShapeDtypeStruct((nb, GLA_HEADS, GLA_DK, GLA_DK), F32)),
        grid=(nb // bpb, GLA_HEADS // GLA_G),
        in_specs=[col(0), col(1), col(2), col(3),
                  pl.BlockSpec((GLA_SUB, LANES), lambda i, h: (i, 0)),
                  pl.BlockSpec((LANES, gw), lambda i, h: (0, h)),
                  pl.BlockSpec((1, gw), lambda i, h: (0, h)),
                  pl.BlockSpec((1, gw), lambda i, h: (0, h)),
                  pl.BlockSpec((bpb, GLA_G, GLA_DK, GLA_DK), lambda i, h: (i, h, 0, 0))],
        out_specs=(pl.BlockSpec((GLA_SUB, gw), lambda i, h: (i, h)),
                   pl.BlockSpec((bpb, GLA_G, GLA_DK, GLA_DK), lambda i, h: (i, h, 0, 0))),
        compiler_params=_params(("parallel", "parallel")),
        name="gla_short",
    )(proj, proj, proj, proj, z, wts["gla_wa2"], wts["gla_ba2"], wts["gla_head_norm"], s0)


def _ffn_kernel(xm_ref, xp_ref, xs_ref, wg_ref, wu_ref, cw_ref, cb_ref, bufs_ref,
                hp_ref, hs_ref, nbufp_ref, nbufs_ref, extp_scr, exts_scr, mhist_scr,
                *, steps, tiles_per_seq, nb):
    i = pl.program_id(1)
    (m_first, _), (p_first, p_count), (s_first, _) = steps
    tail = FFN_CONV_W - 1
    pad = SUB
    tm = xp_ref.shape[0]
    rows_s = xs_ref.shape[0]
    hist_s = tail * nb

    def conv(ext_ref, start, shift, rows):
        gc = cb_ref[...]
        for c in range(FFN_CONV_W):
            lo = start - (tail - c) * shift
            gc = gc + ext_ref[lo:lo + rows, :] * cw_ref[c:c + 1, :]
        return gc

    @pl.when(i == m_first)
    def _():
        g = jnp.dot(xm_ref[...], wg_ref[...], preferred_element_type=F32)
        mhist_scr[...] = g[g.shape[0] - tail:, :]

    @pl.when((i >= p_first) & (i < p_first + p_count))
    def _():
        @pl.when((i - p_first) % tiles_per_seq == 0)
        def _():
            extp_scr[pad - tail:pad, :] = mhist_scr[...]

        x = xp_ref[...]
        extp_scr[pad:pad + tm, :] = jnp.dot(x, wg_ref[...], preferred_element_type=F32)
        up = jnp.dot(x, wu_ref[...], preferred_element_type=F32)
        hp_ref[...] = (_gelu(conv(extp_scr, pad, 1, tm)) * up).astype(hp_ref.dtype)
        last = extp_scr[pad + tm - tail:pad + tm, :]
        extp_scr[pad - tail:pad, :] = last
        nbufp_ref[0] = last

    @pl.when(i == s_first)
    def _():
        x = xs_ref[...]
        exts_scr[0:hist_s, :] = bufs_ref[...]
        exts_scr[hist_s:hist_s + rows_s, :] = jnp.dot(x, wg_ref[...], preferred_element_type=F32)
        up = jnp.dot(x, wu_ref[...], preferred_element_type=F32)
        hs_ref[...] = (_gelu(conv(exts_scr, hist_s, nb, rows_s)) * up).astype(hs_ref.dtype)
        nbufs_ref[...] = exts_scr[rows_s:rows_s + hist_s, :]


def ffn_gate(xm, xp, xs, bufs, wts, *, tm, tn, seq_rows, nb):
    rows_p, rows_s = xp.shape[0], xs.shape[0]
    tail = FFN_CONV_W - 1
    steps, ni = _group_steps([(xm.shape[0], xm.shape[0]), (rows_p, tm), (rows_s, rows_s)])
    (_, _), (p_first, p_count), (_, _) = steps
    tps = seq_rows // tm
    ptile = lambda i: _tile_index(i, p_first, p_count)
    const = lambda shape: pl.BlockSpec(shape, lambda j, i: (0, 0))
    colblk = lambda r: pl.BlockSpec((r, tn), lambda j, i: (0, j))
    return pl.pallas_call(
        functools.partial(_ffn_kernel, steps=tuple(steps), tiles_per_seq=tps, nb=nb),
        out_shape=(jax.ShapeDtypeStruct((rows_p, D_FF), BF16),
                   jax.ShapeDtypeStruct((rows_s, D_FF), BF16),
                   jax.ShapeDtypeStruct((rows_p // seq_rows, tail, D_FF), F32),
                   jax.ShapeDtypeStruct((tail * nb, D_FF), F32)),
        grid=(D_FF // tn, ni),
        in_specs=[const(xm.shape),
                  pl.BlockSpec((tm, D_MODEL), lambda j, i: (ptile(i), 0)),
                  const(xs.shape),
                  colblk(D_MODEL), colblk(D_MODEL), colblk(FFN_CONV_W), colblk(1),
                  colblk(tail * nb)],
        out_specs=(pl.BlockSpec((tm, tn), lambda j, i: (ptile(i), j)),
                   colblk(rows_s),
                   pl.BlockSpec((1, tail, tn), lambda j, i: (ptile(i) // tps, 0, j)),
                   colblk(tail * nb)),
        scratch_shapes=[pltpu.VMEM((tm + SUB, tn), F32),
                        pltpu.VMEM((rows_s + tail * nb, tn), F32),
                        pltpu.VMEM((tail, tn), F32)],
        compiler_params=_params(("parallel", "arbitrary")),
        name="ffn_gate",
    )(xm, xp, xs, wts["w_gate"], wts["w_up"], wts["ffn_conv_w"], wts["ffn_conv_b"], bufs)


PROMPT_TM = 1024
FFN_TM = 512
SEQ_TT = 256
NORM_TM = 256


def kernel(x_prompt, x_sample, state_rglru_conv, state_rglru_h, state_gla_S, state_ffn_conv,
           meta_tokens, norm_mix, w_in, rg_conv_w, rg_conv_b, rg_wa, rg_ba, rg_wx, rg_bx, rg_lambda,
           rg_out_norm, gla_wa2, gla_ba2, gla_head_norm, w_out, norm_ffn, w_gate, w_up,
           ffn_conv_w, ffn_conv_b, w_down, final_norm):
    batch, seq, _ = x_prompt.shape
    nb, sseq, _ = x_sample.shape
    row = lambda a: a.reshape(1, -1)
    wts = {
        "w_z": jnp.pad(w_in[0, :, PROJ_COLS:].astype(BF16), ((0, 0), (0, LANES - GLA_RANK))),
        "w_gate": w_gate[0].astype(BF16), "w_up": w_up[0].astype(BF16),
        "w_down": w_down[0].astype(BF16),
        "rg_conv_w": rg_conv_w[0], "rg_conv_b": row(rg_conv_b[0]),
        "rg_wax": jnp.concatenate([rg_wa[0], rg_wx[0]], axis=-1).astype(BF16),
        "rg_ba": row(rg_ba[0]), "rg_bx": row(rg_bx[0]), "rg_lambda": row(rg_lambda[0]),
        "rg_out_norm": row(rg_out_norm[0]),
        "gla_wa2": jnp.pad(gla_wa2[0].astype(BF16), ((0, LANES - GLA_RANK), (0, 0))),
        "gla_ba2": row(gla_ba2[0]), "gla_head_norm": row(gla_head_norm[0]),
        "ffn_conv_w": ffn_conv_w[0], "ffn_conv_b": row(ffn_conv_b[0]),
    }

    def to_batch_major(a):
        return jnp.swapaxes(a.reshape(sseq, nb, -1), 0, 1).reshape(nb * sseq, -1)

    def to_time_major(a):
        return jnp.swapaxes(a.reshape(nb, sseq, -1), 0, 1).reshape(nb * sseq, -1)

    rows_p, rows_s = batch * seq, nb * sseq
    xs = [meta_tokens, x_prompt.reshape(rows_p, D_MODEL), to_time_major(x_sample)]
    tms = [N_META, PROMPT_TM, rows_s]
    norm_tms = [N_META, NORM_TM, NORM_TM]

    def norm_all(arrs, g, dtype):
        return [rmsnorm(a, g, dtype, t) for a, t in zip(arrs, norm_tms)]

    xn = norm_all(xs, norm_mix[0], BF16)
    proj_m, proj_p, proj_s = grouped_matmul(
        [dict(parts=[a], tm=t) for a, t in zip(xn, tms)], w_in, tn=512, n_cols=PROJ_COLS,
        cast_w=True)
    z_m, z_p, z_s = [matmul(a, wts["w_z"], tm=t, tn=LANES, tk=D_MODEL) for a, t in zip(xn, tms)]

    zeros = lambda *s: jnp.zeros(s, F32)
    rnn_m, m_conv, m_h = rg_long(proj_m, zeros(1, CONV_W - 1, D_RNN), zeros(1, 1, D_RNN), wts,
                                 batch=1, seq=N_META, tt=N_META, reset_first=True)
    gla_m, m_s = gla_long(proj_m, z_m, zeros(1, GLA_HEADS, GLA_DK, GLA_DK), wts, batch=1,
                          seq=N_META, tt=N_META)
    rnn_p, p_conv, p_h = rg_long(proj_p, m_conv, m_h, wts, batch=batch, seq=seq, tt=SEQ_TT,
                                 reset_first=False)
    gla_p, p_s = gla_long(proj_p, z_p, m_s, wts, batch=batch, seq=seq, tt=SEQ_TT)
    rnn_s, s_conv_t, s_h = rg_short(proj_s, jnp.swapaxes(state_rglru_conv[0], 0, 1),
                                    state_rglru_h[0], wts, nb=nb, seq=sseq)
    gla_sb, s_s = gla_short(to_batch_major(proj_s[:, 2 * D_RNN:]), to_batch_major(z_s),
                            state_gla_S[0], wts, nb=nb, seq=sseq)
    gla_s = to_time_major(gla_sb)

    x1 = grouped_matmul(
        [dict(parts=[r, g], tm=t, res=x)
         for r, g, t, x in zip([rnn_m, rnn_p, rnn_s], [gla_m, gla_p, gla_s], tms, xs)],
        w_out, tn=512, n_cols=D_MODEL, cast_w=True)
    xn2 = norm_all(x1, norm_ffn[0], BF16)
    tail = FFN_CONV_W - 1
    bufs = jnp.swapaxes(state_ffn_conv[0], 0, 1).reshape(tail * nb, D_FF)
    hid_p, hid_s, p_ffn, s_ffn_t = ffn_gate(xn2[0], xn2[1], xn2[2], bufs, wts, tm=FFN_TM, tn=512,
                                            seq_rows=seq, nb=nb)
    x2_p, x2_s = grouped_matmul(
        [dict(parts=[hid_p], tm=PROMPT_TM, res=x1[1]), dict(parts=[hid_s], tm=rows_s, res=x1[2])],
        wts["w_down"], tn=1024, n_cols=D_MODEL, tk=2048)
    y_p = rmsnorm(x2_p, final_norm, F32, NORM_TM)
    y_s = to_batch_major(rmsnorm(x2_s, final_norm, F32, NORM_TM))

    return (y_p.reshape(batch, seq, D_MODEL), y_s.reshape(nb, sseq, D_MODEL),
            p_conv[None], p_h.reshape(1, batch, D_RNN), p_s[None], p_ffn[None],
            jnp.swapaxes(s_conv_t, 0, 1)[None], s_h[None], s_s[None],
            jnp.swapaxes(s_ffn_t.reshape(tail, nb, D_FF), 0, 1)[None])
```

```python
import functools
import math

import jax
import jax.numpy as jnp
from jax import lax
from jax.experimental import pallas as pl
from jax.experimental.pallas import tpu as pltpu

F32 = jnp.float32
BF16 = jnp.bfloat16

D_MODEL = 4096
N_META = 16
D_RNN = 2048
D_GLA = 2048
RG_BLOCKS = 16
RG_BLOCK = 128
CONV_W = 4
RG_C = 8.0
GLA_HEADS = 16
GLA_DK = 128
GLA_RANK = 16
GLA_GATE_TAU = 16.0
D_FF = 3 * D_MODEL
FFN_CONV_W = 3
EPS = 1e-6
PROJ_COLS = 2 * D_RNN + 4 * D_GLA

LANES = 128
SUB = 8
VMEM_LIMIT = 60 * 1024 * 1024
GLA_SUB = 16
GLA_G = 4


def _params(sem):
    return pltpu.CompilerParams(dimension_semantics=sem, vmem_limit_bytes=VMEM_LIMIT)


def _gelu(x):
    c = math.sqrt(2.0 / math.pi)
    return 0.5 * x * (1.0 + jnp.tanh(c * (x + 0.044715 * (x * x * x))))


def _softplus(x):
    return jnp.maximum(x, 0.0) + jnp.log1p(jnp.exp(-jnp.abs(x)))


def _sigmoid(x):
    return 0.5 * (1.0 + jnp.tanh(0.5 * x))


def _rmsnorm_kernel(x_ref, g_ref, o_ref):
    x = x_ref[...]
    ms = jnp.mean(x * x, axis=-1, keepdims=True)
    o_ref[...] = (x * lax.rsqrt(ms + EPS) * g_ref[...]).astype(o_ref.dtype)


def rmsnorm(x, g, out_dtype, tm):
    m, d = x.shape
    return pl.pallas_call(
        _rmsnorm_kernel,
        out_shape=jax.ShapeDtypeStruct((m, d), out_dtype),
        grid=(m // tm,),
        in_specs=[pl.BlockSpec((tm, d), lambda i: (i, 0)),
                  pl.BlockSpec((1, d), lambda i: (0, 0))],
        out_specs=pl.BlockSpec((tm, d), lambda i: (i, 0)),
        compiler_params=_params(("parallel",)),
        name="rmsnorm",
    )(x, g.reshape(1, d))


def _mm_kernel(x_ref, w_ref, *rest, nk, has_res):
    if has_res:
        r_ref, o_ref = rest
    else:
        (o_ref,) = rest
    part = jnp.dot(x_ref[...], w_ref[...], preferred_element_type=F32)
    if nk == 1:
        o_ref[...] = part + r_ref[...] if has_res else part
        return
    k = pl.program_id(2)

    @pl.when(k == 0)
    def _():
        o_ref[...] = part + r_ref[...] if has_res else part

    @pl.when(k > 0)
    def _():
        o_ref[...] += part


def matmul(x, w, *, tm, tn, tk, n_cols=None, res=None):
    m, kdim = x.shape
    n = w.shape[1] if n_cols is None else n_cols
    nk = kdim // tk
    in_specs = [pl.BlockSpec((tm, tk), lambda j, i, k: (i, k)),
                pl.BlockSpec((tk, tn), lambda j, i, k: (k, j))]
    args = [x, w]
    if res is not None:
        in_specs.append(pl.BlockSpec((tm, tn), lambda j, i, k: (i, j)))
        args.append(res)
    return pl.pallas_call(
        functools.partial(_mm_kernel, nk=nk, has_res=res is not None),
        out_shape=jax.ShapeDtypeStruct((m, n), F32),
        grid=(n // tn, m // tm, nk),
        in_specs=in_specs,
        out_specs=pl.BlockSpec((tm, tn), lambda j, i, k: (i, j)),
        compiler_params=_params(("parallel", "parallel", "arbitrary")),
        name="matmul",
    )(*args)


def _group_steps(groups):
    steps, first = [], 0
    for rows, tm in groups:
        steps.append((first, rows // tm))
        first += rows // tm
    return steps, first


def _tile_index(i, first, count):
    return jnp.clip(i - first, 0, count - 1)


def _gmm_kernel(*refs, layout, steps, nk, cast_w):
    i = pl.program_id(1)
    pos = 0
    g_in = []
    for n_parts, has_res in layout:
        xs = refs[pos:pos + n_parts]
        pos += n_parts
        r = refs[pos] if has_res else None
        pos += int(has_res)
        g_in.append((xs, r))
    w_ref = refs[pos]
    outs = refs[pos + 1:pos + 1 + len(layout)]
    if cast_w:
        wsrc = refs[pos + 1 + len(layout)]

        @pl.when(i == 0)
        def _():
            wsrc[...] = w_ref[...].astype(BF16)
    else:
        wsrc = w_ref
    for (xs, r), o_ref, (first, count) in zip(g_in, outs, steps):
        @pl.when((i >= first) & (i < first + count))
        def _(xs=xs, r=r, o_ref=o_ref):
            acc, off = None, 0
            for x_ref in xs:
                kk = x_ref.shape[1]
                part = jnp.dot(x_ref[...], wsrc[off:off + kk, :], preferred_element_type=F32)
                acc = part if acc is None else acc + part
                off += kk
            if r is not None:
                first_val = acc + r[...]
            else:
                first_val = acc
            if nk == 1:
                o_ref[...] = first_val
            else:
                k = pl.program_id(2)

                @pl.when(k == 0)
                def _():
                    o_ref[...] = first_val

                @pl.when(k > 0)
                def _():
                    o_ref[...] += acc


def grouped_matmul(groups, w, *, tn, n_cols, tk=None, cast_w=False):
    kdim = sum(p.shape[1] for p in groups[0]["parts"])
    nk = 1 if tk is None else kdim // tk
    assert not (cast_w and nk > 1)
    steps, ni = _group_steps([(g["parts"][0].shape[0], g["tm"]) for g in groups])
    in_specs, args, layout = [], [], []
    for g, (first, count) in zip(groups, steps):
        assert nk == 1 or len(g["parts"]) == 1
        last_k = nk - 1

        def kidx(i, k, first=first, count=count):
            if nk == 1:
                return 0
            return jnp.where(i < first, 0, jnp.where(i < first + count, k, last_k))
        once = dict(pipeline_mode=pl.Buffered(1)) if (count == 1 and nk == 1) else {}
        for p in g["parts"]:
            kp = p.shape[1] if nk == 1 else tk
            in_specs.append(pl.BlockSpec(
                (g["tm"], kp),
                lambda j, i, k, first=first, count=count, kidx=kidx:
                    (_tile_index(i, first, count), kidx(i, k)), **once))
            args.append(p)
        if g.get("res") is not None:
            in_specs.append(pl.BlockSpec(
                (g["tm"], tn),
                lambda j, i, k, first=first, count=count: (_tile_index(i, first, count), j)))
            args.append(g["res"])
        layout.append((len(g["parts"]), g.get("res") is not None))
    kblk = kdim if nk == 1 else tk
    if cast_w:
        in_specs.append(pl.BlockSpec((None, kblk, tn), lambda j, i, k: (0, k, j)))
        scratch = [pltpu.VMEM((kblk, tn), BF16)]
    else:
        in_specs.append(pl.BlockSpec((kblk, tn), lambda j, i, k: (k, j)))
        scratch = []
    args.append(w)
    out_specs = [pl.BlockSpec((g["tm"], tn),
                              lambda j, i, k, first=first, count=count:
                                  (_tile_index(i, first, count), j))
                 for g, (first, count) in zip(groups, steps)]
    return pl.pallas_call(
        functools.partial(_gmm_kernel, layout=tuple(layout), steps=tuple(steps), nk=nk,
                          cast_w=cast_w),
        out_shape=[jax.ShapeDtypeStruct((g["parts"][0].shape[0], n_cols), F32) for g in groups],
        grid=(n_cols // tn, ni, nk),
        in_specs=in_specs,
        out_specs=out_specs,
        scratch_shapes=scratch,
        compiler_params=_params(("parallel", "arbitrary", "arbitrary")),
        name="grouped_matmul",
    )(*args)


def _rg_gates(xc, n, wax_ref, ba_ref, bx_ref, sp_row):
    cs = slice(n * RG_BLOCK, (n + 1) * RG_BLOCK)
    pre = jnp.dot(xc.astype(BF16), wax_ref[n], preferred_element_type=F32)
    r = _sigmoid(pre[:, :RG_BLOCK] + ba_ref[:, cs])
    i = _sigmoid(pre[:, RG_BLOCK:] + bx_ref[:, cs])
    log_a = -RG_C * r * sp_row[:, cs]
    a = jnp.exp(log_a)
    th = jnp.tanh(log_a)
    p = -2.0 * th
    mult = jnp.where(p > 0.0, p * lax.rsqrt(p * (1.0 - th)), 0.0)
    return a, mult, i


def _rg_long_kernel(xr_ref, yr_ref, cb_ref, h0_ref, cw_ref, cbias_ref, wax_ref, ba_ref, bx_ref,
                    lam_ref, gn_ref, out_ref, nconv_ref, hlast_ref,
                    ext_scr, a_scr, u_scr, h_scr, *, tt, reset_first):
    t = pl.program_id(1)
    nt = pl.num_programs(1)
    pad = SUB
    tail = CONV_W - 1

    @pl.when(t == 0)
    def _():
        ext_scr[pad - tail:pad, :] = cb_ref[0]
        h_scr[...] = h0_ref[0]

    ext_scr[pad:pad + tt, :] = xr_ref[...]
    sp_row = _softplus(-lam_ref[...])
    row = lax.broadcasted_iota(jnp.int32, (tt, RG_BLOCK), 0)
    first_row = jnp.where(t == 0, 0, -1)
    for n in range(RG_BLOCKS):
        cs = slice(n * RG_BLOCK, (n + 1) * RG_BLOCK)
        xc = cbias_ref[:, cs]
        for i in range(CONV_W):
            lo = pad - tail + i
            xc = xc + ext_scr[lo:lo + tt, cs] * cw_ref[i:i + 1, cs]
        a, mult, gate_i = _rg_gates(xc, n, wax_ref, ba_ref, bx_ref, sp_row)
        if reset_first:
            mult = jnp.where(row == first_row, 1.0, mult)
        a_scr[:, cs] = a
        u_scr[:, cs] = mult * gate_i * xc

    row8 = lax.broadcasted_iota(jnp.int32, (SUB, 512), 0)
    ncol = D_RNN // 512

    def scan_body(j, hs):
        r0 = pl.multiple_of(j * SUB, SUB)
        new = []
        for c in range(ncol):
            cs = slice(c * 512, (c + 1) * 512)
            a = a_scr[pl.ds(r0, SUB), cs]
            u = u_scr[pl.ds(r0, SUB), cs]
            for s in (1, 2, 4):
                a_sh = jnp.where(row8 >= s, pltpu.roll(a, s, 0), 1.0)
                u_sh = jnp.where(row8 >= s, pltpu.roll(u, s, 0), 0.0)
                u = a * u_sh + u
                a = a * a_sh
            h = a * hs[c] + u
            u_scr[pl.ds(r0, SUB), cs] = h
            new.append(h[SUB - 1:SUB, :])
        return tuple(new)

    h_in = tuple(h_scr[:, c * 512:(c + 1) * 512] for c in range(ncol))
    h_fin = lax.fori_loop(0, tt // SUB, scan_body, h_in)
    for c in range(ncol):
        h_scr[:, c * 512:(c + 1) * 512] = h_fin[c]

    y = u_scr[...] * _gelu(yr_ref[...])
    ms = jnp.mean(y * y, axis=-1, keepdims=True)
    out_ref[...] = (y * lax.rsqrt(ms + EPS) * gn_ref[...]).astype(out_ref.dtype)
    ext_scr[pad - tail:pad, :] = ext_scr[pad + tt - tail:pad + tt, :]

    @pl.when(t == nt - 1)
    def _():
        nconv_ref[0] = ext_scr[pad + tt - tail:pad + tt, :]
        hlast_ref[0] = h_scr[...]


def rg_long(proj, conv_buf, h0, wts, *, batch, seq, tt, reset_first):
    nt = seq // tt
    bsel = (lambda b: b) if conv_buf.shape[0] == batch else (lambda b: 0)
    vec = lambda r: pl.BlockSpec((r, D_RNN), lambda b, t: (0, 0))
    return pl.pallas_call(
        functools.partial(_rg_long_kernel, tt=tt, reset_first=reset_first),
        out_shape=(jax.ShapeDtypeStruct((batch * seq, D_RNN), BF16),
                   jax.ShapeDtypeStruct((batch, CONV_W - 1, D_RNN), F32),
                   jax.ShapeDtypeStruct((batch, 1, D_RNN), F32)),
        grid=(batch, nt),
        in_specs=[pl.BlockSpec((tt, D_RNN), lambda b, t: (b * nt + t, 0)),
                  pl.BlockSpec((tt, D_RNN), lambda b, t: (b * nt + t, 1)),
                  pl.BlockSpec((1, CONV_W - 1, D_RNN), lambda b, t: (bsel(b), 0, 0)),
                  pl.BlockSpec((1, 1, D_RNN), lambda b, t: (bsel(b), 0, 0)),
                  vec(CONV_W), vec(1),
                  pl.BlockSpec((RG_BLOCKS, RG_BLOCK, 2 * RG_BLOCK), lambda b, t: (0, 0, 0)),
                  vec(1), vec(1), vec(1), vec(1)],
        out_specs=(pl.BlockSpec((tt, D_RNN), lambda b, t: (b * nt + t, 0)),
                   pl.BlockSpec((1, CONV_W - 1, D_RNN), lambda b, t: (b, 0, 0)),
                   pl.BlockSpec((1, 1, D_RNN), lambda b, t: (b, 0, 0))),
        scratch_shapes=[pltpu.VMEM((tt + SUB, D_RNN), F32),
                        pltpu.VMEM((tt, D_RNN), F32),
                        pltpu.VMEM((tt, D_RNN), F32),
                        pltpu.VMEM((1, D_RNN), F32)],
        compiler_params=_params(("parallel", "arbitrary")),
        name="rg_long",
    )(proj, proj, conv_buf, h0, wts["rg_conv_w"], wts["rg_conv_b"], wts["rg_wax"],
      wts["rg_ba"], wts["rg_bx"], wts["rg_lambda"], wts["rg_out_norm"])


def _rg_short_kernel(xr_ref, yr_ref, cb_ref, h0_ref, cw_ref, cbias_ref, wax_ref, ba_ref, bx_ref,
                     lam_ref, gn_ref, out_ref, nconv_ref, hlast_ref, y_scr, *, nb, seq):
    sp_row = _softplus(-lam_ref[...])
    tail = CONV_W - 1
    for n in range(RG_BLOCKS):
        cs = slice(n * RG_BLOCK, (n + 1) * RG_BLOCK)
        ext = [cb_ref[i, :, cs] for i in range(tail)]
        ext += [xr_ref[t * nb:(t + 1) * nb, cs] for t in range(seq)]
        h = h0_ref[:, cs]
        for t in range(seq):
            xc = cbias_ref[:, cs]
            for i in range(CONV_W):
                xc = xc + ext[t + i] * cw_ref[i:i + 1, cs]
            a, mult, gate_i = _rg_gates(xc, n, wax_ref, ba_ref, bx_ref, sp_row)
            h = a * h + mult * gate_i * xc
            y_scr[t, :, cs] = h * _gelu(yr_ref[t * nb:(t + 1) * nb, cs])
        hlast_ref[:, cs] = h
        for i in range(tail):
            nconv_ref[i, :, cs] = ext[seq + i]
    for t in range(seq):
        y = y_scr[t]
        ms = jnp.mean(y * y, axis=-1, keepdims=True)
        out_ref[t * nb:(t + 1) * nb, :] = (y * lax.rsqrt(ms + EPS) * gn_ref[...]).astype(out_ref.dtype)


def rg_short(proj, conv_buf_t, h0, wts, *, nb, seq):
    rows = nb * seq
    vec = lambda r: pl.BlockSpec((r, D_RNN), lambda i: (0, 0))
    return pl.pallas_call(
        functools.partial(_rg_short_kernel, nb=nb, seq=seq),
        out_shape=(jax.ShapeDtypeStruct((rows, D_RNN), BF16),
                   jax.ShapeDtypeStruct((CONV_W - 1, nb, D_RNN), F32),
                   jax.ShapeDtypeStruct((nb, D_RNN), F32)),
        grid=(1,),
        in_specs=[pl.BlockSpec((rows, D_RNN), lambda i: (0, 0)),
                  pl.BlockSpec((rows, D_RNN), lambda i: (0, 1)),
                  pl.BlockSpec((CONV_W - 1, nb, D_RNN), lambda i: (0, 0, 0)),
                  pl.BlockSpec((nb, D_RNN), lambda i: (0, 0)),
                  vec(CONV_W), vec(1),
                  pl.BlockSpec((RG_BLOCKS, RG_BLOCK, 2 * RG_BLOCK), lambda i: (0, 0, 0)),
                  vec(1), vec(1), vec(1), vec(1)],
        out_specs=(pl.BlockSpec((rows, D_RNN), lambda i: (0, 0)),
                   pl.BlockSpec((CONV_W - 1, nb, D_RNN), lambda i: (0, 0, 0)),
                   pl.BlockSpec((nb, D_RNN), lambda i: (0, 0))),
        scratch_shapes=[pltpu.VMEM((seq, nb, D_RNN), F32)],
        compiler_params=_params(("arbitrary",)),
        name="rg_short",
    )(proj, proj, conv_buf_t, h0, wts["rg_conv_w"], wts["rg_conv_b"], wts["rg_wax"],
      wts["rg_ba"], wts["rg_bx"], wts["rg_lambda"], wts["rg_out_norm"])


def _gla_consts():
    er = lax.broadcasted_iota(jnp.int32, (LANES, LANES), 0)
    ec = lax.broadcasted_iota(jnp.int32, (LANES, LANES), 1)
    eye = jnp.where(er == ec, 1.0, 0.0).astype(BF16)
    ones = jnp.ones((LANES, LANES), BF16)
    row = lax.broadcasted_iota(jnp.int32, (GLA_SUB, LANES), 0)
    row8 = lax.broadcasted_iota(jnp.int32, (SUB, LANES), 0)
    pr = lax.broadcasted_iota(jnp.int32, (2 * GLA_SUB, LANES), 0)
    pick = jnp.where((pr == GLA_SUB) | (pr == GLA_SUB + 1), 1.0, 0.0).astype(BF16)
    return eye, ones, row, row8, pick


def _gla_local(units, consts):
    eye, ones, row, row8, pick = consts
    assert len(units) == GLA_G

    def scan8(x):
        for s in (1, 2, 4):
            x = x + jnp.where(row8 >= s, pltpu.roll(x, s, 0), 0.0)
        return x

    halves = (slice(0, SUB), slice(SUB, GLA_SUB))
    nt = (((1,), (1,)), ((), ()))
    qes, xs, qas, kbs, cum_all = [], [], [], [], []
    for q, k, v, g, j_range in units:
        cum_lo = scan8(g[halves[0]])
        edge = cum_lo[SUB - 1:SUB, :]
        cum_hi = scan8(g[halves[1]]) + edge
        cums = (cum_lo, cum_hi)
        cum_all.append(cums)
        last = cum_hi[SUB - 1:SUB, :]
        qes.append(jnp.concatenate([q[h] * jnp.exp(c) for h, c in zip(halves, cums)],
                                   axis=0).astype(BF16))
        qas.append(q[halves[1]] * jnp.exp(cum_hi - edge))
        kbs.append(k[halves[0]] * jnp.exp(edge - cum_lo))
        ke = jnp.concatenate([k[h] * jnp.exp(last - c) for h, c in zip(halves, cums)], axis=0)
        el = jnp.exp(last)
        e_hi = el.astype(BF16).astype(F32)
        e_lo = el - e_hi
        extra = jnp.where(row == 0, e_hi, jnp.where(row == 1, e_lo, 0.0))
        xs.append(jnp.concatenate([ke, extra], axis=0).astype(BF16))
    crossing = [u[4][0] < SUB <= u[4][-1] for u in units]
    sc = None
    if any(crossing):
        qa = jnp.concatenate(qas, axis=0).astype(BF16)
        kb = jnp.concatenate(kbs, axis=0).astype(BF16)
        sc = lax.dot_general(qa, kb, nt, preferred_element_type=F32)
    xt = lax.dot_general(eye, jnp.concatenate(xs, axis=0), nt, preferred_element_type=F32)
    o_halves = []
    for (q, k, v, g, j_range), cums in zip(units, cum_all):
        ps = []
        for j in j_range:
            h, c, jj = halves[j // SUB], cums[j // SUB], j % SUB
            w = jnp.exp(jnp.where(row8 >= jj, c - c[jj:jj + 1, :], -jnp.inf))
            ps.append(q[h] * w * k[j:j + 1, :])
        att = jnp.dot(jnp.concatenate(ps, axis=0).astype(BF16), ones, preferred_element_type=F32)
        o_half = [None, None]
        for idx, j in enumerate(j_range):
            term = att[idx * SUB:(idx + 1) * SUB, :] * v[j:j + 1, :]
            o_half[j // SUB] = term if o_half[j // SUB] is None else o_half[j // SUB] + term
        o_halves.append(o_half)
    cross = None
    if sc is not None:
        n = GLA_G * SUB
        ur = lax.broadcasted_iota(jnp.int32, (n, n), 0) // SUB
        uc = lax.broadcasted_iota(jnp.int32, (n, n), 1) // SUB
        sc = jnp.where(ur == uc, sc, 0.0).astype(BF16)
        v_lo = jnp.concatenate([u[2][halves[0]] for u in units], axis=0).astype(BF16)
        cross = jnp.dot(sc, v_lo, preferred_element_type=F32)
    xt = xt.astype(BF16)
    zblk = jnp.zeros((2 * GLA_SUB, LANES), BF16)
    wrows = []
    for u, (_, _, v, _, _) in enumerate(units):
        vpad = jnp.concatenate([v, jnp.zeros_like(v)], axis=0).astype(BF16)
        wrows.append(jnp.concatenate([zblk] * (2 * u) + [vpad, pick]
                                     + [zblk] * (2 * (GLA_G - 1 - u)), axis=1))
    kd = jnp.dot(xt, jnp.concatenate(wrows, axis=0), preferred_element_type=F32)
    out = []
    for u, o_half in enumerate(o_halves):
        if cross is not None and crossing[u]:
            o_half[1] = o_half[1] + cross[u * SUB:(u + 1) * SUB, :]
        o = jnp.concatenate([jnp.zeros((SUB, LANES), F32) if t is None else t for t in o_half],
                            axis=0)
        kv = kd[:, (2 * u) * LANES:(2 * u + 1) * LANES]
        dec = kd[:, (2 * u + 1) * LANES:(2 * u + 2) * LANES]
        out.append((qes[u], o, kv, dec))
    return out


def _gla_log_alpha(z_ref, wa2_ref, ba2_ref):
    zz = jnp.dot(z_ref[...].astype(BF16), wa2_ref[...], preferred_element_type=F32) + ba2_ref[...]
    return (jnp.minimum(zz, 0.0) - jnp.log1p(jnp.exp(-jnp.abs(zz)))) * (1.0 / GLA_GATE_TAU)


def _gla_finish(o, go, gain):
    ms = jnp.mean(o * o, axis=-1, keepdims=True)
    on = o * lax.rsqrt(ms + EPS) * gain
    return on * (go * _sigmoid(go))


def _gla_long_kernel(q_ref, k_ref, v_ref, go_ref, z_ref, wa2_ref, ba2_ref, gain_ref, s0_ref,
                     out_ref, sout_ref, s_scr, gl_scr, o_scr, qe_scr, kv_scr, dec_scr, sall_scr,
                     *, tt):
    t = pl.program_id(2)
    nt = pl.num_programs(2)
    nsub = tt // GLA_SUB
    consts = _gla_consts()
    scale = GLA_DK ** -0.5

    @pl.when(t == 0)
    def _():
        s_scr[...] = s0_ref[0]

    gl_scr[...] = _gla_log_alpha(z_ref, wa2_ref, ba2_ref)

    def local_body(i, carry):
        rows = pl.ds(pl.multiple_of(i * GLA_SUB, GLA_SUB), GLA_SUB)
        heads = [slice(h * LANES, (h + 1) * LANES) for h in range(GLA_G)]
        units = [(q_ref[rows, cs] * scale, k_ref[rows, cs], v_ref[rows, cs], gl_scr[rows, cs],
                  range(GLA_SUB)) for cs in heads]
        for h, (qe, o, kv, dec) in enumerate(_gla_local(units, consts)):
            cs = heads[h]
            qe_scr[rows, cs] = qe
            o_scr[rows, cs] = o
            kv_scr[i, h] = kv
            dec_scr[i, h] = dec
        return carry

    lax.fori_loop(0, nsub, local_body, 0, unroll=min(4, nsub))

    for h in range(GLA_G):
        def state_body(i, s, h=h):
            sall_scr[i, h] = s.astype(BF16)
            return dec_scr[i, h] * s + kv_scr[i, h]
        s_scr[h] = lax.fori_loop(0, nsub, state_body, s_scr[h])

    def inter_body(i, carry):
        rows = pl.ds(pl.multiple_of(i * GLA_SUB, GLA_SUB), GLA_SUB)
        for h in range(GLA_G):
            cs = slice(h * LANES, (h + 1) * LANES)
            o_scr[rows, cs] += jnp.dot(qe_scr[rows, cs], sall_scr[i, h], preferred_element_type=F32)
        return carry

    lax.fori_loop(0, nsub, inter_body, 0, unroll=min(8, nsub))
    for h in range(GLA_G):
        cs = slice(h * LANES, (h + 1) * LANES)
        out_ref[:, cs] = _gla_finish(o_scr[:, cs], go_ref[:, cs], gain_ref[:, cs]).astype(out_ref.dtype)

    @pl.when(t == nt - 1)
    def _():
        sout_ref[0] = s_scr[...]


def gla_long(proj, z, s0, wts, *, batch, seq, tt):
    nt = seq // tt
    gw = GLA_G * LANES
    base = 2 * D_RNN // gw
    per = D_GLA // gw
    bsel = (lambda b: b) if s0.shape[0] == batch else (lambda b: 0)
    col = lambda which: pl.BlockSpec((tt, gw), lambda b, h, t: (b * nt + t, base + which * per + h))
    return pl.pallas_call(
        functools.partial(_gla_long_kernel, tt=tt),
        out_shape=(jax.ShapeDtypeStruct((batch * seq, D_GLA), BF16),
                   jax.ShapeDtypeStruct((batch, GLA_HEADS, GLA_DK, GLA_DK), F32)),
        grid=(batch, GLA_HEADS // GLA_G, nt),
        in_specs=[col(0), col(1), col(2), col(3),
                  pl.BlockSpec((tt, LANES), lambda b, h, t: (b * nt + t, 0)),
                  pl.BlockSpec((LANES, gw), lambda b, h, t: (0, h)),
                  pl.BlockSpec((1, gw), lambda b, h, t: (0, h)),
                  pl.BlockSpec((1, gw), lambda b, h, t: (0, h)),
                  pl.BlockSpec((1, GLA_G, GLA_DK, GLA_DK), lambda b, h, t: (bsel(b), h, 0, 0))],
        out_specs=(pl.BlockSpec((tt, gw), lambda b, h, t: (b * nt + t, h)),
                   pl.BlockSpec((1, GLA_G, GLA_DK, GLA_DK), lambda b, h, t: (b, h, 0, 0))),
        scratch_shapes=[pltpu.VMEM((GLA_G, GLA_DK, GLA_DK), F32),
                        pltpu.VMEM((tt, gw), F32),
                        pltpu.VMEM((tt, gw), F32),
                        pltpu.VMEM((tt, gw), BF16),
                        pltpu.VMEM((tt // GLA_SUB, GLA_G, GLA_DK, GLA_DK), F32),
                        pltpu.VMEM((tt // GLA_SUB, GLA_G, GLA_DK, GLA_DK), F32),
                        pltpu.VMEM((tt // GLA_SUB, GLA_G, GLA_DK, GLA_DK), BF16)],
        compiler_params=_params(("parallel", "parallel", "arbitrary")),
        name="gla_long",
    )(proj, proj, proj, proj, z, wts["gla_wa2"], wts["gla_ba2"], wts["gla_head_norm"], s0)


def _gla_short_kernel(q_ref, k_ref, v_ref, go_ref, z_ref, wa2_ref, ba2_ref, gain_ref, s0_ref,
                      out_ref, sout_ref, *, seq):
    consts = _gla_consts()
    row = consts[2]
    scale = GLA_DK ** -0.5
    gl = _gla_log_alpha(z_ref, wa2_ref, ba2_ref)
    nseq = GLA_SUB // seq
    assert nseq == GLA_G
    local = {}
    for h in range(GLA_G):
        cs = slice(h * LANES, (h + 1) * LANES)
        q = q_ref[:, cs] * scale
        k = k_ref[:, cs]
        v = v_ref[:, cs]
        g = gl[:, cs]
        units = []
        for b in range(nseq):
            mine = (row >= b * seq) & (row < (b + 1) * seq)
            zero = lambda a, mine=mine: jnp.where(mine, a, 0.0)
            units.append((zero(q), zero(k), zero(v), zero(g), range(b * seq, (b + 1) * seq)))
        for b, res in enumerate(_gla_local(units, consts)):
            local[b, h] = res
    for h in range(GLA_G):
        cs = slice(h * LANES, (h + 1) * LANES)
        o = jnp.zeros((GLA_SUB, LANES), F32)
        for b in range(nseq):
            qe, o_b, kv, dec = local[b, h]
            s = s0_ref[b, h]
            o = o + o_b + jnp.dot(qe, s.astype(BF16), preferred_element_type=F32)
            sout_ref[b, h] = dec * s + kv
        out_ref[:, cs] = _gla_finish(o, go_ref[:, cs], gain_ref[:, cs]).astype(out_ref.dtype)


def gla_short(proj, z, s0, wts, *, nb, seq):
    gw = GLA_G * LANES
    per = D_GLA // gw
    bpb = GLA_SUB // seq
    col = lambda which: pl.BlockSpec((GLA_SUB, gw), lambda i, h: (i, which * per + h))
    return pl.pallas_call(
        functools.partial(_gla_short_kernel, seq=seq),
        out_shape=(jax.ShapeDtypeStruct((nb * seq, D_GLA), BF16),
                   jax.ShapeDtypeStruct((nb, GLA_HEADS, GLA_DK, GLA_DK), F32)),
        grid=(nb // bpb, GLA_HEADS // GLA_G),
        in_specs=[col(0), col(1), col(2), col(3),
                  pl.BlockSpec((GLA_SUB, LANES), lambda i, h: (i, 0)),
                  pl.BlockSpec((LANES, gw), lambda i, h: (0, h)),
                  pl.BlockSpec((1, gw), lambda i, h: (0, h)),
                  pl.BlockSpec((1, gw), lambda i, h: (0, h)),
                  pl.BlockSpec((bpb, GLA_G, GLA_DK, GLA_DK), lambda i, h: (i, h, 0, 0))],
        out_specs=(pl.BlockSpec((GLA_SUB, gw), lambda i, h: (i, h)),
                   pl.BlockSpec((bpb, GLA_G, GLA_DK, GLA_DK), lambda i, h: (i, h, 0, 0))),
        compiler_params=_params(("parallel", "parallel")),
        name="gla_short",
    )(proj, proj, proj, proj, z, wts["gla_wa2"], wts["gla_ba2"], wts["gla_head_norm"], s0)


def _ffn_kernel(xm_ref, xs_ref, xp_ref, wg_ref, wu_ref, cw_ref, cb_ref, bufs_ref,
                hp_ref, hs_ref, nbufp_ref, nbufs_ref, extp_scr, exts_scr, mhist_scr,
                *, steps, tiles_per_seq, nb):
    i = pl.program_id(1)
    (m_first, _), (s_first, _), (p_first, p_count) = steps
    tail = FFN_CONV_W - 1
    pad = SUB
    tm = xp_ref.shape[0]
    rows_s = xs_ref.shape[0]
    hist_s = tail * nb

    def conv(ext_ref, start, shift, rows):
        gc = cb_ref[...]
        for c in range(FFN_CONV_W):
            lo = start - (tail - c) * shift
            gc = gc + ext_ref[lo:lo + rows, :] * cw_ref[c:c + 1, :]
        return gc

    @pl.when(i == m_first)
    def _():
        g = jnp.dot(xm_ref[...], wg_ref[...], preferred_element_type=F32)
        mhist_scr[...] = g[g.shape[0] - tail:, :]

    @pl.when((i >= p_first) & (i < p_first + p_count))
    def _():
        @pl.when((i - p_first) % tiles_per_seq == 0)
        def _():
            extp_scr[pad - tail:pad, :] = mhist_scr[...]

        x = xp_ref[...]
        extp_scr[pad:pad + tm, :] = jnp.dot(x, wg_ref[...], preferred_element_type=F32)
        up = jnp.dot(x, wu_ref[...], preferred_element_type=F32)
        hp_ref[...] = (_gelu(conv(extp_scr, pad, 1, tm)) * up).astype(hp_ref.dtype)
        last = extp_scr[pad + tm - tail:pad + tm, :]
        extp_scr[pad - tail:pad, :] = last
        nbufp_ref[0] = last

    @pl.when(i == s_first)
    def _():
        x = xs_ref[...]
        exts_scr[0:hist_s, :] = bufs_ref[...]
        exts_scr[hist_s:hist_s + rows_s, :] = jnp.dot(x, wg_ref[...], preferred_element_type=F32)
        up = jnp.dot(x, wu_ref[...], preferred_element_type=F32)
        hs_ref[...] = (_gelu(conv(exts_scr, hist_s, nb, rows_s)) * up).astype(hs_ref.dtype)
        nbufs_ref[...] = exts_scr[rows_s:rows_s + hist_s, :]


def ffn_gate(xm, xs, xp, bufs, wts, *, tm, tn, seq_rows, nb):
    rows_p, rows_s = xp.shape[0], xs.shape[0]
    tail = FFN_CONV_W - 1
    steps, ni = _group_steps([(xm.shape[0], xm.shape[0]), (rows_s, rows_s), (rows_p, tm)])
    (_, _), (_, _), (p_first, p_count) = steps
    tps = seq_rows // tm
    ptile = lambda i: _tile_index(i, p_first, p_count)
    const = lambda shape: pl.BlockSpec(shape, lambda j, i: (0, 0), pipeline_mode=pl.Buffered(1))
    colblk = lambda r: pl.BlockSpec((r, tn), lambda j, i: (0, j))
    return pl.pallas_call(
        functools.partial(_ffn_kernel, steps=tuple(steps), tiles_per_seq=tps, nb=nb),
        out_shape=(jax.ShapeDtypeStruct((rows_p, D_FF), BF16),
                   jax.ShapeDtypeStruct((rows_s, D_FF), BF16),
                   jax.ShapeDtypeStruct((rows_p // seq_rows, tail, D_FF), F32),
                   jax.ShapeDtypeStruct((tail * nb, D_FF), F32)),
        grid=(D_FF // tn, ni),
        in_specs=[const(xm.shape),
                  const(xs.shape),
                  pl.BlockSpec((tm, D_MODEL), lambda j, i: (ptile(i), 0)),
                  colblk(D_MODEL), colblk(D_MODEL), colblk(FFN_CONV_W), colblk(1),
                  colblk(tail * nb)],
        out_specs=(pl.BlockSpec((tm, tn), lambda j, i: (ptile(i), j)),
                   colblk(rows_s),
                   pl.BlockSpec((1, tail, tn), lambda j, i: (ptile(i) // tps, 0, j)),
                   colblk(tail * nb)),
        scratch_shapes=[pltpu.VMEM((tm + SUB, tn), F32),
                        pltpu.VMEM((rows_s + tail * nb, tn), F32),
                        pltpu.VMEM((tail, tn), F32)],
        compiler_params=_params(("parallel", "arbitrary")),
        name="ffn_gate",
    )(xm, xs, xp, wts["w_gate"], wts["w_up"], wts["ffn_conv_w"], wts["ffn_conv_b"], bufs)


PROMPT_TM = 1024
FFN_TM = 512
FFN_TN = 512
SEQ_TT = 256
NORM_TM = 256


def kernel(x_prompt, x_sample, state_rglru_conv, state_rglru_h, state_gla_S, state_ffn_conv,
           meta_tokens, norm_mix, w_in, rg_conv_w, rg_conv_b, rg_wa, rg_ba, rg_wx, rg_bx, rg_lambda,
           rg_out_norm, gla_wa2, gla_ba2, gla_head_norm, w_out, norm_ffn, w_gate, w_up,
           ffn_conv_w, ffn_conv_b, w_down, final_norm):
    batch, seq, _ = x_prompt.shape
    nb, sseq, _ = x_sample.shape
    row = lambda a: a.reshape(1, -1)
    wts = {
        "w_in": w_in[0, :, :PROJ_COLS].astype(BF16),
        "w_z": jnp.pad(w_in[0, :, PROJ_COLS:].astype(BF16), ((0, 0), (0, LANES - GLA_RANK))),
        "w_gate": w_gate[0].astype(BF16), "w_up": w_up[0].astype(BF16),
        "w_down": w_down[0].astype(BF16),
        "rg_conv_w": rg_conv_w[0], "rg_conv_b": row(rg_conv_b[0]),
        "rg_wax": jnp.concatenate([rg_wa[0], rg_wx[0]], axis=-1).astype(BF16),
        "rg_ba": row(rg_ba[0]), "rg_bx": row(rg_bx[0]), "rg_lambda": row(rg_lambda[0]),
        "rg_out_norm": row(rg_out_norm[0]),
        "gla_wa2": jnp.pad(gla_wa2[0].astype(BF16), ((0, LANES - GLA_RANK), (0, 0))),
        "gla_ba2": row(gla_ba2[0]), "gla_head_norm": row(gla_head_norm[0]),
        "ffn_conv_w": ffn_conv_w[0], "ffn_conv_b": row(ffn_conv_b[0]),
    }

    def to_batch_major(a):
        return jnp.swapaxes(a.reshape(sseq, nb, -1), 0, 1).reshape(nb * sseq, -1)

    def to_time_major(a):
        return jnp.swapaxes(a.reshape(nb, sseq, -1), 0, 1).reshape(nb * sseq, -1)

    rows_p, rows_s = batch * seq, nb * sseq
    xs = [meta_tokens, to_time_major(x_sample), x_prompt.reshape(rows_p, D_MODEL)]
    tms = [N_META, rows_s, PROMPT_TM]
    norm_tms = [N_META, NORM_TM, NORM_TM]

    def norm_all(arrs, g, dtype):
        return [rmsnorm(a, g, dtype, t) for a, t in zip(arrs, norm_tms)]

    xn = norm_all(xs, norm_mix[0], BF16)
    proj_m, proj_s, proj_p = grouped_matmul(
        [dict(parts=[a], tm=t) for a, t in zip(xn, tms)], wts["w_in"], tn=1024, n_cols=PROJ_COLS)
    z_m, z_s, z_p = [matmul(a, wts["w_z"], tm=t, tn=LANES, tk=D_MODEL) for a, t in zip(xn, tms)]

    zeros = lambda *s: jnp.zeros(s, F32)
    rnn_m, m_conv, m_h = rg_long(proj_m, zeros(1, CONV_W - 1, D_RNN), zeros(1, 1, D_RNN), wts,
                                 batch=1, seq=N_META, tt=N_META, reset_first=True)
    gla_m, m_s = gla_long(proj_m, z_m, zeros(1, GLA_HEADS, GLA_DK, GLA_DK), wts, batch=1,
                          seq=N_META, tt=N_META)
    rnn_p, p_conv, p_h = rg_long(proj_p, m_conv, m_h, wts, batch=batch, seq=seq, tt=SEQ_TT,
                                 reset_first=False)
    gla_p, p_s = gla_long(proj_p, z_p, m_s, wts, batch=batch, seq=seq, tt=SEQ_TT)
    rnn_s, s_conv_t, s_h = rg_short(proj_s, jnp.swapaxes(state_rglru_conv[0], 0, 1),
                                    state_rglru_h[0], wts, nb=nb, seq=sseq)
    gla_sb, s_s = gla_short(to_batch_major(proj_s[:, 2 * D_RNN:]), to_batch_major(z_s),
                            state_gla_S[0], wts, nb=nb, seq=sseq)
    gla_s = to_time_major(gla_sb)

    x1 = grouped_matmul(
        [dict(parts=[r, g], tm=t, res=x)
         for r, g, t, x in zip([rnn_m, rnn_s, rnn_p], [gla_m, gla_s, gla_p], tms, xs)],
        w_out, tn=512, n_cols=D_MODEL, cast_w=True)
    xn2 = norm_all(x1, norm_ffn[0], BF16)
    tail = FFN_CONV_W - 1
    bufs = jnp.swapaxes(state_ffn_conv[0], 0, 1).reshape(tail * nb, D_FF)
    hid_p, hid_s, p_ffn, s_ffn_t = ffn_gate(xn2[0], xn2[1], xn2[2], bufs, wts,
                                            tm=FFN_TM, tn=FFN_TN, seq_rows=seq, nb=nb)
    x2_s, x2_p = grouped_matmul(
        [dict(parts=[hid_s], tm=rows_s, res=x1[1]), dict(parts=[hid_p], tm=PROMPT_TM, res=x1[2])],
        wts["w_down"], tn=1024, n_cols=D_MODEL, tk=2048)
    y_p = rmsnorm(x2_p, final_norm, F32, NORM_TM)
    y_s = to_batch_major(rmsnorm(x2_s, final_norm, F32, NORM_TM))

    return (y_p.reshape(batch, seq, D_MODEL), y_s.reshape(nb, sseq, D_MODEL),
            p_conv[None], p_h.reshape(1, batch, D_RNN), p_s[None], p_ffn[None],
            jnp.swapaxes(s_conv_t, 0, 1)[None], s_h[None], s_s[None],
            jnp.swapaxes(s_ffn_t.reshape(tail, nb, D_FF), 0, 1)[None])
```

```python
import functools
import math

import jax
import jax.numpy as jnp
from jax import lax
from jax.experimental import pallas as pl
from jax.experimental.pallas import tpu as pltpu

F32 = jnp.float32
BF16 = jnp.bfloat16

D_MODEL = 4096
N_META = 16
D_RNN = 2048
D_GLA = 2048
RG_BLOCKS = 16
RG_BLOCK = 128
CONV_W = 4
RG_C = 8.0
GLA_HEADS = 16
GLA_DK = 128
GLA_RANK = 16
GLA_GATE_TAU = 16.0
D_FF = 3 * D_MODEL
FFN_CONV_W = 3
EPS = 1e-6
PROJ_COLS = 2 * D_RNN + 4 * D_GLA

LANES = 128
SUB = 8
VMEM_LIMIT = 60 * 1024 * 1024
GLA_SUB = 16
GLA_G = 4


def _params(sem):
    return pltpu.CompilerParams(dimension_semantics=sem, vmem_limit_bytes=VMEM_LIMIT)


def _gelu(x):
    c = math.sqrt(2.0 / math.pi)
    return 0.5 * x * (1.0 + jnp.tanh(c * (x + 0.044715 * (x * x * x))))


def _softplus(x):
    return jnp.maximum(x, 0.0) + jnp.log1p(jnp.exp(-jnp.abs(x)))


def _sigmoid(x):
    return 0.5 * (1.0 + jnp.tanh(0.5 * x))


def _rmsnorm_kernel(x_ref, g_ref, o_ref):
    x = x_ref[...]
    ms = jnp.mean(x * x, axis=-1, keepdims=True)
    o_ref[...] = (x * lax.rsqrt(ms + EPS) * g_ref[...]).astype(o_ref.dtype)


def rmsnorm(x, g, out_dtype, tm):
    m, d = x.shape
    return pl.pallas_call(
        _rmsnorm_kernel,
        out_shape=jax.ShapeDtypeStruct((m, d), out_dtype),
        grid=(m // tm,),
        in_specs=[pl.BlockSpec((tm, d), lambda i: (i, 0)),
                  pl.BlockSpec((1, d), lambda i: (0, 0))],
        out_specs=pl.BlockSpec((tm, d), lambda i: (i, 0)),
        compiler_params=_params(("parallel",)),
        name="rmsnorm",
    )(x, g.reshape(1, d))


def _mm_kernel(x_ref, w_ref, *rest, nk, has_res):
    if has_res:
        r_ref, o_ref = rest
    else:
        (o_ref,) = rest
    part = jnp.dot(x_ref[...], w_ref[...], preferred_element_type=F32)
    if nk == 1:
        o_ref[...] = part + r_ref[...] if has_res else part
        return
    k = pl.program_id(2)

    @pl.when(k == 0)
    def _():
        o_ref[...] = part + r_ref[...] if has_res else part

    @pl.when(k > 0)
    def _():
        o_ref[...] += part


def matmul(x, w, *, tm, tn, tk, n_cols=None, res=None):
    m, kdim = x.shape
    n = w.shape[1] if n_cols is None else n_cols
    nk = kdim // tk
    in_specs = [pl.BlockSpec((tm, tk), lambda j, i, k: (i, k)),
                pl.BlockSpec((tk, tn), lambda j, i, k: (k, j))]
    args = [x, w]
    if res is not None:
        in_specs.append(pl.BlockSpec((tm, tn), lambda j, i, k: (i, j)))
        args.append(res)
    return pl.pallas_call(
        functools.partial(_mm_kernel, nk=nk, has_res=res is not None),
        out_shape=jax.ShapeDtypeStruct((m, n), F32),
        grid=(n // tn, m // tm, nk),
        in_specs=in_specs,
        out_specs=pl.BlockSpec((tm, tn), lambda j, i, k: (i, j)),
        compiler_params=_params(("parallel", "parallel", "arbitrary")),
        name="matmul",
    )(*args)


def _group_steps(groups):
    steps, first = [], 0
    for rows, tm in groups:
        steps.append((first, rows // tm))
        first += rows // tm
    return steps, first


def _tile_index(i, first, count):
    return jnp.clip(i - first, 0, count - 1)


def _gmm_kernel(*refs, layout, steps, nk, cast_w):
    i = pl.program_id(1)
    pos = 0
    g_in = []
    for n_parts, has_res in layout:
        xs = refs[pos:pos + n_parts]
        pos += n_parts
        r = refs[pos] if has_res else None
        pos += int(has_res)
        g_in.append((xs, r))
    w_ref = refs[pos]
    outs = refs[pos + 1:pos + 1 + len(layout)]
    if cast_w:
        wsrc = refs[pos + 1 + len(layout)]

        @pl.when(i == 0)
        def _():
            wsrc[...] = w_ref[...].astype(BF16)
    else:
        wsrc = w_ref
    for (xs, r), o_ref, (first, count) in zip(g_in, outs, steps):
        @pl.when((i >= first) & (i < first + count))
        def _(xs=xs, r=r, o_ref=o_ref):
            acc, off = None, 0
            for x_ref in xs:
                kk = x_ref.shape[1]
                part = jnp.dot(x_ref[...], wsrc[off:off + kk, :], preferred_element_type=F32)
                acc = part if acc is None else acc + part
                off += kk
            if r is not None:
                first_val = acc + r[...]
            else:
                first_val = acc
            if nk == 1:
                o_ref[...] = first_val
            else:
                k = pl.program_id(2)

                @pl.when(k == 0)
                def _():
                    o_ref[...] = first_val

                @pl.when(k > 0)
                def _():
                    o_ref[...] += acc


def grouped_matmul(groups, w, *, tn, n_cols, tk=None, cast_w=False):
    kdim = sum(p.shape[1] for p in groups[0]["parts"])
    nk = 1 if tk is None else kdim // tk
    assert not (cast_w and nk > 1)
    steps, ni = _group_steps([(g["parts"][0].shape[0], g["tm"]) for g in groups])
    in_specs, args, layout = [], [], []
    for g, (first, count) in zip(groups, steps):
        assert nk == 1 or len(g["parts"]) == 1
        last_k = nk - 1

        def kidx(i, k, first=first, count=count):
            if nk == 1:
                return 0
            return jnp.where(i < first, 0, jnp.where(i < first + count, k, last_k))
        once = dict(pipeline_mode=pl.Buffered(1)) if (count == 1 and nk == 1) else {}
        for p in g["parts"]:
            kp = p.shape[1] if nk == 1 else tk
            in_specs.append(pl.BlockSpec(
                (g["tm"], kp),
                lambda j, i, k, first=first, count=count, kidx=kidx:
                    (_tile_index(i, first, count), kidx(i, k)), **once))
            args.append(p)
        if g.get("res") is not None:
            in_specs.append(pl.BlockSpec(
                (g["tm"], tn),
                lambda j, i, k, first=first, count=count: (_tile_index(i, first, count), j)))
            args.append(g["res"])
        layout.append((len(g["parts"]), g.get("res") is not None))
    kblk = kdim if nk == 1 else tk
    scratch = [pltpu.VMEM((kblk, tn), BF16)] if cast_w else []
    if w.ndim == 3:
        in_specs.append(pl.BlockSpec((None, kblk, tn), lambda j, i, k: (0, k, j)))
    else:
        in_specs.append(pl.BlockSpec((kblk, tn), lambda j, i, k: (k, j)))
    args.append(w)
    out_specs = [pl.BlockSpec((g["tm"], tn),
                              lambda j, i, k, first=first, count=count:
                                  (_tile_index(i, first, count), j))
                 for g, (first, count) in zip(groups, steps)]
    return pl.pallas_call(
        functools.partial(_gmm_kernel, layout=tuple(layout), steps=tuple(steps), nk=nk,
                          cast_w=cast_w),
        out_shape=[jax.ShapeDtypeStruct((g["parts"][0].shape[0], n_cols), F32) for g in groups],
        grid=(n_cols // tn, ni, nk),
        in_specs=in_specs,
        out_specs=out_specs,
        scratch_shapes=scratch,
        compiler_params=_params(("parallel", "arbitrary", "arbitrary")),
        name="grouped_matmul",
    )(*args)


def _rg_gates(xc, n, wax_ref, ba_ref, bx_ref, sp_row):
    cs = slice(n * RG_BLOCK, (n + 1) * RG_BLOCK)
    pre = jnp.dot(xc.astype(BF16), wax_ref[n], preferred_element_type=F32)
    r = _sigmoid(pre[:, :RG_BLOCK] + ba_ref[:, cs])
    i = _sigmoid(pre[:, RG_BLOCK:] + bx_ref[:, cs])
    log_a = -RG_C * r * sp_row[:, cs]
    a = jnp.exp(log_a)
    th = jnp.tanh(log_a)
    p = -2.0 * th
    mult = jnp.where(p > 0.0, p * lax.rsqrt(p * (1.0 - th)), 0.0)
    return a, mult, i


def _rg_long_kernel(xr_ref, yr_ref, cb_ref, h0_ref, cw_ref, cbias_ref, wax_ref, ba_ref, bx_ref,
                    lam_ref, gn_ref, out_ref, nconv_ref, hlast_ref,
                    ext_scr, a_scr, u_scr, h_scr, *, tt, reset_first):
    t = pl.program_id(1)
    nt = pl.num_programs(1)
    pad = SUB
    tail = CONV_W - 1

    @pl.when(t == 0)
    def _():
        ext_scr[pad - tail:pad, :] = cb_ref[0]
        h_scr[...] = h0_ref[0]

    ext_scr[pad:pad + tt, :] = xr_ref[...]
    sp_row = _softplus(-lam_ref[...])
    row = lax.broadcasted_iota(jnp.int32, (tt, RG_BLOCK), 0)
    first_row = jnp.where(t == 0, 0, -1)
    for n in range(RG_BLOCKS):
        cs = slice(n * RG_BLOCK, (n + 1) * RG_BLOCK)
        xc = cbias_ref[:, cs]
        for i in range(CONV_W):
            lo = pad - tail + i
            xc = xc + ext_scr[lo:lo + tt, cs] * cw_ref[i:i + 1, cs]
        a, mult, gate_i = _rg_gates(xc, n, wax_ref, ba_ref, bx_ref, sp_row)
        if reset_first:
            mult = jnp.where(row == first_row, 1.0, mult)
        a_scr[:, cs] = a
        u_scr[:, cs] = mult * gate_i * xc

    row8 = lax.broadcasted_iota(jnp.int32, (SUB, 512), 0)
    ncol = D_RNN // 512

    def scan_body(j, hs):
        r0 = pl.multiple_of(j * SUB, SUB)
        new = []
        for c in range(ncol):
            cs = slice(c * 512, (c + 1) * 512)
            a = a_scr[pl.ds(r0, SUB), cs]
            u = u_scr[pl.ds(r0, SUB), cs]
            for s in (1, 2, 4):
                a_sh = jnp.where(row8 >= s, pltpu.roll(a, s, 0), 1.0)
                u_sh = jnp.where(row8 >= s, pltpu.roll(u, s, 0), 0.0)
                u = a * u_sh + u
                a = a * a_sh
            h = a * hs[c] + u
            u_scr[pl.ds(r0, SUB), cs] = h
            new.append(h[SUB - 1:SUB, :])
        return tuple(new)

    h_in = tuple(h_scr[:, c * 512:(c + 1) * 512] for c in range(ncol))
    h_fin = lax.fori_loop(0, tt // SUB, scan_body, h_in)
    for c in range(ncol):
        h_scr[:, c * 512:(c + 1) * 512] = h_fin[c]

    y = u_scr[...] * _gelu(yr_ref[...])
    ms = jnp.mean(y * y, axis=-1, keepdims=True)
    out_ref[...] = (y * lax.rsqrt(ms + EPS) * gn_ref[...]).astype(out_ref.dtype)
    ext_scr[pad - tail:pad, :] = ext_scr[pad + tt - tail:pad + tt, :]

    @pl.when(t == nt - 1)
    def _():
        nconv_ref[0] = ext_scr[pad + tt - tail:pad + tt, :]
        hlast_ref[0] = h_scr[...]


def rg_long(proj, conv_buf, h0, wts, *, batch, seq, tt, reset_first):
    nt = seq // tt
    bsel = (lambda b: b) if conv_buf.shape[0] == batch else (lambda b: 0)
    vec = lambda r: pl.BlockSpec((r, D_RNN), lambda b, t: (0, 0))
    return pl.pallas_call(
        functools.partial(_rg_long_kernel, tt=tt, reset_first=reset_first),
        out_shape=(jax.ShapeDtypeStruct((batch * seq, D_RNN), BF16),
                   jax.ShapeDtypeStruct((batch, CONV_W - 1, D_RNN), F32),
                   jax.ShapeDtypeStruct((batch, 1, D_RNN), F32)),
        grid=(batch, nt),
        in_specs=[pl.BlockSpec((tt, D_RNN), lambda b, t: (b * nt + t, 0)),
                  pl.BlockSpec((tt, D_RNN), lambda b, t: (b * nt + t, 1)),
                  pl.BlockSpec((1, CONV_W - 1, D_RNN), lambda b, t: (bsel(b), 0, 0)),
                  pl.BlockSpec((1, 1, D_RNN), lambda b, t: (bsel(b), 0, 0)),
                  vec(CONV_W), vec(1),
                  pl.BlockSpec((RG_BLOCKS, RG_BLOCK, 2 * RG_BLOCK), lambda b, t: (0, 0, 0)),
                  vec(1), vec(1), vec(1), vec(1)],
        out_specs=(pl.BlockSpec((tt, D_RNN), lambda b, t: (b * nt + t, 0)),
                   pl.BlockSpec((1, CONV_W - 1, D_RNN), lambda b, t: (b, 0, 0)),
                   pl.BlockSpec((1, 1, D_RNN), lambda b, t: (b, 0, 0))),
        scratch_shapes=[pltpu.VMEM((tt + SUB, D_RNN), F32),
                        pltpu.VMEM((tt, D_RNN), F32),
                        pltpu.VMEM((tt, D_RNN), F32),
                        pltpu.VMEM((1, D_RNN), F32)],
        compiler_params=_params(("parallel", "arbitrary")),
        name="rg_long",
    )(proj, proj, conv_buf, h0, wts["rg_conv_w"], wts["rg_conv_b"], wts["rg_wax"],
      wts["rg_ba"], wts["rg_bx"], wts["rg_lambda"], wts["rg_out_norm"])


def _rg_short_kernel(xr_ref, yr_ref, cb_ref, h0_ref, cw_ref, cbias_ref, wax_ref, ba_ref, bx_ref,
                     lam_ref, gn_ref, out_ref, nconv_ref, hlast_ref, y_scr, *, nb, seq):
    sp_row = _softplus(-lam_ref[...])
    tail = CONV_W - 1
    for n in range(RG_BLOCKS):
        cs = slice(n * RG_BLOCK, (n + 1) * RG_BLOCK)
        ext = [cb_ref[i, :, cs] for i in range(tail)]
        ext += [xr_ref[t * nb:(t + 1) * nb, cs] for t in range(seq)]
        h = h0_ref[:, cs]
        for t in range(seq):
            xc = cbias_ref[:, cs]
            for i in range(CONV_W):
                xc = xc + ext[t + i] * cw_ref[i:i + 1, cs]
            a, mult, gate_i = _rg_gates(xc, n, wax_ref, ba_ref, bx_ref, sp_row)
            h = a * h + mult * gate_i * xc
            y_scr[t, :, cs] = h * _gelu(yr_ref[t * nb:(t + 1) * nb, cs])
        hlast_ref[:, cs] = h
        for i in range(tail):
            nconv_ref[i, :, cs] = ext[seq + i]
    for t in range(seq):
        y = y_scr[t]
        ms = jnp.mean(y * y, axis=-1, keepdims=True)
        out_ref[t * nb:(t + 1) * nb, :] = (y * lax.rsqrt(ms + EPS) * gn_ref[...]).astype(out_ref.dtype)


def rg_short(proj, conv_buf_t, h0, wts, *, nb, seq):
    rows = nb * seq
    vec = lambda r: pl.BlockSpec((r, D_RNN), lambda i: (0, 0))
    return pl.pallas_call(
        functools.partial(_rg_short_kernel, nb=nb, seq=seq),
        out_shape=(jax.ShapeDtypeStruct((rows, D_RNN), BF16),
                   jax.ShapeDtypeStruct((CONV_W - 1, nb, D_RNN), F32),
                   jax.ShapeDtypeStruct((nb, D_RNN), F32)),
        grid=(1,),
        in_specs=[pl.BlockSpec((rows, D_RNN), lambda i: (0, 0)),
                  pl.BlockSpec((rows, D_RNN), lambda i: (0, 1)),
                  pl.BlockSpec((CONV_W - 1, nb, D_RNN), lambda i: (0, 0, 0)),
                  pl.BlockSpec((nb, D_RNN), lambda i: (0, 0)),
                  vec(CONV_W), vec(1),
                  pl.BlockSpec((RG_BLOCKS, RG_BLOCK, 2 * RG_BLOCK), lambda i: (0, 0, 0)),
                  vec(1), vec(1), vec(1), vec(1)],
        out_specs=(pl.BlockSpec((rows, D_RNN), lambda i: (0, 0)),
                   pl.BlockSpec((CONV_W - 1, nb, D_RNN), lambda i: (0, 0, 0)),
                   pl.BlockSpec((nb, D_RNN), lambda i: (0, 0))),
        scratch_shapes=[pltpu.VMEM((seq, nb, D_RNN), F32)],
        compiler_params=_params(("arbitrary",)),
        name="rg_short",
    )(proj, proj, conv_buf_t, h0, wts["rg_conv_w"], wts["rg_conv_b"], wts["rg_wax"],
      wts["rg_ba"], wts["rg_bx"], wts["rg_lambda"], wts["rg_out_norm"])


def _gla_consts():
    er = lax.broadcasted_iota(jnp.int32, (LANES, LANES), 0)
    ec = lax.broadcasted_iota(jnp.int32, (LANES, LANES), 1)
    eye = jnp.where(er == ec, 1.0, 0.0).astype(BF16)
    ones = jnp.ones((LANES, LANES), BF16)
    row = lax.broadcasted_iota(jnp.int32, (GLA_SUB, LANES), 0)
    row8 = lax.broadcasted_iota(jnp.int32, (SUB, LANES), 0)
    pr = lax.broadcasted_iota(jnp.int32, (2 * GLA_SUB, LANES), 0)
    pick = jnp.where((pr == GLA_SUB) | (pr == GLA_SUB + 1), 1.0, 0.0).astype(BF16)
    return eye, ones, row, row8, pick


def _gla_local(units, consts):
    eye, ones, row, row8, pick = consts
    assert len(units) == GLA_G

    def scan8(x):
        for s in (1, 2, 4):
            x = x + jnp.where(row8 >= s, pltpu.roll(x, s, 0), 0.0)
        return x

    halves = (slice(0, SUB), slice(SUB, GLA_SUB))
    nt = (((1,), (1,)), ((), ()))
    qes, xs, qas, kbs, cum_all = [], [], [], [], []
    for q, k, v, g, j_range in units:
        cum_lo = scan8(g[halves[0]])
        edge = cum_lo[SUB - 1:SUB, :]
        cum_hi = scan8(g[halves[1]]) + edge
        cums = (cum_lo, cum_hi)
        cum_all.append(cums)
        last = cum_hi[SUB - 1:SUB, :]
        qes.append(jnp.concatenate([q[h] * jnp.exp(c) for h, c in zip(halves, cums)],
                                   axis=0).astype(BF16))
        qas.append(q[halves[1]] * jnp.exp(cum_hi - edge))
        kbs.append(k[halves[0]] * jnp.exp(edge - cum_lo))
        ke = jnp.concatenate([k[h] * jnp.exp(last - c) for h, c in zip(halves, cums)], axis=0)
        el = jnp.exp(last)
        e_hi = el.astype(BF16).astype(F32)
        e_lo = el - e_hi
        extra = jnp.where(row == 0, e_hi, jnp.where(row == 1, e_lo, 0.0))
        xs.append(jnp.concatenate([ke, extra], axis=0).astype(BF16))
    crossing = [u[4][0] < SUB <= u[4][-1] for u in units]
    sc = None
    if any(crossing):
        qa = jnp.concatenate(qas, axis=0).astype(BF16)
        kb = jnp.concatenate(kbs, axis=0).astype(BF16)
        sc = lax.dot_general(qa, kb, nt, preferred_element_type=F32)
    xt = lax.dot_general(eye, jnp.concatenate(xs, axis=0), nt, preferred_element_type=F32)
    o_halves = []
    for (q, k, v, g, j_range), cums in zip(units, cum_all):
        ps = []
        for j in j_range:
            h, c, jj = halves[j // SUB], cums[j // SUB], j % SUB
            w = jnp.exp(jnp.where(row8 >= jj, c - c[jj:jj + 1, :], -jnp.inf))
            ps.append(q[h] * w * k[j:j + 1, :])
        att = jnp.dot(jnp.concatenate(ps, axis=0).astype(BF16), ones, preferred_element_type=F32)
        o_half = [None, None]
        for idx, j in enumerate(j_range):
            term = att[idx * SUB:(idx + 1) * SUB, :] * v[j:j + 1, :]
            o_half[j // SUB] = term if o_half[j // SUB] is None else o_half[j // SUB] + term
        o_halves.append(o_half)
    cross = None
    if sc is not None:
        n = GLA_G * SUB
        ur = lax.broadcasted_iota(jnp.int32, (n, n), 0) // SUB
        uc = lax.broadcasted_iota(jnp.int32, (n, n), 1) // SUB
        sc = jnp.where(ur == uc, sc, 0.0).astype(BF16)
        v_lo = jnp.concatenate([u[2][halves[0]] for u in units], axis=0).astype(BF16)
        cross = jnp.dot(sc, v_lo, preferred_element_type=F32)
    xt = xt.astype(BF16)
    zblk = jnp.zeros((2 * GLA_SUB, LANES), BF16)
    wrows = []
    for u, (_, _, v, _, _) in enumerate(units):
        vpad = jnp.concatenate([v, jnp.zeros_like(v)], axis=0).astype(BF16)
        wrows.append(jnp.concatenate([zblk] * (2 * u) + [vpad, pick]
                                     + [zblk] * (2 * (GLA_G - 1 - u)), axis=1))
    kd = jnp.dot(xt, jnp.concatenate(wrows, axis=0), preferred_element_type=F32)
    out = []
    for u, o_half in enumerate(o_halves):
        if cross is not None and crossing[u]:
            o_half[1] = o_half[1] + cross[u * SUB:(u + 1) * SUB, :]
        o = jnp.concatenate([jnp.zeros((SUB, LANES), F32) if t is None else t for t in o_half],
                            axis=0)
        kv = kd[:, (2 * u) * LANES:(2 * u + 1) * LANES]
        dec = kd[:, (2 * u + 1) * LANES:(2 * u + 2) * LANES]
        out.append((qes[u], o, kv, dec))
    return out


def _gla_log_alpha(z_ref, wa2_ref, ba2_ref):
    zz = jnp.dot(z_ref[...].astype(BF16), wa2_ref[...], preferred_element_type=F32) + ba2_ref[...]
    return (jnp.minimum(zz, 0.0) - jnp.log1p(jnp.exp(-jnp.abs(zz)))) * (1.0 / GLA_GATE_TAU)


def _gla_finish(o, go, gain):
    ms = jnp.mean(o * o, axis=-1, keepdims=True)
    on = o * lax.rsqrt(ms + EPS) * gain
    return on * (go * _sigmoid(go))


def _gla_long_kernel(q_ref, k_ref, v_ref, go_ref, z_ref, wa2_ref, ba2_ref, gain_ref, s0_ref,
                     *rest, tt, n_casts):
    cast_in = rest[:n_casts]
    out_ref, sout_ref = rest[n_casts:n_casts + 2]
    cast_out = rest[n_casts + 2:2 * n_casts + 2]
    s_scr, gl_scr, o_scr, qe_scr, kv_scr, dec_scr, sall_scr = rest[2 * n_casts + 2:]
    t = pl.program_id(2)
    nt = pl.num_programs(2)
    nsub = tt // GLA_SUB
    consts = _gla_consts()
    scale = GLA_DK ** -0.5

    for src, dst in zip(cast_in, cast_out):
        dst[...] = src[...].astype(BF16)

    @pl.when(t == 0)
    def _():
        s_scr[...] = s0_ref[0]

    gl_scr[...] = _gla_log_alpha(z_ref, wa2_ref, ba2_ref)

    def local_body(i, carry):
        rows = pl.ds(pl.multiple_of(i * GLA_SUB, GLA_SUB), GLA_SUB)
        heads = [slice(h * LANES, (h + 1) * LANES) for h in range(GLA_G)]
        units = [(q_ref[rows, cs] * scale, k_ref[rows, cs], v_ref[rows, cs], gl_scr[rows, cs],
                  range(GLA_SUB)) for cs in heads]
        for h, (qe, o, kv, dec) in enumerate(_gla_local(units, consts)):
            cs = heads[h]
            qe_scr[rows, cs] = qe
            o_scr[rows, cs] = o
            kv_scr[i, h] = kv
            dec_scr[i, h] = dec
        return carry

    lax.fori_loop(0, nsub, local_body, 0, unroll=min(4, nsub))

    for h in range(GLA_G):
        def state_body(i, s, h=h):
            sall_scr[i, h] = s.astype(BF16)
            return dec_scr[i, h] * s + kv_scr[i, h]
        s_scr[h] = lax.fori_loop(0, nsub, state_body, s_scr[h])

    def inter_body(i, carry):
        rows = pl.ds(pl.multiple_of(i * GLA_SUB, GLA_SUB), GLA_SUB)
        for h in range(GLA_G):
            cs = slice(h * LANES, (h + 1) * LANES)
            o_scr[rows, cs] += jnp.dot(qe_scr[rows, cs], sall_scr[i, h], preferred_element_type=F32)
        return carry

    lax.fori_loop(0, nsub, inter_body, 0, unroll=min(8, nsub))
    for h in range(GLA_G):
        cs = slice(h * LANES, (h + 1) * LANES)
        out_ref[:, cs] = _gla_finish(o_scr[:, cs], go_ref[:, cs], gain_ref[:, cs]).astype(out_ref.dtype)

    @pl.when(t == nt - 1)
    def _():
        sout_ref[0] = s_scr[...]


def gla_long(proj, z, s0, wts, *, batch, seq, tt, casts=()):
    nt = seq // tt
    gw = GLA_G * LANES
    base = 2 * D_RNN // gw
    per = D_GLA // gw
    nhg = GLA_HEADS // GLA_G
    bsel = (lambda b: b) if s0.shape[0] == batch else (lambda b: 0)
    col = lambda which: pl.BlockSpec((tt, gw), lambda b, h, t: (b * nt + t, base + which * per + h))
    n_steps = batch * nhg * nt
    cast_specs = []
    for a in casts:
        rblocks = math.gcd(n_steps, a.shape[0] // 16)
        cblocks = n_steps // rblocks
        blk = (a.shape[0] // rblocks, a.shape[1] // cblocks)
        assert blk[0] * rblocks == a.shape[0] and blk[1] * cblocks == a.shape[1]
        assert blk[0] % 16 == 0 and blk[1] % LANES == 0
        cast_specs.append(pl.BlockSpec(
            blk, lambda b, h, t, cblocks=cblocks:
                (((b * nhg + h) * nt + t) // cblocks, ((b * nhg + h) * nt + t) % cblocks)))
    return pl.pallas_call(
        functools.partial(_gla_long_kernel, tt=tt, n_casts=len(casts)),
        out_shape=(jax.ShapeDtypeStruct((batch * seq, D_GLA), BF16),
                   jax.ShapeDtypeStruct((batch, GLA_HEADS, GLA_DK, GLA_DK), F32),
                   *[jax.ShapeDtypeStruct(a.shape, BF16) for a in casts]),
        grid=(batch, nhg, nt),
        in_specs=[col(0), col(1), col(2), col(3),
                  pl.BlockSpec((tt, LANES), lambda b, h, t: (b * nt + t, 0)),
                  pl.BlockSpec((LANES, gw), lambda b, h, t: (0, h)),
                  pl.BlockSpec((1, gw), lambda b, h, t: (0, h)),
                  pl.BlockSpec((1, gw), lambda b, h, t: (0, h)),
                  pl.BlockSpec((1, GLA_G, GLA_DK, GLA_DK), lambda b, h, t: (bsel(b), h, 0, 0)),
                  *cast_specs],
        out_specs=(pl.BlockSpec((tt, gw), lambda b, h, t: (b * nt + t, h)),
                   pl.BlockSpec((1, GLA_G, GLA_DK, GLA_DK), lambda b, h, t: (b, h, 0, 0)),
                   *cast_specs),
        scratch_shapes=[pltpu.VMEM((GLA_G, GLA_DK, GLA_DK), F32),
                        pltpu.VMEM((tt, gw), F32),
                        pltpu.VMEM((tt, gw), F32),
                        pltpu.VMEM((tt, gw), BF16),
                        pltpu.VMEM((tt // GLA_SUB, GLA_G, GLA_DK, GLA_DK), F32),
                        pltpu.VMEM((tt // GLA_SUB, GLA_G, GLA_DK, GLA_DK), F32),
                        pltpu.VMEM((tt // GLA_SUB, GLA_G, GLA_DK, GLA_DK), BF16)],
        compiler_params=_params(("parallel", "parallel", "arbitrary")),
        name="gla_long",
    )(proj, proj, proj, proj, z, wts["gla_wa2"], wts["gla_ba2"], wts["gla_head_norm"], s0, *casts)


def _gla_short_kernel(q_ref, k_ref, v_ref, go_ref, z_ref, wa2_ref, ba2_ref, gain_ref, s0_ref,
                      out_ref, sout_ref, *, seq):
    consts = _gla_consts()
    row = consts[2]
    scale = GLA_DK ** -0.5
    gl = _gla_log_alpha(z_ref, wa2_ref, ba2_ref)
    nseq = GLA_SUB // seq
    assert nseq == GLA_G
    local = {}
    for h in range(GLA_G):
        cs = slice(h * LANES, (h + 1) * LANES)
        q = q_ref[:, cs] * scale
        k = k_ref[:, cs]
        v = v_ref[:, cs]
        g = gl[:, cs]
        units = []
        for b in range(nseq):
            mine = (row >= b * seq) & (row < (b + 1) * seq)
            zero = lambda a, mine=mine: jnp.where(mine, a, 0.0)
            units.append((zero(q), zero(k), zero(v), zero(g), range(b * seq, (b + 1) * seq)))
        for b, res in enumerate(_gla_local(units, consts)):
            local[b, h] = res
    for h in range(GLA_G):
        cs = slice(h * LANES, (h + 1) * LANES)
        o = jnp.zeros((GLA_SUB, LANES), F32)
        for b in range(nseq):
            qe, o_b, kv, dec = local[b, h]
            s = s0_ref[b, h]
            o = o + o_b + jnp.dot(qe, s.astype(BF16), preferred_element_type=F32)
            sout_ref[b, h] = dec * s + kv
        out_ref[:, cs] = _gla_finish(o, go_ref[:, cs], gain_ref[:, cs]).astype(out_ref.dtype)


def gla_short(proj, z, s0, wts, *, nb, seq):
    gw = GLA_G * LANES
    per = D_GLA // gw
    bpb = GLA_SUB // seq
    col = lambda which: pl.BlockSpec((GLA_SUB, gw), lambda i, h: (i, which * per + h))
    return pl.pallas_call(
        functools.partial(_gla_short_kernel, seq=seq),
        out_shape=(jax.ShapeDtypeStruct((nb * seq, D_GLA), BF16),
                   jax.ShapeDtypeStruct((nb, GLA_HEADS, GLA_DK, GLA_DK), F32)),
        grid=(nb // bpb, GLA_HEADS // GLA_G),
        in_specs=[col(0), col(1), col(2), col(3),
                  pl.BlockSpec((GLA_SUB, LANES), lambda i, h: (i, 0)),
                  pl.BlockSpec((LANES, gw), lambda i, h: (0, h)),
                  pl.BlockSpec((1, gw), lambda i, h: (0, h)),
                  pl.BlockSpec((1, gw), lambda i, h: (0, h)),
                  pl.BlockSpec((bpb, GLA_G, GLA_DK, GLA_DK), lambda i, h: (i, h, 0, 0))],
        out_specs=(pl.BlockSpec((GLA_SUB, gw), lambda i, h: (i, h)),
                   pl.BlockSpec((bpb, GLA_G, GLA_DK, GLA_DK), lambda i, h: (i, h, 0, 0))),
        compiler_params=_params(("parallel", "parallel")),
        name="gla_short",
    )(proj, proj, proj, proj, z, wts["gla_wa2"], wts["gla_ba2"], wts["gla_head_norm"], s0)


def _ffn_kernel(xm_ref, xs_ref, xp_ref, wg_ref, wu_ref, cw_ref, cb_ref, bufs_ref,
                hp_ref, hs_ref, nbufp_ref, nbufs_ref, extp_scr, exts_scr, mhist_scr,
                *, steps, tiles_per_seq, nb):
    i = pl.program_id(1)
    (m_first, _), (s_first, _), (p_first, p_count) = steps
    tail = FFN_CONV_W - 1
    pad = SUB
    tm = xp_ref.shape[0]
    rows_s = xs_ref.shape[0]
    hist_s = tail * nb

    def conv(ext_ref, start, shift, rows):
        gc = cb_ref[...]
        for c in range(FFN_CONV_W):
            lo = start - (tail - c) * shift
            gc = gc + ext_ref[lo:lo + rows, :] * cw_ref[c:c + 1, :]
        return gc

    @pl.when(i == m_first)
    def _():
        g = jnp.dot(xm_ref[...], wg_ref[...], preferred_element_type=F32)
        mhist_scr[...] = g[g.shape[0] - tail:, :]

    @pl.when((i >= p_first) & (i < p_first + p_count))
    def _():
        @pl.when((i - p_first) % tiles_per_seq == 0)
        def _():
            extp_scr[pad - tail:pad, :] = mhist_scr[...]

        x = xp_ref[...]
        extp_scr[pad:pad + tm, :] = jnp.dot(x, wg_ref[...], preferred_element_type=F32)
        up = jnp.dot(x, wu_ref[...], preferred_element_type=F32)
        hp_ref[...] = (_gelu(conv(extp_scr, pad, 1, tm)) * up).astype(hp_ref.dtype)
        last = extp_scr[pad + tm - tail:pad + tm, :]
        extp_scr[pad - tail:pad, :] = last
        nbufp_ref[0] = last

    @pl.when(i == s_first)
    def _():
        x = xs_ref[...]
        exts_scr[0:hist_s, :] = bufs_ref[...]
        exts_scr[hist_s:hist_s + rows_s, :] = jnp.dot(x, wg_ref[...], preferred_element_type=F32)
        up = jnp.dot(x, wu_ref[...], preferred_element_type=F32)
        hs_ref[...] = (_gelu(conv(exts_scr, hist_s, nb, rows_s)) * up).astype(hs_ref.dtype)
        nbufs_ref[...] = exts_scr[rows_s:rows_s + hist_s, :]


def ffn_gate(xm, xs, xp, bufs, wts, *, tm, tn, seq_rows, nb):
    rows_p, rows_s = xp.shape[0], xs.shape[0]
    tail = FFN_CONV_W - 1
    steps, ni = _group_steps([(xm.shape[0], xm.shape[0]), (rows_s, rows_s), (rows_p, tm)])
    (_, _), (_, _), (p_first, p_count) = steps
    tps = seq_rows // tm
    ptile = lambda i: _tile_index(i, p_first, p_count)
    const = lambda shape: pl.BlockSpec(shape, lambda j, i: (0, 0), pipeline_mode=pl.Buffered(1))
    colblk = lambda r: pl.BlockSpec((r, tn), lambda j, i: (0, j))
    return pl.pallas_call(
        functools.partial(_ffn_kernel, steps=tuple(steps), tiles_per_seq=tps, nb=nb),
        out_shape=(jax.ShapeDtypeStruct((rows_p, D_FF), BF16),
                   jax.ShapeDtypeStruct((rows_s, D_FF), BF16),
                   jax.ShapeDtypeStruct((rows_p // seq_rows, tail, D_FF), F32),
                   jax.ShapeDtypeStruct((tail * nb, D_FF), F32)),
        grid=(D_FF // tn, ni),
        in_specs=[const(xm.shape),
                  const(xs.shape),
                  pl.BlockSpec((tm, D_MODEL), lambda j, i: (ptile(i), 0)),
                  colblk(D_MODEL), colblk(D_MODEL), colblk(FFN_CONV_W), colblk(1),
                  colblk(tail * nb)],
        out_specs=(pl.BlockSpec((tm, tn), lambda j, i: (ptile(i), j)),
                   colblk(rows_s),
                   pl.BlockSpec((1, tail, tn), lambda j, i: (ptile(i) // tps, 0, j)),
                   colblk(tail * nb)),
        scratch_shapes=[pltpu.VMEM((tm + SUB, tn), F32),
                        pltpu.VMEM((rows_s + tail * nb, tn), F32),
                        pltpu.VMEM((tail, tn), F32)],
        compiler_params=_params(("parallel", "arbitrary")),
        name="ffn_gate",
    )(xm, xs, xp, wts["w_gate"], wts["w_up"], wts["ffn_conv_w"], wts["ffn_conv_b"], bufs)


PROMPT_TM = 1024
FFN_TM = 512
FFN_TN = 512
SEQ_TT = 256
NORM_TM = 256


def kernel(x_prompt, x_sample, state_rglru_conv, state_rglru_h, state_gla_S, state_ffn_conv,
           meta_tokens, norm_mix, w_in, rg_conv_w, rg_conv_b, rg_wa, rg_ba, rg_wx, rg_bx, rg_lambda,
           rg_out_norm, gla_wa2, gla_ba2, gla_head_norm, w_out, norm_ffn, w_gate, w_up,
           ffn_conv_w, ffn_conv_b, w_down, final_norm):
    batch, seq, _ = x_prompt.shape
    nb, sseq, _ = x_sample.shape
    row = lambda a: a.reshape(1, -1)
    wts = {
        "w_z": jnp.pad(w_in[0, :, PROJ_COLS:].astype(BF16), ((0, 0), (0, LANES - GLA_RANK))),
        "rg_conv_w": rg_conv_w[0], "rg_conv_b": row(rg_conv_b[0]),
        "rg_wax": jnp.concatenate([rg_wa[0], rg_wx[0]], axis=-1).astype(BF16),
        "rg_ba": row(rg_ba[0]), "rg_bx": row(rg_bx[0]), "rg_lambda": row(rg_lambda[0]),
        "rg_out_norm": row(rg_out_norm[0]),
        "gla_wa2": jnp.pad(gla_wa2[0].astype(BF16), ((0, LANES - GLA_RANK), (0, 0))),
        "gla_ba2": row(gla_ba2[0]), "gla_head_norm": row(gla_head_norm[0]),
        "ffn_conv_w": ffn_conv_w[0], "ffn_conv_b": row(ffn_conv_b[0]),
    }

    def to_batch_major(a):
        return jnp.swapaxes(a.reshape(sseq, nb, -1), 0, 1).reshape(nb * sseq, -1)

    def to_time_major(a):
        return jnp.swapaxes(a.reshape(nb, sseq, -1), 0, 1).reshape(nb * sseq, -1)

    rows_p, rows_s = batch * seq, nb * sseq
    xs = [meta_tokens, to_time_major(x_sample), x_prompt.reshape(rows_p, D_MODEL)]
    tms = [N_META, rows_s, PROMPT_TM]
    norm_tms = [N_META, NORM_TM, NORM_TM]

    def norm_all(arrs, g, dtype):
        return [rmsnorm(a, g, dtype, t) for a, t in zip(arrs, norm_tms)]

    xn = norm_all(xs, norm_mix[0], BF16)
    proj_m, proj_s, proj_p = grouped_matmul(
        [dict(parts=[a], tm=t) for a, t in zip(xn, tms)], w_in.reshape(D_MODEL, -1), tn=512,
        n_cols=PROJ_COLS, cast_w=True)
    z_m, z_s, z_p = [matmul(a, wts["w_z"], tm=t, tn=LANES, tk=D_MODEL) for a, t in zip(xn, tms)]

    zeros = lambda *s: jnp.zeros(s, F32)
    rnn_m, m_conv, m_h = rg_long(proj_m, zeros(1, CONV_W - 1, D_RNN), zeros(1, 1, D_RNN), wts,
                                 batch=1, seq=N_META, tt=N_META, reset_first=True)
    gla_m, m_s = gla_long(proj_m, z_m, zeros(1, GLA_HEADS, GLA_DK, GLA_DK), wts, batch=1,
                          seq=N_META, tt=N_META)
    rnn_p, p_conv, p_h = rg_long(proj_p, m_conv, m_h, wts, batch=batch, seq=seq, tt=SEQ_TT,
                                 reset_first=False)
    gla_p, p_s, wts["w_gate"], wts["w_up"], wts["w_down"] = gla_long(
        proj_p, z_p, m_s, wts, batch=batch, seq=seq, tt=SEQ_TT,
        casts=(w_gate.reshape(D_MODEL, D_FF), w_up.reshape(D_MODEL, D_FF),
               w_down.reshape(D_FF, D_MODEL)))
    rnn_s, s_conv_t, s_h = rg_short(proj_s, jnp.swapaxes(state_rglru_conv[0], 0, 1),
                                    state_rglru_h[0], wts, nb=nb, seq=sseq)
    gla_sb, s_s = gla_short(to_batch_major(proj_s[:, 2 * D_RNN:]), to_batch_major(z_s),
                            state_gla_S[0], wts, nb=nb, seq=sseq)
    gla_s = to_time_major(gla_sb)

    x1 = grouped_matmul(
        [dict(parts=[r, g], tm=t, res=x)
         for r, g, t, x in zip([rnn_m, rnn_s, rnn_p], [gla_m, gla_s, gla_p], tms, xs)],
        w_out, tn=512, n_cols=D_MODEL, cast_w=True)
    xn2 = norm_all(x1, norm_ffn[0], BF16)
    tail = FFN_CONV_W - 1
    bufs = jnp.swapaxes(state_ffn_conv[0], 0, 1).reshape(tail * nb, D_FF)
    hid_p, hid_s, p_ffn, s_ffn_t = ffn_gate(xn2[0], xn2[1], xn2[2], bufs, wts,
                                            tm=FFN_TM, tn=FFN_TN, seq_rows=seq, nb=nb)
    x2_s, x2_p = grouped_matmul(
        [dict(parts=[hid_s], tm=rows_s, res=x1[1]), dict(parts=[hid_p], tm=PROMPT_TM, res=x1[2])],
        wts["w_down"], tn=1024, n_cols=D_MODEL, tk=2048)
    y_p = rmsnorm(x2_p, final_norm, F32, NORM_TM)
    y_s = to_batch_major(rmsnorm(x2_s, final_norm, F32, NORM_TM))

    return (y_p.reshape(batch, seq, D_MODEL), y_s.reshape(nb, sseq, D_MODEL),
            p_conv[None], p_h.reshape(1, batch, D_RNN), p_s[None], p_ffn[None],
            jnp.swapaxes(s_conv_t, 0, 1)[None], s_h[None], s_s[None],
            jnp.swapaxes(s_ffn_t.reshape(tail, nb, D_FF), 0, 1)[None])
```

```python
import functools
import math

import jax
import jax.numpy as jnp
from jax import lax
from jax.experimental import pallas as pl
from jax.experimental.pallas import tpu as pltpu

F32 = jnp.float32
BF16 = jnp.bfloat16

D_MODEL = 4096
N_META = 16
D_RNN = 2048
D_GLA = 2048
RG_BLOCKS = 16
RG_BLOCK = 128
CONV_W = 4
RG_C = 8.0
GLA_HEADS = 16
GLA_DK = 128
GLA_RANK = 16
GLA_GATE_TAU = 16.0
D_FF = 3 * D_MODEL
FFN_CONV_W = 3
EPS = 1e-6
PROJ_COLS = 2 * D_RNN + 4 * D_GLA

LANES = 128
SUB = 8
VMEM_LIMIT = 60 * 1024 * 1024
GLA_SUB = 16
GLA_G = 4


def _params(sem):
    return pltpu.CompilerParams(dimension_semantics=sem, vmem_limit_bytes=VMEM_LIMIT)


def _gelu(x):
    c = math.sqrt(2.0 / math.pi)
    return 0.5 * x * (1.0 + jnp.tanh(c * (x + 0.044715 * (x * x * x))))


def _softplus(x):
    return jnp.maximum(x, 0.0) + jnp.log1p(jnp.exp(-jnp.abs(x)))


def _sigmoid(x):
    return 0.5 * (1.0 + jnp.tanh(0.5 * x))


def _rmsnorm_kernel(x_ref, g_ref, o_ref):
    x = x_ref[...]
    ms = jnp.mean(x * x, axis=-1, keepdims=True)
    o_ref[...] = (x * lax.rsqrt(ms + EPS) * g_ref[...]).astype(o_ref.dtype)


def rmsnorm(x, g, out_dtype, tm):
    m, d = x.shape
    return pl.pallas_call(
        _rmsnorm_kernel,
        out_shape=jax.ShapeDtypeStruct((m, d), out_dtype),
        grid=(m // tm,),
        in_specs=[pl.BlockSpec((tm, d), lambda i: (i, 0)),
                  pl.BlockSpec((1, d), lambda i: (0, 0))],
        out_specs=pl.BlockSpec((tm, d), lambda i: (i, 0)),
        compiler_params=_params(("parallel",)),
        name="rmsnorm",
    )(x, g.reshape(1, d))


def _mm_kernel(x_ref, w_ref, *rest, nk, has_res):
    if has_res:
        r_ref, o_ref = rest
    else:
        (o_ref,) = rest
    part = jnp.dot(x_ref[...], w_ref[...], preferred_element_type=F32)
    if nk == 1:
        o_ref[...] = part + r_ref[...] if has_res else part
        return
    k = pl.program_id(2)

    @pl.when(k == 0)
    def _():
        o_ref[...] = part + r_ref[...] if has_res else part

    @pl.when(k > 0)
    def _():
        o_ref[...] += part


def matmul(x, w, *, tm, tn, tk, n_cols=None, res=None):
    m, kdim = x.shape
    n = w.shape[1] if n_cols is None else n_cols
    nk = kdim // tk
    in_specs = [pl.BlockSpec((tm, tk), lambda j, i, k: (i, k)),
                pl.BlockSpec((tk, tn), lambda j, i, k: (k, j))]
    args = [x, w]
    if res is not None:
        in_specs.append(pl.BlockSpec((tm, tn), lambda j, i, k: (i, j)))
        args.append(res)
    return pl.pallas_call(
        functools.partial(_mm_kernel, nk=nk, has_res=res is not None),
        out_shape=jax.ShapeDtypeStruct((m, n), F32),
        grid=(n // tn, m // tm, nk),
        in_specs=in_specs,
        out_specs=pl.BlockSpec((tm, tn), lambda j, i, k: (i, j)),
        compiler_params=_params(("parallel", "parallel", "arbitrary")),
        name="matmul",
    )(*args)


def _group_steps(groups):
    steps, first = [], 0
    for rows, tm in groups:
        steps.append((first, rows // tm))
        first += rows // tm
    return steps, first


def _tile_index(i, first, count):
    return jnp.clip(i - first, 0, count - 1)


def _gmm_kernel(*refs, layout, steps, nk, cast_w):
    i = pl.program_id(1)
    pos = 0
    g_in = []
    for n_parts, has_res in layout:
        xs = refs[pos:pos + n_parts]
        pos += n_parts
        r = refs[pos] if has_res else None
        pos += int(has_res)
        g_in.append((xs, r))
    w_ref = refs[pos]
    outs = refs[pos + 1:pos + 1 + len(layout)]
    if cast_w:
        wsrc = refs[pos + 1 + len(layout)]

        @pl.when(i == 0)
        def _():
            wsrc[...] = w_ref[...].astype(BF16)
    else:
        wsrc = w_ref
    for (xs, r), o_ref, (first, count) in zip(g_in, outs, steps):
        @pl.when((i >= first) & (i < first + count))
        def _(xs=xs, r=r, o_ref=o_ref):
            acc, off = None, 0
            for x_ref in xs:
                kk = x_ref.shape[1]
                part = jnp.dot(x_ref[...], wsrc[off:off + kk, :], preferred_element_type=F32)
                acc = part if acc is None else acc + part
                off += kk
            if r is not None:
                first_val = acc + r[...]
            else:
                first_val = acc
            if nk == 1:
                o_ref[...] = first_val
            else:
                k = pl.program_id(2)

                @pl.when(k == 0)
                def _():
                    o_ref[...] = first_val

                @pl.when(k > 0)
                def _():
                    o_ref[...] += acc


def grouped_matmul(groups, w, *, tn, n_cols, tk=None, cast_w=False):
    kdim = sum(p.shape[1] for p in groups[0]["parts"])
    nk = 1 if tk is None else kdim // tk
    assert not (cast_w and nk > 1)
    steps, ni = _group_steps([(g["parts"][0].shape[0], g["tm"]) for g in groups])
    in_specs, args, layout = [], [], []
    for g, (first, count) in zip(groups, steps):
        assert nk == 1 or len(g["parts"]) == 1
        last_k = nk - 1

        def kidx(i, k, first=first, count=count):
            if nk == 1:
                return 0
            return jnp.where(i < first, 0, jnp.where(i < first + count, k, last_k))
        once = dict(pipeline_mode=pl.Buffered(1)) if (count == 1 and nk == 1) else {}
        for p in g["parts"]:
            kp = p.shape[1] if nk == 1 else tk
            in_specs.append(pl.BlockSpec(
                (g["tm"], kp),
                lambda j, i, k, first=first, count=count, kidx=kidx:
                    (_tile_index(i, first, count), kidx(i, k)), **once))
            args.append(p)
        if g.get("res") is not None:
            in_specs.append(pl.BlockSpec(
                (g["tm"], tn),
                lambda j, i, k, first=first, count=count: (_tile_index(i, first, count), j)))
            args.append(g["res"])
        layout.append((len(g["parts"]), g.get("res") is not None))
    kblk = kdim if nk == 1 else tk
    scratch = [pltpu.VMEM((kblk, tn), BF16)] if cast_w else []
    if w.ndim == 3:
        in_specs.append(pl.BlockSpec((None, kblk, tn), lambda j, i, k: (0, k, j)))
    else:
        in_specs.append(pl.BlockSpec((kblk, tn), lambda j, i, k: (k, j)))
    args.append(w)
    out_specs = [pl.BlockSpec((g["tm"], tn),
                              lambda j, i, k, first=first, count=count:
                                  (_tile_index(i, first, count), j))
                 for g, (first, count) in zip(groups, steps)]
    return pl.pallas_call(
        functools.partial(_gmm_kernel, layout=tuple(layout), steps=tuple(steps), nk=nk,
                          cast_w=cast_w),
        out_shape=[jax.ShapeDtypeStruct((g["parts"][0].shape[0], n_cols), F32) for g in groups],
        grid=(n_cols // tn, ni, nk),
        in_specs=in_specs,
        out_specs=out_specs,
        scratch_shapes=scratch,
        compiler_params=_params(("parallel", "arbitrary", "arbitrary")),
        name="grouped_matmul",
    )(*args)


def _rg_gates(xc, n, wax_ref, ba_ref, bx_ref, sp_row):
    cs = slice(n * RG_BLOCK, (n + 1) * RG_BLOCK)
    pre = jnp.dot(xc.astype(BF16), wax_ref[n], preferred_element_type=F32)
    r = _sigmoid(pre[:, :RG_BLOCK] + ba_ref[:, cs])
    i = _sigmoid(pre[:, RG_BLOCK:] + bx_ref[:, cs])
    log_a = -RG_C * r * sp_row[:, cs]
    a = jnp.exp(log_a)
    th = jnp.tanh(log_a)
    p = -2.0 * th
    mult = jnp.where(p > 0.0, p * lax.rsqrt(p * (1.0 - th)), 0.0)
    return a, mult, i


def _rg_long_kernel(xr_ref, yr_ref, cb_ref, h0_ref, cw_ref, cbias_ref, wax_ref, ba_ref, bx_ref,
                    lam_ref, gn_ref, out_ref, nconv_ref, hlast_ref,
                    ext_scr, a_scr, u_scr, h_scr, *, tt, reset_first):
    t = pl.program_id(1)
    nt = pl.num_programs(1)
    pad = SUB
    tail = CONV_W - 1

    @pl.when(t == 0)
    def _():
        ext_scr[pad - tail:pad, :] = cb_ref[0]
        h_scr[...] = h0_ref[0]

    ext_scr[pad:pad + tt, :] = xr_ref[...]
    sp_row = _softplus(-lam_ref[...])
    row = lax.broadcasted_iota(jnp.int32, (tt, RG_BLOCK), 0)
    first_row = jnp.where(t == 0, 0, -1)
    for n in range(RG_BLOCKS):
        cs = slice(n * RG_BLOCK, (n + 1) * RG_BLOCK)
        xc = cbias_ref[:, cs]
        for i in range(CONV_W):
            lo = pad - tail + i
            xc = xc + ext_scr[lo:lo + tt, cs] * cw_ref[i:i + 1, cs]
        a, mult, gate_i = _rg_gates(xc, n, wax_ref, ba_ref, bx_ref, sp_row)
        if reset_first:
            mult = jnp.where(row == first_row, 1.0, mult)
        a_scr[:, cs] = a
        u_scr[:, cs] = mult * gate_i * xc

    row8 = lax.broadcasted_iota(jnp.int32, (SUB, 512), 0)
    ncol = D_RNN // 512

    def scan_body(j, hs):
        r0 = pl.multiple_of(j * SUB, SUB)
        new = []
        for c in range(ncol):
            cs = slice(c * 512, (c + 1) * 512)
            a = a_scr[pl.ds(r0, SUB), cs]
            u = u_scr[pl.ds(r0, SUB), cs]
            for s in (1, 2, 4):
                a_sh = jnp.where(row8 >= s, pltpu.roll(a, s, 0), 1.0)
                u_sh = jnp.where(row8 >= s, pltpu.roll(u, s, 0), 0.0)
                u = a * u_sh + u
                a = a * a_sh
            h = a * hs[c] + u
            u_scr[pl.ds(r0, SUB), cs] = h
            new.append(h[SUB - 1:SUB, :])
        return tuple(new)

    h_in = tuple(h_scr[:, c * 512:(c + 1) * 512] for c in range(ncol))
    h_fin = lax.fori_loop(0, tt // SUB, scan_body, h_in)
    for c in range(ncol):
        h_scr[:, c * 512:(c + 1) * 512] = h_fin[c]

    y = u_scr[...] * _gelu(yr_ref[...])
    ms = jnp.mean(y * y, axis=-1, keepdims=True)
    out_ref[...] = (y * lax.rsqrt(ms + EPS) * gn_ref[...]).astype(out_ref.dtype)
    ext_scr[pad - tail:pad, :] = ext_scr[pad + tt - tail:pad + tt, :]

    @pl.when(t == nt - 1)
    def _():
        nconv_ref[0] = ext_scr[pad + tt - tail:pad + tt, :]
        hlast_ref[0] = h_scr[...]


def rg_long(proj, conv_buf, h0, wts, *, batch, seq, tt, reset_first):
    nt = seq // tt
    bsel = (lambda b: b) if conv_buf.shape[0] == batch else (lambda b: 0)
    vec = lambda r: pl.BlockSpec((r, D_RNN), lambda b, t: (0, 0))
    return pl.pallas_call(
        functools.partial(_rg_long_kernel, tt=tt, reset_first=reset_first),
        out_shape=(jax.ShapeDtypeStruct((batch * seq, D_RNN), BF16),
                   jax.ShapeDtypeStruct((batch, CONV_W - 1, D_RNN), F32),
                   jax.ShapeDtypeStruct((batch, 1, D_RNN), F32)),
        grid=(batch, nt),
        in_specs=[pl.BlockSpec((tt, D_RNN), lambda b, t: (b * nt + t, 0)),
                  pl.BlockSpec((tt, D_RNN), lambda b, t: (b * nt + t, 1)),
                  pl.BlockSpec((1, CONV_W - 1, D_RNN), lambda b, t: (bsel(b), 0, 0)),
                  pl.BlockSpec((1, 1, D_RNN), lambda b, t: (bsel(b), 0, 0)),
                  vec(CONV_W), vec(1),
                  pl.BlockSpec((RG_BLOCKS, RG_BLOCK, 2 * RG_BLOCK), lambda b, t: (0, 0, 0)),
                  vec(1), vec(1), vec(1), vec(1)],
        out_specs=(pl.BlockSpec((tt, D_RNN), lambda b, t: (b * nt + t, 0)),
                   pl.BlockSpec((1, CONV_W - 1, D_RNN), lambda b, t: (b, 0, 0)),
                   pl.BlockSpec((1, 1, D_RNN), lambda b, t: (b, 0, 0))),
        scratch_shapes=[pltpu.VMEM((tt + SUB, D_RNN), F32),
                        pltpu.VMEM((tt, D_RNN), F32),
                        pltpu.VMEM((tt, D_RNN), F32),
                        pltpu.VMEM((1, D_RNN), F32)],
        compiler_params=_params(("parallel", "arbitrary")),
        name="rg_long",
    )(proj, proj, conv_buf, h0, wts["rg_conv_w"], wts["rg_conv_b"], wts["rg_wax"],
      wts["rg_ba"], wts["rg_bx"], wts["rg_lambda"], wts["rg_out_norm"])


def _rg_short_kernel(xr_ref, yr_ref, cb_ref, h0_ref, cw_ref, cbias_ref, wax_ref, ba_ref, bx_ref,
                     lam_ref, gn_ref, out_ref, nconv_ref, hlast_ref, y_scr, *, nb, seq):
    sp_row = _softplus(-lam_ref[...])
    tail = CONV_W - 1
    for n in range(RG_BLOCKS):
        cs = slice(n * RG_BLOCK, (n + 1) * RG_BLOCK)
        ext = [cb_ref[i, :, cs] for i in range(tail)]
        ext += [xr_ref[t * nb:(t + 1) * nb, cs] for t in range(seq)]
        h = h0_ref[:, cs]
        for t in range(seq):
            xc = cbias_ref[:, cs]
            for i in range(CONV_W):
                xc = xc + ext[t + i] * cw_ref[i:i + 1, cs]
            a, mult, gate_i = _rg_gates(xc, n, wax_ref, ba_ref, bx_ref, sp_row)
            h = a * h + mult * gate_i * xc
            y_scr[t, :, cs] = h * _gelu(yr_ref[t * nb:(t + 1) * nb, cs])
        hlast_ref[:, cs] = h
        for i in range(tail):
            nconv_ref[i, :, cs] = ext[seq + i]
    for t in range(seq):
        y = y_scr[t]
        ms = jnp.mean(y * y, axis=-1, keepdims=True)
        out_ref[t * nb:(t + 1) * nb, :] = (y * lax.rsqrt(ms + EPS) * gn_ref[...]).astype(out_ref.dtype)


def rg_short(proj, conv_buf_t, h0, wts, *, nb, seq):
    rows = nb * seq
    vec = lambda r: pl.BlockSpec((r, D_RNN), lambda i: (0, 0))
    return pl.pallas_call(
        functools.partial(_rg_short_kernel, nb=nb, seq=seq),
        out_shape=(jax.ShapeDtypeStruct((rows, D_RNN), BF16),
                   jax.ShapeDtypeStruct((CONV_W - 1, nb, D_RNN), F32),
                   jax.ShapeDtypeStruct((nb, D_RNN), F32)),
        grid=(1,),
        in_specs=[pl.BlockSpec((rows, D_RNN), lambda i: (0, 0)),
                  pl.BlockSpec((rows, D_RNN), lambda i: (0, 1)),
                  pl.BlockSpec((CONV_W - 1, nb, D_RNN), lambda i: (0, 0, 0)),
                  pl.BlockSpec((nb, D_RNN), lambda i: (0, 0)),
                  vec(CONV_W), vec(1),
                  pl.BlockSpec((RG_BLOCKS, RG_BLOCK, 2 * RG_BLOCK), lambda i: (0, 0, 0)),
                  vec(1), vec(1), vec(1), vec(1)],
        out_specs=(pl.BlockSpec((rows, D_RNN), lambda i: (0, 0)),
                   pl.BlockSpec((CONV_W - 1, nb, D_RNN), lambda i: (0, 0, 0)),
                   pl.BlockSpec((nb, D_RNN), lambda i: (0, 0))),
        scratch_shapes=[pltpu.VMEM((seq, nb, D_RNN), F32)],
        compiler_params=_params(("arbitrary",)),
        name="rg_short",
    )(proj, proj, conv_buf_t, h0, wts["rg_conv_w"], wts["rg_conv_b"], wts["rg_wax"],
      wts["rg_ba"], wts["rg_bx"], wts["rg_lambda"], wts["rg_out_norm"])


def _gla_consts():
    er = lax.broadcasted_iota(jnp.int32, (LANES, LANES), 0)
    ec = lax.broadcasted_iota(jnp.int32, (LANES, LANES), 1)
    eye = jnp.where(er == ec, 1.0, 0.0).astype(BF16)
    ones = jnp.ones((LANES, LANES), BF16)
    row = lax.broadcasted_iota(jnp.int32, (GLA_SUB, LANES), 0)
    row8 = lax.broadcasted_iota(jnp.int32, (SUB, LANES), 0)
    pr = lax.broadcasted_iota(jnp.int32, (2 * GLA_SUB, LANES), 0)
    pick = jnp.where((pr == GLA_SUB) | (pr == GLA_SUB + 1), 1.0, 0.0).astype(BF16)
    return eye, ones, row, row8, pick


def _gla_local(units, consts):
    eye, ones, row, row8, pick = consts
    assert len(units) == GLA_G

    def scan8(x):
        for s in (1, 2, 4):
            x = x + jnp.where(row8 >= s, pltpu.roll(x, s, 0), 0.0)
        return x

    halves = (slice(0, SUB), slice(SUB, GLA_SUB))
    nt = (((1,), (1,)), ((), ()))
    qes, xs, qas, kbs, cum_all = [], [], [], [], []
    for q, k, v, g, j_range in units:
        cum_lo = scan8(g[halves[0]])
        edge = cum_lo[SUB - 1:SUB, :]
        cum_hi = scan8(g[halves[1]]) + edge
        cums = (cum_lo, cum_hi)
        cum_all.append(cums)
        last = cum_hi[SUB - 1:SUB, :]
        qes.append(jnp.concatenate([q[h] * jnp.exp(c) for h, c in zip(halves, cums)],
                                   axis=0).astype(BF16))
        qas.append(q[halves[1]] * jnp.exp(cum_hi - edge))
        kbs.append(k[halves[0]] * jnp.exp(edge - cum_lo))
        ke = jnp.concatenate([k[h] * jnp.exp(last - c) for h, c in zip(halves, cums)], axis=0)
        el = jnp.exp(last)
        e_hi = el.astype(BF16).astype(F32)
        e_lo = el - e_hi
        extra = jnp.where(row == 0, e_hi, jnp.where(row == 1, e_lo, 0.0))
        xs.append(jnp.concatenate([ke, extra], axis=0).astype(BF16))
    crossing = [u[4][0] < SUB <= u[4][-1] for u in units]
    sc = None
    if any(crossing):
        qa = jnp.concatenate(qas, axis=0).astype(BF16)
        kb = jnp.concatenate(kbs, axis=0).astype(BF16)
        sc = lax.dot_general(qa, kb, nt, preferred_element_type=F32)
    xt = lax.dot_general(eye, jnp.concatenate(xs, axis=0), nt, preferred_element_type=F32)
    o_halves = []
    for (q, k, v, g, j_range), cums in zip(units, cum_all):
        ps = []
        for j in j_range:
            h, c, jj = halves[j // SUB], cums[j // SUB], j % SUB
            w = jnp.exp(jnp.where(row8 >= jj, c - c[jj:jj + 1, :], -jnp.inf))
            ps.append(q[h] * w * k[j:j + 1, :])
        att = jnp.dot(jnp.concatenate(ps, axis=0).astype(BF16), ones, preferred_element_type=F32)
        o_half = [None, None]
        for idx, j in enumerate(j_range):
            term = att[idx * SUB:(idx + 1) * SUB, :] * v[j:j + 1, :]
            o_half[j // SUB] = term if o_half[j // SUB] is None else o_half[j // SUB] + term
        o_halves.append(o_half)
    cross = None
    if sc is not None:
        n = GLA_G * SUB
        ur = lax.broadcasted_iota(jnp.int32, (n, n), 0) // SUB
        uc = lax.broadcasted_iota(jnp.int32, (n, n), 1) // SUB
        sc = jnp.where(ur == uc, sc, 0.0).astype(BF16)
        v_lo = jnp.concatenate([u[2][halves[0]] for u in units], axis=0).astype(BF16)
        cross = jnp.dot(sc, v_lo, preferred_element_type=F32)
    xt = xt.astype(BF16)
    zblk = jnp.zeros((2 * GLA_SUB, LANES), BF16)
    wrows = []
    for u, (_, _, v, _, _) in enumerate(units):
        vpad = jnp.concatenate([v, jnp.zeros_like(v)], axis=0).astype(BF16)
        wrows.append(jnp.concatenate([zblk] * (2 * u) + [vpad, pick]
                                     + [zblk] * (2 * (GLA_G - 1 - u)), axis=1))
    kd = jnp.dot(xt, jnp.concatenate(wrows, axis=0), preferred_element_type=F32)
    out = []
    for u, o_half in enumerate(o_halves):
        if cross is not None and crossing[u]:
            o_half[1] = o_half[1] + cross[u * SUB:(u + 1) * SUB, :]
        o = jnp.concatenate([jnp.zeros((SUB, LANES), F32) if t is None else t for t in o_half],
                            axis=0)
        kv = kd[:, (2 * u) * LANES:(2 * u + 1) * LANES]
        dec = kd[:, (2 * u + 1) * LANES:(2 * u + 2) * LANES]
        out.append((qes[u], o, kv, dec))
    return out


def _gla_log_alpha(z_ref, wa2_ref, ba2_ref):
    zz = jnp.dot(z_ref[...].astype(BF16), wa2_ref[...], preferred_element_type=F32) + ba2_ref[...]
    return (jnp.minimum(zz, 0.0) - jnp.log1p(jnp.exp(-jnp.abs(zz)))) * (1.0 / GLA_GATE_TAU)


def _gla_finish(o, go, gain):
    ms = jnp.mean(o * o, axis=-1, keepdims=True)
    on = o * lax.rsqrt(ms + EPS) * gain
    return on * (go * _sigmoid(go))


def _gla_long_kernel(q_ref, k_ref, v_ref, go_ref, z_ref, wa2_ref, ba2_ref, gain_ref, s0_ref,
                     *rest, tt, n_casts):
    cast_in = rest[:n_casts]
    out_ref, sout_ref = rest[n_casts:n_casts + 2]
    cast_out = rest[n_casts + 2:2 * n_casts + 2]
    s_scr, gl_scr, o_scr, qe_scr, kv_scr, dec_scr, sall_scr = rest[2 * n_casts + 2:]
    t = pl.program_id(2)
    nt = pl.num_programs(2)
    nsub = tt // GLA_SUB
    consts = _gla_consts()
    scale = GLA_DK ** -0.5

    for src, dst in zip(cast_in, cast_out):
        dst[...] = src[...].astype(BF16)

    @pl.when(t == 0)
    def _():
        s_scr[...] = s0_ref[0]

    gl_scr[...] = _gla_log_alpha(z_ref, wa2_ref, ba2_ref)

    def local_body(i, carry):
        rows = pl.ds(pl.multiple_of(i * GLA_SUB, GLA_SUB), GLA_SUB)
        heads = [slice(h * LANES, (h + 1) * LANES) for h in range(GLA_G)]
        units = [(q_ref[rows, cs] * scale, k_ref[rows, cs], v_ref[rows, cs], gl_scr[rows, cs],
                  range(GLA_SUB)) for cs in heads]
        for h, (qe, o, kv, dec) in enumerate(_gla_local(units, consts)):
            cs = heads[h]
            qe_scr[rows, cs] = qe
            o_scr[rows, cs] = o
            kv_scr[i, h] = kv
            dec_scr[i, h] = dec
        return carry

    lax.fori_loop(0, nsub, local_body, 0, unroll=min(4, nsub))

    for h in range(GLA_G):
        def state_body(i, s, h=h):
            sall_scr[i, h] = s.astype(BF16)
            return dec_scr[i, h] * s + kv_scr[i, h]
        s_scr[h] = lax.fori_loop(0, nsub, state_body, s_scr[h])

    def inter_body(i, carry):
        rows = pl.ds(pl.multiple_of(i * GLA_SUB, GLA_SUB), GLA_SUB)
        for h in range(GLA_G):
            cs = slice(h * LANES, (h + 1) * LANES)
            o_scr[rows, cs] += jnp.dot(qe_scr[rows, cs], sall_scr[i, h], preferred_element_type=F32)
        return carry

    lax.fori_loop(0, nsub, inter_body, 0, unroll=min(8, nsub))
    for h in range(GLA_G):
        cs = slice(h * LANES, (h + 1) * LANES)
        out_ref[:, cs] = _gla_finish(o_scr[:, cs], go_ref[:, cs], gain_ref[:, cs]).astype(out_ref.dtype)

    @pl.when(t == nt - 1)
    def _():
        sout_ref[0] = s_scr[...]


def gla_long(proj, z, s0, wts, *, batch, seq, tt, casts=()):
    nt = seq // tt
    gw = GLA_G * LANES
    base = 2 * D_RNN // gw
    per = D_GLA // gw
    nhg = GLA_HEADS // GLA_G
    bsel = (lambda b: b) if s0.shape[0] == batch else (lambda b: 0)
    col = lambda which: pl.BlockSpec((tt, gw), lambda b, h, t: (b * nt + t, base + which * per + h))
    n_steps = batch * nhg * nt
    cast_specs = []
    for a in casts:
        rblocks = math.gcd(n_steps, a.shape[0] // 16)
        cblocks = n_steps // rblocks
        blk = (a.shape[0] // rblocks, a.shape[1] // cblocks)
        assert blk[0] * rblocks == a.shape[0] and blk[1] * cblocks == a.shape[1]
        assert blk[0] % 16 == 0 and blk[1] % LANES == 0
        cast_specs.append(pl.BlockSpec(
            blk, lambda b, h, t, cblocks=cblocks:
                (((b * nhg + h) * nt + t) // cblocks, ((b * nhg + h) * nt + t) % cblocks)))
    return pl.pallas_call(
        functools.partial(_gla_long_kernel, tt=tt, n_casts=len(casts)),
        out_shape=(jax.ShapeDtypeStruct((batch * seq, D_GLA), BF16),
                   jax.ShapeDtypeStruct((batch, GLA_HEADS, GLA_DK, GLA_DK), F32),
                   *[jax.ShapeDtypeStruct(a.shape, BF16) for a in casts]),
        grid=(batch, nhg, nt),
        in_specs=[col(0), col(1), col(2), col(3),
                  pl.BlockSpec((tt, LANES), lambda b, h, t: (b * nt + t, 0)),
                  pl.BlockSpec((LANES, gw), lambda b, h, t: (0, h)),
                  pl.BlockSpec((1, gw), lambda b, h, t: (0, h)),
                  pl.BlockSpec((1, gw), lambda b, h, t: (0, h)),
                  pl.BlockSpec((1, GLA_G, GLA_DK, GLA_DK), lambda b, h, t: (bsel(b), h, 0, 0)),
                  *cast_specs],
        out_specs=(pl.BlockSpec((tt, gw), lambda b, h, t: (b * nt + t, h)),
                   pl.BlockSpec((1, GLA_G, GLA_DK, GLA_DK), lambda b, h, t: (b, h, 0, 0)),
                   *cast_specs),
        scratch_shapes=[pltpu.VMEM((GLA_G, GLA_DK, GLA_DK), F32),
                        pltpu.VMEM((tt, gw), F32),
                        pltpu.VMEM((tt, gw), F32),
                        pltpu.VMEM((tt, gw), BF16),
                        pltpu.VMEM((tt // GLA_SUB, GLA_G, GLA_DK, GLA_DK), F32),
                        pltpu.VMEM((tt // GLA_SUB, GLA_G, GLA_DK, GLA_DK), F32),
                        pltpu.VMEM((tt // GLA_SUB, GLA_G, GLA_DK, GLA_DK), BF16)],
        compiler_params=_params(("parallel", "parallel", "arbitrary")),
        name="gla_long",
    )(proj, proj, proj, proj, z, wts["gla_wa2"], wts["gla_ba2"], wts["gla_head_norm"], s0, *casts)


def _gla_short_kernel(q_ref, k_ref, v_ref, go_ref, z_ref, wa2_ref, ba2_ref, gain_ref, s0_ref,
                      out_ref, sout_ref, *, seq):
    consts = _gla_consts()
    row = consts[2]
    scale = GLA_DK ** -0.5
    gl = _gla_log_alpha(z_ref, wa2_ref, ba2_ref)
    nseq = GLA_SUB // seq
    assert nseq == GLA_G
    local = {}
    for h in range(GLA_G):
        cs = slice(h * LANES, (h + 1) * LANES)
        q = q_ref[:, cs] * scale
        k = k_ref[:, cs]
        v = v_ref[:, cs]
        g = gl[:, cs]
        units = []
        for b in range(nseq):
            mine = (row >= b * seq) & (row < (b + 1) * seq)
            zero = lambda a, mine=mine: jnp.where(mine, a, 0.0)
            units.append((zero(q), zero(k), zero(v), zero(g), range(b * seq, (b + 1) * seq)))
        for b, res in enumerate(_gla_local(units, consts)):
            local[b, h] = res
    for h in range(GLA_G):
        cs = slice(h * LANES, (h + 1) * LANES)
        o = jnp.zeros((GLA_SUB, LANES), F32)
        for b in range(nseq):
            qe, o_b, kv, dec = local[b, h]
            s = s0_ref[b, h]
            o = o + o_b + jnp.dot(qe, s.astype(BF16), preferred_element_type=F32)
            sout_ref[b, h] = dec * s + kv
        out_ref[:, cs] = _gla_finish(o, go_ref[:, cs], gain_ref[:, cs]).astype(out_ref.dtype)


def gla_short(proj, z, s0, wts, *, nb, seq):
    gw = GLA_G * LANES
    per = D_GLA // gw
    bpb = GLA_SUB // seq
    col = lambda which: pl.BlockSpec((GLA_SUB, gw), lambda i, h: (i, which * per + h))
    return pl.pallas_call(
        functools.partial(_gla_short_kernel, seq=seq),
        out_shape=(jax.ShapeDtypeStruct((nb * seq, D_GLA), BF16),
                   jax.ShapeDtypeStruct((nb, GLA_HEADS, GLA_DK, GLA_DK), F32)),
        grid=(nb // bpb, GLA_HEADS // GLA_G),
        in_specs=[col(0), col(1), col(2), col(3),
                  pl.BlockSpec((GLA_SUB, LANES), lambda i, h: (i, 0)),
                  pl.BlockSpec((LANES, gw), lambda i, h: (0, h)),
                  pl.BlockSpec((1, gw), lambda i, h: (0, h)),
                  pl.BlockSpec((1, gw), lambda i, h: (0, h)),
                  pl.BlockSpec((bpb, GLA_G, GLA_DK, GLA_DK), lambda i, h: (i, h, 0, 0))],
        out_specs=(pl.BlockSpec((GLA_SUB, gw), lambda i, h: (i, h)),
                   pl.BlockSpec((bpb, GLA_G, GLA_DK, GLA_DK), lambda i, h: (i, h, 0, 0))),
        compiler_params=_params(("parallel", "parallel")),
        name="gla_short",
    )(proj, proj, proj, proj, z, wts["gla_wa2"], wts["gla_ba2"], wts["gla_head_norm"], s0)


def _ffn_kernel(xm_ref, xs_ref, xp_ref, wg_ref, wu_ref, cw_ref, cb_ref, bufs_ref,
                hp_ref, hs_ref, nbufp_ref, nbufs_ref, extp_scr, exts_scr, mhist_scr,
                *, steps, tiles_per_seq, nb):
    i = pl.program_id(1)
    (m_first, _), (s_first, _), (p_first, p_count) = steps
    tail = FFN_CONV_W - 1
    pad = SUB
    tm = xp_ref.shape[0]
    rows_s = xs_ref.shape[0]
    hist_s = tail * nb

    def conv(ext_ref, start, shift, rows):
        gc = cb_ref[...]
        for c in range(FFN_CONV_W):
            lo = start - (tail - c) * shift
            gc = gc + ext_ref[lo:lo + rows, :] * cw_ref[c:c + 1, :]
        return gc

    @pl.when(i == m_first)
    def _():
        g = jnp.dot(xm_ref[...], wg_ref[...], preferred_element_type=F32)
        mhist_scr[...] = g[g.shape[0] - tail:, :]

    @pl.when((i >= p_first) & (i < p_first + p_count))
    def _():
        @pl.when((i - p_first) % tiles_per_seq == 0)
        def _():
            extp_scr[pad - tail:pad, :] = mhist_scr[...]

        x = xp_ref[...]
        extp_scr[pad:pad + tm, :] = jnp.dot(x, wg_ref[...], preferred_element_type=F32)
        up = jnp.dot(x, wu_ref[...], preferred_element_type=F32)
        hp_ref[...] = (_gelu(conv(extp_scr, pad, 1, tm)) * up).astype(hp_ref.dtype)
        last = extp_scr[pad + tm - tail:pad + tm, :]
        extp_scr[pad - tail:pad, :] = last
        nbufp_ref[0] = last

    @pl.when(i == s_first)
    def _():
        x = xs_ref[...]
        exts_scr[0:hist_s, :] = bufs_ref[...]
        exts_scr[hist_s:hist_s + rows_s, :] = jnp.dot(x, wg_ref[...], preferred_element_type=F32)
        up = jnp.dot(x, wu_ref[...], preferred_element_type=F32)
        hs_ref[...] = (_gelu(conv(exts_scr, hist_s, nb, rows_s)) * up).astype(hs_ref.dtype)
        nbufs_ref[...] = exts_scr[rows_s:rows_s + hist_s, :]


def ffn_gate(xm, xs, xp, bufs, wts, *, tm, tn, seq_rows, nb):
    rows_p, rows_s = xp.shape[0], xs.shape[0]
    tail = FFN_CONV_W - 1
    steps, ni = _group_steps([(xm.shape[0], xm.shape[0]), (rows_s, rows_s), (rows_p, tm)])
    (_, _), (_, _), (p_first, p_count) = steps
    tps = seq_rows // tm
    ptile = lambda i: _tile_index(i, p_first, p_count)
    const = lambda shape: pl.BlockSpec(shape, lambda j, i: (0, 0), pipeline_mode=pl.Buffered(1))
    colblk = lambda r: pl.BlockSpec((r, tn), lambda j, i: (0, j))
    return pl.pallas_call(
        functools.partial(_ffn_kernel, steps=tuple(steps), tiles_per_seq=tps, nb=nb),
        out_shape=(jax.ShapeDtypeStruct((rows_p, D_FF), BF16),
                   jax.ShapeDtypeStruct((rows_s, D_FF), BF16),
                   jax.ShapeDtypeStruct((rows_p // seq_rows, tail, D_FF), F32),
                   jax.ShapeDtypeStruct((tail * nb, D_FF), F32)),
        grid=(D_FF // tn, ni),
        in_specs=[const(xm.shape),
                  const(xs.shape),
                  pl.BlockSpec((tm, D_MODEL), lambda j, i: (ptile(i), 0)),
                  colblk(D_MODEL), colblk(D_MODEL), colblk(FFN_CONV_W), colblk(1),
                  colblk(tail * nb)],
        out_specs=(pl.BlockSpec((tm, tn), lambda j, i: (ptile(i), j)),
                   colblk(rows_s),
                   pl.BlockSpec((1, tail, tn), lambda j, i: (ptile(i) // tps, 0, j)),
                   colblk(tail * nb)),
        scratch_shapes=[pltpu.VMEM((tm + SUB, tn), F32),
                        pltpu.VMEM((rows_s + tail * nb, tn), F32),
                        pltpu.VMEM((tail, tn), F32)],
        compiler_params=_params(("parallel", "arbitrary")),
        name="ffn_gate",
    )(xm, xs, xp, wts["w_gate"], wts["w_up"], wts["ffn_conv_w"], wts["ffn_conv_b"], bufs)


PROMPT_TM = 1024
DOWN_TM = 512
DOWN_TN = 512
FFN_TM = 512
FFN_TN = 512
SEQ_TT = 256
NORM_TM = 256


def kernel(x_prompt, x_sample, state_rglru_conv, state_rglru_h, state_gla_S, state_ffn_conv,
           meta_tokens, norm_mix, w_in, rg_conv_w, rg_conv_b, rg_wa, rg_ba, rg_wx, rg_bx, rg_lambda,
           rg_out_norm, gla_wa2, gla_ba2, gla_head_norm, w_out, norm_ffn, w_gate, w_up,
           ffn_conv_w, ffn_conv_b, w_down, final_norm):
    batch, seq, _ = x_prompt.shape
    nb, sseq, _ = x_sample.shape
    row = lambda a: a.reshape(1, -1)
    w_in_b = w_in[0].astype(BF16)
    wts = {
        "w_in": w_in_b,
        "w_z": jnp.pad(w_in_b[:, PROJ_COLS:], ((0, 0), (0, LANES - GLA_RANK))),
        "rg_conv_w": rg_conv_w[0], "rg_conv_b": row(rg_conv_b[0]),
        "rg_wax": jnp.concatenate([rg_wa[0], rg_wx[0]], axis=-1).astype(BF16),
        "rg_ba": row(rg_ba[0]), "rg_bx": row(rg_bx[0]), "rg_lambda": row(rg_lambda[0]),
        "rg_out_norm": row(rg_out_norm[0]),
        "gla_wa2": jnp.pad(gla_wa2[0].astype(BF16), ((0, LANES - GLA_RANK), (0, 0))),
        "gla_ba2": row(gla_ba2[0]), "gla_head_norm": row(gla_head_norm[0]),
        "ffn_conv_w": ffn_conv_w[0], "ffn_conv_b": row(ffn_conv_b[0]),
    }

    def to_batch_major(a):
        return jnp.swapaxes(a.reshape(sseq, nb, -1), 0, 1).reshape(nb * sseq, -1)

    def to_time_major(a):
        return jnp.swapaxes(a.reshape(nb, sseq, -1), 0, 1).reshape(nb * sseq, -1)

    rows_p, rows_s = batch * seq, nb * sseq
    xs = [meta_tokens, to_time_major(x_sample), x_prompt.reshape(rows_p, D_MODEL)]
    tms = [N_META, rows_s, PROMPT_TM]
    norm_tms = [N_META, NORM_TM, NORM_TM]

    def norm_all(arrs, g, dtype):
        return [rmsnorm(a, g, dtype, t) for a, t in zip(arrs, norm_tms)]

    xn = norm_all(xs, norm_mix[0], BF16)
    proj_m, proj_s, proj_p = grouped_matmul(
        [dict(parts=[a], tm=t) for a, t in zip(xn, tms)], wts["w_in"], tn=1024, n_cols=PROJ_COLS)
    z_m, z_s, z_p = [matmul(a, wts["w_z"], tm=t, tn=LANES, tk=D_MODEL) for a, t in zip(xn, tms)]

    zeros = lambda *s: jnp.zeros(s, F32)
    rnn_m, m_conv, m_h = rg_long(proj_m, zeros(1, CONV_W - 1, D_RNN), zeros(1, 1, D_RNN), wts,
                                 batch=1, seq=N_META, tt=N_META, reset_first=True)
    gla_m, m_s = gla_long(proj_m, z_m, zeros(1, GLA_HEADS, GLA_DK, GLA_DK), wts, batch=1,
                          seq=N_META, tt=N_META)
    rnn_p, p_conv, p_h = rg_long(proj_p, m_conv, m_h, wts, batch=batch, seq=seq, tt=SEQ_TT,
                                 reset_first=False)
    gla_p, p_s, wts["w_gate"], wts["w_up"], wts["w_down"] = gla_long(
        proj_p, z_p, m_s, wts, batch=batch, seq=seq, tt=SEQ_TT,
        casts=(w_gate.reshape(D_MODEL, D_FF), w_up.reshape(D_MODEL, D_FF),
               w_down.reshape(D_FF, D_MODEL)))
    rnn_s, s_conv_t, s_h = rg_short(proj_s, jnp.swapaxes(state_rglru_conv[0], 0, 1),
                                    state_rglru_h[0], wts, nb=nb, seq=sseq)
    gla_sb, s_s = gla_short(to_batch_major(proj_s[:, 2 * D_RNN:]), to_batch_major(z_s),
                            state_gla_S[0], wts, nb=nb, seq=sseq)
    gla_s = to_time_major(gla_sb)

    x1 = grouped_matmul(
        [dict(parts=[r, g], tm=t, res=x)
         for r, g, t, x in zip([rnn_m, rnn_s, rnn_p], [gla_m, gla_s, gla_p], tms, xs)],
        w_out, tn=512, n_cols=D_MODEL, cast_w=True)
    xn2 = norm_all(x1, norm_ffn[0], BF16)
    tail = FFN_CONV_W - 1
    bufs = jnp.swapaxes(state_ffn_conv[0], 0, 1).reshape(tail * nb, D_FF)
    hid_p, hid_s, p_ffn, s_ffn_t = ffn_gate(xn2[0], xn2[1], xn2[2], bufs, wts,
                                            tm=FFN_TM, tn=FFN_TN, seq_rows=seq, nb=nb)
    (x2_s,) = grouped_matmul([dict(parts=[hid_s], tm=rows_s, res=x1[1])], wts["w_down"],
                             tn=DOWN_TN, n_cols=D_MODEL)
    (x2_p,) = grouped_matmul([dict(parts=[hid_p], tm=DOWN_TM, res=x1[2])], wts["w_down"],
                             tn=DOWN_TN, n_cols=D_MODEL)
    y_p = rmsnorm(x2_p, final_norm, F32, NORM_TM)
    y_s = to_batch_major(rmsnorm(x2_s, final_norm, F32, NORM_TM))

    return (y_p.reshape(batch, seq, D_MODEL), y_s.reshape(nb, sseq, D_MODEL),
            p_conv[None], p_h.reshape(1, batch, D_RNN), p_s[None], p_ffn[None],
            jnp.swapaxes(s_conv_t, 0, 1)[None], s_h[None], s_s[None],
            jnp.swapaxes(s_ffn_t.reshape(tail, nb, D_FF), 0, 1)[None])
```

```python
import functools
import math

import jax
import jax.numpy as jnp
from jax import lax
from jax.experimental import pallas as pl
from jax.experimental.pallas import tpu as pltpu

F32 = jnp.float32
BF16 = jnp.bfloat16

D_MODEL = 4096
N_META = 16
D_RNN = 2048
D_GLA = 2048
RG_BLOCKS = 16
RG_BLOCK = 128
CONV_W = 4
RG_C = 8.0
GLA_HEADS = 16
GLA_DK = 128
GLA_RANK = 16
GLA_GATE_TAU = 16.0
D_FF = 3 * D_MODEL
FFN_CONV_W = 3
EPS = 1e-6
PROJ_COLS = 2 * D_RNN + 4 * D_GLA

LANES = 128
SUB = 8
VMEM_LIMIT = 60 * 1024 * 1024
GLA_SUB = 16
GLA_G = 4


def _params(sem):
    return pltpu.CompilerParams(dimension_semantics=sem, vmem_limit_bytes=VMEM_LIMIT)


def _gelu(x):
    c = math.sqrt(2.0 / math.pi)
    return 0.5 * x * (1.0 + jnp.tanh(c * (x + 0.044715 * (x * x * x))))


def _softplus(x):
    return jnp.maximum(x, 0.0) + jnp.log1p(jnp.exp(-jnp.abs(x)))


def _sigmoid(x):
    return 0.5 * (1.0 + jnp.tanh(0.5 * x))


def _rmsnorm_kernel(x_ref, g_ref, o_ref):
    x = x_ref[...]
    ms = jnp.mean(x * x, axis=-1, keepdims=True)
    o_ref[...] = (x * lax.rsqrt(ms + EPS) * g_ref[...]).astype(o_ref.dtype)


def rmsnorm(x, g, out_dtype, tm):
    m, d = x.shape
    return pl.pallas_call(
        _rmsnorm_kernel,
        out_shape=jax.ShapeDtypeStruct((m, d), out_dtype),
        grid=(m // tm,),
        in_specs=[pl.BlockSpec((tm, d), lambda i: (i, 0)),
                  pl.BlockSpec((1, d), lambda i: (0, 0))],
        out_specs=pl.BlockSpec((tm, d), lambda i: (i, 0)),
        compiler_params=_params(("parallel",)),
        name="rmsnorm",
    )(x, g.reshape(1, d))


def _mm_kernel(x_ref, w_ref, *rest, nk, has_res):
    if has_res:
        r_ref, o_ref = rest
    else:
        (o_ref,) = rest
    part = jnp.dot(x_ref[...], w_ref[...], preferred_element_type=F32)
    if nk == 1:
        o_ref[...] = part + r_ref[...] if has_res else part
        return
    k = pl.program_id(2)

    @pl.when(k == 0)
    def _():
        o_ref[...] = part + r_ref[...] if has_res else part

    @pl.when(k > 0)
    def _():
        o_ref[...] += part


def matmul(x, w, *, tm, tn, tk, n_cols=None, res=None):
    m, kdim = x.shape
    n = w.shape[1] if n_cols is None else n_cols
    nk = kdim // tk
    in_specs = [pl.BlockSpec((tm, tk), lambda j, i, k: (i, k)),
                pl.BlockSpec((tk, tn), lambda j, i, k: (k, j))]
    args = [x, w]
    if res is not None:
        in_specs.append(pl.BlockSpec((tm, tn), lambda j, i, k: (i, j)))
        args.append(res)
    return pl.pallas_call(
        functools.partial(_mm_kernel, nk=nk, has_res=res is not None),
        out_shape=jax.ShapeDtypeStruct((m, n), F32),
        grid=(n // tn, m // tm, nk),
        in_specs=in_specs,
        out_specs=pl.BlockSpec((tm, tn), lambda j, i, k: (i, j)),
        compiler_params=_params(("parallel", "parallel", "arbitrary")),
        name="matmul",
    )(*args)


def _group_steps(groups):
    steps, first = [], 0
    for rows, tm in groups:
        steps.append((first, rows // tm))
        first += rows // tm
    return steps, first


def _tile_index(i, first, count):
    return jnp.clip(i - first, 0, count - 1)


def _gmm_kernel(*refs, layout, steps, nk, cast_w):
    i = pl.program_id(1)
    pos = 0
    g_in = []
    for n_parts, has_res in layout:
        xs = refs[pos:pos + n_parts]
        pos += n_parts
        r = refs[pos] if has_res else None
        pos += int(has_res)
        g_in.append((xs, r))
    w_ref = refs[pos]
    outs = refs[pos + 1:pos + 1 + len(layout)]
    if cast_w:
        wsrc = refs[pos + 1 + len(layout)]

        @pl.when(i == 0)
        def _():
            wsrc[...] = w_ref[...].astype(BF16)
    else:
        wsrc = w_ref
    for (xs, r), o_ref, (first, count) in zip(g_in, outs, steps):
        @pl.when((i >= first) & (i < first + count))
        def _(xs=xs, r=r, o_ref=o_ref):
            acc, off = None, 0
            for x_ref in xs:
                kk = x_ref.shape[1]
                part = jnp.dot(x_ref[...], wsrc[off:off + kk, :], preferred_element_type=F32)
                acc = part if acc is None else acc + part
                off += kk
            if r is not None:
                first_val = acc + r[...]
            else:
                first_val = acc
            if nk == 1:
                o_ref[...] = first_val
            else:
                k = pl.program_id(2)

                @pl.when(k == 0)
                def _():
                    o_ref[...] = first_val

                @pl.when(k > 0)
                def _():
                    o_ref[...] += acc


def grouped_matmul(groups, w, *, tn, n_cols, tk=None, cast_w=False):
    kdim = sum(p.shape[1] for p in groups[0]["parts"])
    nk = 1 if tk is None else kdim // tk
    assert not (cast_w and nk > 1)
    steps, ni = _group_steps([(g["parts"][0].shape[0], g["tm"]) for g in groups])
    in_specs, args, layout = [], [], []
    for g, (first, count) in zip(groups, steps):
        assert nk == 1 or len(g["parts"]) == 1
        last_k = nk - 1

        def kidx(i, k, first=first, count=count):
            if nk == 1:
                return 0
            return jnp.where(i < first, 0, jnp.where(i < first + count, k, last_k))
        once = dict(pipeline_mode=pl.Buffered(1)) if (count == 1 and nk == 1) else {}
        for p in g["parts"]:
            kp = p.shape[1] if nk == 1 else tk
            in_specs.append(pl.BlockSpec(
                (g["tm"], kp),
                lambda j, i, k, first=first, count=count, kidx=kidx:
                    (_tile_index(i, first, count), kidx(i, k)), **once))
            args.append(p)
        if g.get("res") is not None:
            in_specs.append(pl.BlockSpec(
                (g["tm"], tn),
                lambda j, i, k, first=first, count=count: (_tile_index(i, first, count), j)))
            args.append(g["res"])
        layout.append((len(g["parts"]), g.get("res") is not None))
    kblk = kdim if nk == 1 else tk
    scratch = [pltpu.VMEM((kblk, tn), BF16)] if cast_w else []
    if w.ndim == 3:
        in_specs.append(pl.BlockSpec((None, kblk, tn), lambda j, i, k: (0, k, j)))
    else:
        in_specs.append(pl.BlockSpec((kblk, tn), lambda j, i, k: (k, j)))
    args.append(w)
    out_specs = [pl.BlockSpec((g["tm"], tn),
                              lambda j, i, k, first=first, count=count:
                                  (_tile_index(i, first, count), j))
                 for g, (first, count) in zip(groups, steps)]
    return pl.pallas_call(
        functools.partial(_gmm_kernel, layout=tuple(layout), steps=tuple(steps), nk=nk,
                          cast_w=cast_w),
        out_shape=[jax.ShapeDtypeStruct((g["parts"][0].shape[0], n_cols), F32) for g in groups],
        grid=(n_cols // tn, ni, nk),
        in_specs=in_specs,
        out_specs=out_specs,
        scratch_shapes=scratch,
        compiler_params=_params(("parallel", "arbitrary", "arbitrary")),
        name="grouped_matmul",
    )(*args)


def _rg_gates(xc, n, wax_ref, ba_ref, bx_ref, sp_row):
    cs = slice(n * RG_BLOCK, (n + 1) * RG_BLOCK)
    pre = jnp.dot(xc.astype(BF16), wax_ref[n], preferred_element_type=F32)
    r = _sigmoid(pre[:, :RG_BLOCK] + ba_ref[:, cs])
    i = _sigmoid(pre[:, RG_BLOCK:] + bx_ref[:, cs])
    log_a = -RG_C * r * sp_row[:, cs]
    a = jnp.exp(log_a)
    th = jnp.tanh(log_a)
    p = -2.0 * th
    mult = jnp.where(p > 0.0, p * lax.rsqrt(p * (1.0 - th)), 0.0)
    return a, mult, i


def _rg_long_kernel(xr_ref, yr_ref, cb_ref, h0_ref, cw_ref, cbias_ref, wax_ref, ba_ref, bx_ref,
                    lam_ref, gn_ref, out_ref, nconv_ref, hlast_ref,
                    ext_scr, a_scr, u_scr, h_scr, *, tt, reset_first):
    t = pl.program_id(1)
    nt = pl.num_programs(1)
    pad = SUB
    tail = CONV_W - 1

    @pl.when(t == 0)
    def _():
        ext_scr[0:pad - tail, :] = jnp.zeros((pad - tail, D_RNN), F32)
        ext_scr[pad - tail:pad, :] = cb_ref[0]
        h_scr[...] = h0_ref[0]

    ext_scr[pad:pad + tt, :] = xr_ref[...]
    sp_row = _softplus(-lam_ref[...])
    row = lax.broadcasted_iota(jnp.int32, (tt, RG_BLOCK), 0)
    first_row = jnp.where(t == 0, 0, -1)
    for n in range(RG_BLOCKS):
        cs = slice(n * RG_BLOCK, (n + 1) * RG_BLOCK)
        x = ext_scr[pad:pad + tt, cs]
        hist = ext_scr[0:pad, cs]
        xc = cbias_ref[:, cs]
        for i in range(CONV_W):
            s = tail - i
            if s == 0:
                xs = x
            else:
                xs = pltpu.roll(x, s, 0)
                head = jnp.where(row[:SUB] < s, pltpu.roll(hist, s, 0), xs[:SUB])
                xs = jnp.concatenate([head, xs[SUB:]], axis=0) if tt > SUB else head
            xc = xc + xs * cw_ref[i:i + 1, cs]
        a, mult, gate_i = _rg_gates(xc, n, wax_ref, ba_ref, bx_ref, sp_row)
        if reset_first:
            mult = jnp.where(row == first_row, 1.0, mult)
        a_scr[:, cs] = a
        u_scr[:, cs] = mult * gate_i * xc

    row8 = lax.broadcasted_iota(jnp.int32, (SUB, 512), 0)
    ncol = D_RNN // 512

    def scan_body(j, hs):
        r0 = pl.multiple_of(j * SUB, SUB)
        new = []
        for c in range(ncol):
            cs = slice(c * 512, (c + 1) * 512)
            a = a_scr[pl.ds(r0, SUB), cs]
            u = u_scr[pl.ds(r0, SUB), cs]
            for s in (1, 2, 4):
                a_sh = jnp.where(row8 >= s, pltpu.roll(a, s, 0), 1.0)
                u_sh = jnp.where(row8 >= s, pltpu.roll(u, s, 0), 0.0)
                u = a * u_sh + u
                a = a * a_sh
            h = a * hs[c] + u
            u_scr[pl.ds(r0, SUB), cs] = h
            new.append(h[SUB - 1:SUB, :])
        return tuple(new)

    h_in = tuple(h_scr[:, c * 512:(c + 1) * 512] for c in range(ncol))
    h_fin = lax.fori_loop(0, tt // SUB, scan_body, h_in)
    for c in range(ncol):
        h_scr[:, c * 512:(c + 1) * 512] = h_fin[c]

    y = u_scr[...] * _gelu(yr_ref[...])
    ms = jnp.mean(y * y, axis=-1, keepdims=True)
    out_ref[...] = (y * lax.rsqrt(ms + EPS) * gn_ref[...]).astype(out_ref.dtype)
    ext_scr[pad - tail:pad, :] = ext_scr[pad + tt - tail:pad + tt, :]

    @pl.when(t == nt - 1)
    def _():
        nconv_ref[0] = ext_scr[pad + tt - tail:pad + tt, :]
        hlast_ref[0] = h_scr[...]


def rg_long(proj, conv_buf, h0, wts, *, batch, seq, tt, reset_first):
    nt = seq // tt
    bsel = (lambda b: b) if conv_buf.shape[0] == batch else (lambda b: 0)
    vec = lambda r: pl.BlockSpec((r, D_RNN), lambda b, t: (0, 0))
    return pl.pallas_call(
        functools.partial(_rg_long_kernel, tt=tt, reset_first=reset_first),
        out_shape=(jax.ShapeDtypeStruct((batch * seq, D_RNN), BF16),
                   jax.ShapeDtypeStruct((batch, CONV_W - 1, D_RNN), F32),
                   jax.ShapeDtypeStruct((batch, 1, D_RNN), F32)),
        grid=(batch, nt),
        in_specs=[pl.BlockSpec((tt, D_RNN), lambda b, t: (b * nt + t, 0)),
                  pl.BlockSpec((tt, D_RNN), lambda b, t: (b * nt + t, 1)),
                  pl.BlockSpec((1, CONV_W - 1, D_RNN), lambda b, t: (bsel(b), 0, 0)),
                  pl.BlockSpec((1, 1, D_RNN), lambda b, t: (bsel(b), 0, 0)),
                  vec(CONV_W), vec(1),
                  pl.BlockSpec((RG_BLOCKS, RG_BLOCK, 2 * RG_BLOCK), lambda b, t: (0, 0, 0)),
                  vec(1), vec(1), vec(1), vec(1)],
        out_specs=(pl.BlockSpec((tt, D_RNN), lambda b, t: (b * nt + t, 0)),
                   pl.BlockSpec((1, CONV_W - 1, D_RNN), lambda b, t: (b, 0, 0)),
                   pl.BlockSpec((1, 1, D_RNN), lambda b, t: (b, 0, 0))),
        scratch_shapes=[pltpu.VMEM((tt + SUB, D_RNN), F32),
                        pltpu.VMEM((tt, D_RNN), F32),
                        pltpu.VMEM((tt, D_RNN), F32),
                        pltpu.VMEM((1, D_RNN), F32)],
        compiler_params=_params(("parallel", "arbitrary")),
        name="rg_long",
    )(proj, proj, conv_buf, h0, wts["rg_conv_w"], wts["rg_conv_b"], wts["rg_wax"],
      wts["rg_ba"], wts["rg_bx"], wts["rg_lambda"], wts["rg_out_norm"])


def _rg_short_kernel(xr_ref, yr_ref, cb_ref, h0_ref, cw_ref, cbias_ref, wax_ref, ba_ref, bx_ref,
                     lam_ref, gn_ref, out_ref, nconv_ref, hlast_ref, y_scr, *, nb, seq):
    sp_row = _softplus(-lam_ref[...])
    tail = CONV_W - 1
    for n in range(RG_BLOCKS):
        cs = slice(n * RG_BLOCK, (n + 1) * RG_BLOCK)
        ext = [cb_ref[i, :, cs] for i in range(tail)]
        ext += [xr_ref[t * nb:(t + 1) * nb, cs] for t in range(seq)]
        h = h0_ref[:, cs]
        for t in range(seq):
            xc = cbias_ref[:, cs]
            for i in range(CONV_W):
                xc = xc + ext[t + i] * cw_ref[i:i + 1, cs]
            a, mult, gate_i = _rg_gates(xc, n, wax_ref, ba_ref, bx_ref, sp_row)
            h = a * h + mult * gate_i * xc
            y_scr[t, :, cs] = h * _gelu(yr_ref[t * nb:(t + 1) * nb, cs])
        hlast_ref[:, cs] = h
        for i in range(tail):
            nconv_ref[i, :, cs] = ext[seq + i]
    for t in range(seq):
        y = y_scr[t]
        ms = jnp.mean(y * y, axis=-1, keepdims=True)
        out_ref[t * nb:(t + 1) * nb, :] = (y * lax.rsqrt(ms + EPS) * gn_ref[...]).astype(out_ref.dtype)


def rg_short(proj, conv_buf_t, h0, wts, *, nb, seq):
    rows = nb * seq
    vec = lambda r: pl.BlockSpec((r, D_RNN), lambda i: (0, 0))
    return pl.pallas_call(
        functools.partial(_rg_short_kernel, nb=nb, seq=seq),
        out_shape=(jax.ShapeDtypeStruct((rows, D_RNN), BF16),
                   jax.ShapeDtypeStruct((CONV_W - 1, nb, D_RNN), F32),
                   jax.ShapeDtypeStruct((nb, D_RNN), F32)),
        grid=(1,),
        in_specs=[pl.BlockSpec((rows, D_RNN), lambda i: (0, 0)),
                  pl.BlockSpec((rows, D_RNN), lambda i: (0, 1)),
                  pl.BlockSpec((CONV_W - 1, nb, D_RNN), lambda i: (0, 0, 0)),
                  pl.BlockSpec((nb, D_RNN), lambda i: (0, 0)),
                  vec(CONV_W), vec(1),
                  pl.BlockSpec((RG_BLOCKS, RG_BLOCK, 2 * RG_BLOCK), lambda i: (0, 0, 0)),
                  vec(1), vec(1), vec(1), vec(1)],
        out_specs=(pl.BlockSpec((rows, D_RNN), lambda i: (0, 0)),
                   pl.BlockSpec((CONV_W - 1, nb, D_RNN), lambda i: (0, 0, 0)),
                   pl.BlockSpec((nb, D_RNN), lambda i: (0, 0))),
        scratch_shapes=[pltpu.VMEM((seq, nb, D_RNN), F32)],
        compiler_params=_params(("arbitrary",)),
        name="rg_short",
    )(proj, proj, conv_buf_t, h0, wts["rg_conv_w"], wts["rg_conv_b"], wts["rg_wax"],
      wts["rg_ba"], wts["rg_bx"], wts["rg_lambda"], wts["rg_out_norm"])


def _gla_consts():
    er = lax.broadcasted_iota(jnp.int32, (LANES, LANES), 0)
    ec = lax.broadcasted_iota(jnp.int32, (LANES, LANES), 1)
    eye = jnp.where(er == ec, 1.0, 0.0).astype(BF16)
    ones = jnp.ones((LANES, LANES), BF16)
    row = lax.broadcasted_iota(jnp.int32, (GLA_SUB, LANES), 0)
    row8 = lax.broadcasted_iota(jnp.int32, (SUB, LANES), 0)
    pr = lax.broadcasted_iota(jnp.int32, (2 * GLA_SUB, LANES), 0)
    pick = jnp.where((pr == GLA_SUB) | (pr == GLA_SUB + 1), 1.0, 0.0).astype(BF16)
    return eye, ones, row, row8, pick


def _gla_local(units, consts):
    eye, ones, row, row8, pick = consts
    assert len(units) == GLA_G

    def scan8(x):
        for s in (1, 2, 4):
            x = x + jnp.where(row8 >= s, pltpu.roll(x, s, 0), 0.0)
        return x

    halves = (slice(0, SUB), slice(SUB, GLA_SUB))
    nt = (((1,), (1,)), ((), ()))
    qes, xs, qas, kbs, cum_all = [], [], [], [], []
    for q, k, v, g, j_range in units:
        cum_lo = scan8(g[halves[0]])
        edge = cum_lo[SUB - 1:SUB, :]
        cum_hi = scan8(g[halves[1]]) + edge
        cums = (cum_lo, cum_hi)
        cum_all.append(cums)
        last = cum_hi[SUB - 1:SUB, :]
        qes.append(jnp.concatenate([q[h] * jnp.exp(c) for h, c in zip(halves, cums)],
                                   axis=0).astype(BF16))
        qas.append(q[halves[1]] * jnp.exp(cum_hi - edge))
        kbs.append(k[halves[0]] * jnp.exp(edge - cum_lo))
        ke = jnp.concatenate([k[h] * jnp.exp(last - c) for h, c in zip(halves, cums)], axis=0)
        el = jnp.exp(last)
        e_hi = el.astype(BF16).astype(F32)
        e_lo = el - e_hi
        extra = jnp.where(row == 0, e_hi, jnp.where(row == 1, e_lo, 0.0))
        xs.append(jnp.concatenate([ke, extra], axis=0).astype(BF16))
    crossing = [u[4][0] < SUB <= u[4][-1] for u in units]
    sc = None
    if any(crossing):
        qa = jnp.concatenate(qas, axis=0).astype(BF16)
        kb = jnp.concatenate(kbs, axis=0).astype(BF16)
        sc = lax.dot_general(qa, kb, nt, preferred_element_type=F32)
    xt = lax.dot_general(eye, jnp.concatenate(xs, axis=0), nt, preferred_element_type=F32)
    o_halves = []
    for (q, k, v, g, j_range), cums in zip(units, cum_all):
        ps = []
        for j in j_range:
            h, c, jj = halves[j // SUB], cums[j // SUB], j % SUB
            w = jnp.exp(jnp.where(row8 >= jj, c - c[jj:jj + 1, :], -jnp.inf))
            ps.append(q[h] * w * k[j:j + 1, :])
        att = jnp.dot(jnp.concatenate(ps, axis=0).astype(BF16), ones, preferred_element_type=F32)
        o_half = [None, None]
        for idx, j in enumerate(j_range):
            term = att[idx * SUB:(idx + 1) * SUB, :] * v[j:j + 1, :]
            o_half[j // SUB] = term if o_half[j // SUB] is None else o_half[j // SUB] + term
        o_halves.append(o_half)
    cross = None
    if sc is not None:
        n = GLA_G * SUB
        ur = lax.broadcasted_iota(jnp.int32, (n, n), 0) // SUB
        uc = lax.broadcasted_iota(jnp.int32, (n, n), 1) // SUB
        sc = jnp.where(ur == uc, sc, 0.0).astype(BF16)
        v_lo = jnp.concatenate([u[2][halves[0]] for u in units], axis=0).astype(BF16)
        cross = jnp.dot(sc, v_lo, preferred_element_type=F32)
    xt = xt.astype(BF16)
    zblk = jnp.zeros((2 * GLA_SUB, LANES), BF16)
    wrows = []
    for u, (_, _, v, _, _) in enumerate(units):
        vpad = jnp.concatenate([v, jnp.zeros_like(v)], axis=0).astype(BF16)
        wrows.append(jnp.concatenate([zblk] * (2 * u) + [vpad, pick]
                                     + [zblk] * (2 * (GLA_G - 1 - u)), axis=1))
    kd = jnp.dot(xt, jnp.concatenate(wrows, axis=0), preferred_element_type=F32)
    out = []
    for u, o_half in enumerate(o_halves):
        if cross is not None and crossing[u]:
            o_half[1] = o_half[1] + cross[u * SUB:(u + 1) * SUB, :]
        o = jnp.concatenate([jnp.zeros((SUB, LANES), F32) if t is None else t for t in o_half],
                            axis=0)
        kv = kd[:, (2 * u) * LANES:(2 * u + 1) * LANES]
        dec = kd[:, (2 * u + 1) * LANES:(2 * u + 2) * LANES]
        out.append((qes[u], o, kv, dec))
    return out


def _gla_log_alpha(z_ref, wa2_ref, ba2_ref):
    zz = jnp.dot(z_ref[...].astype(BF16), wa2_ref[...], preferred_element_type=F32) + ba2_ref[...]
    return (jnp.minimum(zz, 0.0) - jnp.log1p(jnp.exp(-jnp.abs(zz)))) * (1.0 / GLA_GATE_TAU)


def _gla_finish(o, go, gain):
    ms = jnp.mean(o * o, axis=-1, keepdims=True)
    on = o * lax.rsqrt(ms + EPS) * gain
    return on * (go * _sigmoid(go))


def _gla_long_kernel(q_ref, k_ref, v_ref, go_ref, z_ref, wa2_ref, ba2_ref, gain_ref, s0_ref,
                     *rest, tt, n_casts):
    cast_in = rest[:n_casts]
    out_ref, sout_ref = rest[n_casts:n_casts + 2]
    cast_out = rest[n_casts + 2:2 * n_casts + 2]
    s_scr, gl_scr, o_scr, qe_scr, kv_scr, dec_scr, sall_scr = rest[2 * n_casts + 2:]
    t = pl.program_id(2)
    nt = pl.num_programs(2)
    nsub = tt // GLA_SUB
    consts = _gla_consts()
    scale = GLA_DK ** -0.5

    for src, dst in zip(cast_in, cast_out):
        dst[...] = src[...].astype(BF16)

    @pl.when(t == 0)
    def _():
        s_scr[...] = s0_ref[0]

    gl_scr[...] = _gla_log_alpha(z_ref, wa2_ref, ba2_ref)

    def local_body(i, carry):
        rows = pl.ds(pl.multiple_of(i * GLA_SUB, GLA_SUB), GLA_SUB)
        heads = [slice(h * LANES, (h + 1) * LANES) for h in range(GLA_G)]
        units = [(q_ref[rows, cs] * scale, k_ref[rows, cs], v_ref[rows, cs], gl_scr[rows, cs],
                  range(GLA_SUB)) for cs in heads]
        for h, (qe, o, kv, dec) in enumerate(_gla_local(units, consts)):
            cs = heads[h]
            qe_scr[rows, cs] = qe
            o_scr[rows, cs] = o
            kv_scr[i, h] = kv
            dec_scr[i, h] = dec
        return carry

    lax.fori_loop(0, nsub, local_body, 0, unroll=min(4, nsub))

    for h in range(GLA_G):
        def state_body(i, s, h=h):
            sall_scr[i, h] = s.astype(BF16)
            return dec_scr[i, h] * s + kv_scr[i, h]
        s_scr[h] = lax.fori_loop(0, nsub, state_body, s_scr[h])

    def inter_body(i, carry):
        rows = pl.ds(pl.multiple_of(i * GLA_SUB, GLA_SUB), GLA_SUB)
        for h in range(GLA_G):
            cs = slice(h * LANES, (h + 1) * LANES)
            o_scr[rows, cs] += jnp.dot(qe_scr[rows, cs], sall_scr[i, h], preferred_element_type=F32)
        return carry

    lax.fori_loop(0, nsub, inter_body, 0, unroll=min(8, nsub))
    for h in range(GLA_G):
        cs = slice(h * LANES, (h + 1) * LANES)
        out_ref[:, cs] = _gla_finish(o_scr[:, cs], go_ref[:, cs], gain_ref[:, cs]).astype(out_ref.dtype)

    @pl.when(t == nt - 1)
    def _():
        sout_ref[0] = s_scr[...]


def gla_long(proj, z, s0, wts, *, batch, seq, tt, casts=()):
    nt = seq // tt
    gw = GLA_G * LANES
    base = 2 * D_RNN // gw
    per = D_GLA // gw
    nhg = GLA_HEADS // GLA_G
    bsel = (lambda b: b) if s0.shape[0] == batch else (lambda b: 0)
    col = lambda which: pl.BlockSpec((tt, gw), lambda b, h, t: (b * nt + t, base + which * per + h))
    n_steps = batch * nhg * nt
    cast_specs = []
    for a in casts:
        rblocks = math.gcd(n_steps, a.shape[0] // 16)
        cblocks = n_steps // rblocks
        blk = (a.shape[0] // rblocks, a.shape[1] // cblocks)
        assert blk[0] * rblocks == a.shape[0] and blk[1] * cblocks == a.shape[1]
        assert blk[0] % 16 == 0 and blk[1] % LANES == 0
        cast_specs.append(pl.BlockSpec(
            blk, lambda b, h, t, cblocks=cblocks:
                (((b * nhg + h) * nt + t) // cblocks, ((b * nhg + h) * nt + t) % cblocks)))
    return pl.pallas_call(
        functools.partial(_gla_long_kernel, tt=tt, n_casts=len(casts)),
        out_shape=(jax.ShapeDtypeStruct((batch * seq, D_GLA), BF16),
                   jax.ShapeDtypeStruct((batch, GLA_HEADS, GLA_DK, GLA_DK), F32),
                   *[jax.ShapeDtypeStruct(a.shape, BF16) for a in casts]),
        grid=(batch, nhg, nt),
        in_specs=[col(0), col(1), col(2), col(3),
                  pl.BlockSpec((tt, LANES), lambda b, h, t: (b * nt + t, 0)),
                  pl.BlockSpec((LANES, gw), lambda b, h, t: (0, h)),
                  pl.BlockSpec((1, gw), lambda b, h, t: (0, h)),
                  pl.BlockSpec((1, gw), lambda b, h, t: (0, h)),
                  pl.BlockSpec((1, GLA_G, GLA_DK, GLA_DK), lambda b, h, t: (bsel(b), h, 0, 0)),
                  *cast_specs],
        out_specs=(pl.BlockSpec((tt, gw), lambda b, h, t: (b * nt + t, h)),
                   pl.BlockSpec((1, GLA_G, GLA_DK, GLA_DK), lambda b, h, t: (b, h, 0, 0)),
                   *cast_specs),
        scratch_shapes=[pltpu.VMEM((GLA_G, GLA_DK, GLA_DK), F32),
                        pltpu.VMEM((tt, gw), F32),
                        pltpu.VMEM((tt, gw), F32),
                        pltpu.VMEM((tt, gw), BF16),
                        pltpu.VMEM((tt // GLA_SUB, GLA_G, GLA_DK, GLA_DK), F32),
                        pltpu.VMEM((tt // GLA_SUB, GLA_G, GLA_DK, GLA_DK), F32),
                        pltpu.VMEM((tt // GLA_SUB, GLA_G, GLA_DK, GLA_DK), BF16)],
        compiler_params=_params(("parallel", "parallel", "arbitrary")),
        name="gla_long",
    )(proj, proj, proj, proj, z, wts["gla_wa2"], wts["gla_ba2"], wts["gla_head_norm"], s0, *casts)


def _gla_short_kernel(q_ref, k_ref, v_ref, go_ref, z_ref, wa2_ref, ba2_ref, gain_ref, s0_ref,
                      out_ref, sout_ref, *, seq):
    consts = _gla_consts()
    row = consts[2]
    scale = GLA_DK ** -0.5
    gl = _gla_log_alpha(z_ref, wa2_ref, ba2_ref)
    nseq = GLA_SUB // seq
    assert nseq == GLA_G
    local = {}
    for h in range(GLA_G):
        cs = slice(h * LANES, (h + 1) * LANES)
        q = q_ref[:, cs] * scale
        k = k_ref[:, cs]
        v = v_ref[:, cs]
        g = gl[:, cs]
        units = []
        for b in range(nseq):
            mine = (row >= b * seq) & (row < (b + 1) * seq)
            zero = lambda a, mine=mine: jnp.where(mine, a, 0.0)
            units.append((zero(q), zero(k), zero(v), zero(g), range(b * seq, (b + 1) * seq)))
        for b, res in enumerate(_gla_local(units, consts)):
            local[b, h] = res
    for h in range(GLA_G):
        cs = slice(h * LANES, (h + 1) * LANES)
        o = jnp.zeros((GLA_SUB, LANES), F32)
        for b in range(nseq):
            qe, o_b, kv, dec = local[b, h]
            s = s0_ref[b, h]
            o = o + o_b + jnp.dot(qe, s.astype(BF16), preferred_element_type=F32)
            sout_ref[b, h] = dec * s + kv
        out_ref[:, cs] = _gla_finish(o, go_ref[:, cs], gain_ref[:, cs]).astype(out_ref.dtype)


def gla_short(proj, z, s0, wts, *, nb, seq):
    gw = GLA_G * LANES
    per = D_GLA // gw
    bpb = GLA_SUB // seq
    col = lambda which: pl.BlockSpec((GLA_SUB, gw), lambda i, h: (i, which * per + h))
    return pl.pallas_call(
        functools.partial(_gla_short_kernel, seq=seq),
        out_shape=(jax.ShapeDtypeStruct((nb * seq, D_GLA), BF16),
                   jax.ShapeDtypeStruct((nb, GLA_HEADS, GLA_DK, GLA_DK), F32)),
        grid=(nb // bpb, GLA_HEADS // GLA_G),
        in_specs=[col(0), col(1), col(2), col(3),
                  pl.BlockSpec((GLA_SUB, LANES), lambda i, h: (i, 0)),
                  pl.BlockSpec((LANES, gw), lambda i, h: (0, h)),
                  pl.BlockSpec((1, gw), lambda i, h: (0, h)),
                  pl.BlockSpec((1, gw), lambda i, h: (0, h)),
                  pl.BlockSpec((bpb, GLA_G, GLA_DK, GLA_DK), lambda i, h: (i, h, 0, 0))],
        out_specs=(pl.BlockSpec((GLA_SUB, gw), lambda i, h: (i, h)),
                   pl.BlockSpec((bpb, GLA_G, GLA_DK, GLA_DK), lambda i, h: (i, h, 0, 0))),
        compiler_params=_params(("parallel", "parallel")),
        name="gla_short",
    )(proj, proj, proj, proj, z, wts["gla_wa2"], wts["gla_ba2"], wts["gla_head_norm"], s0)


def _ffn_kernel(xm_ref, xs_ref, xp_ref, wg_ref, wu_ref, cw_ref, cb_ref, bufs_ref,
                hp_ref, hs_ref, nbufp_ref, nbufs_ref, extp_scr, exts_scr, mhist_scr,
                *, steps, tiles_per_seq, nb):
    i = pl.program_id(1)
    (m_first, _), (s_first, _), (p_first, p_count) = steps
    tail = FFN_CONV_W - 1
    pad = SUB
    tm = xp_ref.shape[0]
    rows_s = xs_ref.shape[0]
    hist_s = tail * nb

    def conv(ext_ref, start, shift, rows):
        gc = cb_ref[...]
        for c in range(FFN_CONV_W):
            lo = start - (tail - c) * shift
            gc = gc + ext_ref[lo:lo + rows, :] * cw_ref[c:c + 1, :]
        return gc

    @pl.when(i == m_first)
    def _():
        g = jnp.dot(xm_ref[...], wg_ref[...], preferred_element_type=F32)
        mhist_scr[...] = g[g.shape[0] - tail:, :]

    @pl.when((i >= p_first) & (i < p_first + p_count))
    def _():
        @pl.when((i - p_first) % tiles_per_seq == 0)
        def _():
            extp_scr[pad - tail:pad, :] = mhist_scr[...]

        x = xp_ref[...]
        extp_scr[pad:pad + tm, :] = jnp.dot(x, wg_ref[...], preferred_element_type=F32)
        up = jnp.dot(x, wu_ref[...], preferred_element_type=F32)
        hp_ref[...] = (_gelu(conv(extp_scr, pad, 1, tm)) * up).astype(hp_ref.dtype)
        last = extp_scr[pad + tm - tail:pad + tm, :]
        extp_scr[pad - tail:pad, :] = last
        nbufp_ref[0] = last

    @pl.when(i == s_first)
    def _():
        x = xs_ref[...]
        exts_scr[0:hist_s, :] = bufs_ref[...]
        exts_scr[hist_s:hist_s + rows_s, :] = jnp.dot(x, wg_ref[...], preferred_element_type=F32)
        up = jnp.dot(x, wu_ref[...], preferred_element_type=F32)
        hs_ref[...] = (_gelu(conv(exts_scr, hist_s, nb, rows_s)) * up).astype(hs_ref.dtype)
        nbufs_ref[...] = exts_scr[rows_s:rows_s + hist_s, :]


def ffn_gate(xm, xs, xp, bufs, wts, *, tm, tn, seq_rows, nb):
    rows_p, rows_s = xp.shape[0], xs.shape[0]
    tail = FFN_CONV_W - 1
    steps, ni = _group_steps([(xm.shape[0], xm.shape[0]), (rows_s, rows_s), (rows_p, tm)])
    (_, _), (_, _), (p_first, p_count) = steps
    tps = seq_rows // tm
    ptile = lambda i: _tile_index(i, p_first, p_count)
    const = lambda shape: pl.BlockSpec(shape, lambda j, i: (0, 0), pipeline_mode=pl.Buffered(1))
    colblk = lambda r: pl.BlockSpec((r, tn), lambda j, i: (0, j))
    return pl.pallas_call(
        functools.partial(_ffn_kernel, steps=tuple(steps), tiles_per_seq=tps, nb=nb),
        out_shape=(jax.ShapeDtypeStruct((rows_p, D_FF), BF16),
                   jax.ShapeDtypeStruct((rows_s, D_FF), BF16),
                   jax.ShapeDtypeStruct((rows_p // seq_rows, tail, D_FF), F32),
                   jax.ShapeDtypeStruct((tail * nb, D_FF), F32)),
        grid=(D_FF // tn, ni),
        in_specs=[const(xm.shape),
                  const(xs.shape),
                  pl.BlockSpec((tm, D_MODEL), lambda j, i: (ptile(i), 0)),
                  colblk(D_MODEL), colblk(D_MODEL), colblk(FFN_CONV_W), colblk(1),
                  colblk(tail * nb)],
        out_specs=(pl.BlockSpec((tm, tn), lambda j, i: (ptile(i), j)),
                   colblk(rows_s),
                   pl.BlockSpec((1, tail, tn), lambda j, i: (ptile(i) // tps, 0, j)),
                   colblk(tail * nb)),
        scratch_shapes=[pltpu.VMEM((tm + SUB, tn), F32),
                        pltpu.VMEM((rows_s + tail * nb, tn), F32),
                        pltpu.VMEM((tail, tn), F32)],
        compiler_params=_params(("parallel", "arbitrary")),
        name="ffn_gate",
    )(xm, xs, xp, wts["w_gate"], wts["w_up"], wts["ffn_conv_w"], wts["ffn_conv_b"], bufs)


PROMPT_TM = 1024
DOWN_TM = 512
DOWN_TN = 512
FFN_TM = 1024
FFN_TN = 512
SEQ_TT = 256
NORM_TM = 256


def kernel(x_prompt, x_sample, state_rglru_conv, state_rglru_h, state_gla_S, state_ffn_conv,
           meta_tokens, norm_mix, w_in, rg_conv_w, rg_conv_b, rg_wa, rg_ba, rg_wx, rg_bx, rg_lambda,
           rg_out_norm, gla_wa2, gla_ba2, gla_head_norm, w_out, norm_ffn, w_gate, w_up,
           ffn_conv_w, ffn_conv_b, w_down, final_norm):
    batch, seq, _ = x_prompt.shape
    nb, sseq, _ = x_sample.shape
    row = lambda a: a.reshape(1, -1)
    w_in_b = w_in[0].astype(BF16)
    wts = {
        "w_in": w_in_b,
        "w_z": jnp.pad(w_in_b[:, PROJ_COLS:], ((0, 0), (0, LANES - GLA_RANK))),
        "rg_conv_w": rg_conv_w[0], "rg_conv_b": row(rg_conv_b[0]),
        "rg_wax": jnp.concatenate([rg_wa[0], rg_wx[0]], axis=-1).astype(BF16),
        "rg_ba": row(rg_ba[0]), "rg_bx": row(rg_bx[0]), "rg_lambda": row(rg_lambda[0]),
        "rg_out_norm": row(rg_out_norm[0]),
        "gla_wa2": jnp.pad(gla_wa2[0].astype(BF16), ((0, LANES - GLA_RANK), (0, 0))),
        "gla_ba2": row(gla_ba2[0]), "gla_head_norm": row(gla_head_norm[0]),
        "ffn_conv_w": ffn_conv_w[0], "ffn_conv_b": row(ffn_conv_b[0]),
    }

    def to_batch_major(a):
        return jnp.swapaxes(a.reshape(sseq, nb, -1), 0, 1).reshape(nb * sseq, -1)

    def to_time_major(a):
        return jnp.swapaxes(a.reshape(nb, sseq, -1), 0, 1).reshape(nb * sseq, -1)

    rows_p, rows_s = batch * seq, nb * sseq
    xs = [meta_tokens, to_time_major(x_sample), x_prompt.reshape(rows_p, D_MODEL)]
    tms = [N_META, rows_s, PROMPT_TM]
    norm_tms = [N_META, NORM_TM, NORM_TM]

    def norm_all(arrs, g, dtype):
        return [rmsnorm(a, g, dtype, t) for a, t in zip(arrs, norm_tms)]

    xn = norm_all(xs, norm_mix[0], BF16)
    proj_m, proj_s, proj_p = grouped_matmul(
        [dict(parts=[a], tm=t) for a, t in zip(xn, tms)], wts["w_in"], tn=1024, n_cols=PROJ_COLS)
    z_m, z_s, z_p = [matmul(a, wts["w_z"], tm=t, tn=LANES, tk=D_MODEL) for a, t in zip(xn, tms)]

    zeros = lambda *s: jnp.zeros(s, F32)
    rnn_m, m_conv, m_h = rg_long(proj_m, zeros(1, CONV_W - 1, D_RNN), zeros(1, 1, D_RNN), wts,
                                 batch=1, seq=N_META, tt=N_META, reset_first=True)
    gla_m, m_s = gla_long(proj_m, z_m, zeros(1, GLA_HEADS, GLA_DK, GLA_DK), wts, batch=1,
                          seq=N_META, tt=N_META)
    rnn_p, p_conv, p_h = rg_long(proj_p, m_conv, m_h, wts, batch=batch, seq=seq, tt=SEQ_TT,
                                 reset_first=False)
    gla_p, p_s, wts["w_gate"], wts["w_up"], wts["w_down"] = gla_long(
        proj_p, z_p, m_s, wts, batch=batch, seq=seq, tt=SEQ_TT,
        casts=(w_gate.reshape(D_MODEL, D_FF), w_up.reshape(D_MODEL, D_FF),
               w_down.reshape(D_FF, D_MODEL)))
    rnn_s, s_conv_t, s_h = rg_short(proj_s, jnp.swapaxes(state_rglru_conv[0], 0, 1),
                                    state_rglru_h[0], wts, nb=nb, seq=sseq)
    gla_sb, s_s = gla_short(to_batch_major(proj_s[:, 2 * D_RNN:]), to_batch_major(z_s),
                            state_gla_S[0], wts, nb=nb, seq=sseq)
    gla_s = to_time_major(gla_sb)

    x1 = grouped_matmul(
        [dict(parts=[r, g], tm=t, res=x)
         for r, g, t, x in zip([rnn_m, rnn_s, rnn_p], [gla_m, gla_s, gla_p], tms, xs)],
        w_out, tn=512, n_cols=D_MODEL, cast_w=True)
    xn2 = norm_all(x1, norm_ffn[0], BF16)
    tail = FFN_CONV_W - 1
    bufs = jnp.swapaxes(state_ffn_conv[0], 0, 1).reshape(tail * nb, D_FF)
    hid_p, hid_s, p_ffn, s_ffn_t = ffn_gate(xn2[0], xn2[1], xn2[2], bufs, wts,
                                            tm=FFN_TM, tn=FFN_TN, seq_rows=seq, nb=nb)
    (x2_s,) = grouped_matmul([dict(parts=[hid_s], tm=rows_s, res=x1[1])], wts["w_down"],
                             tn=DOWN_TN, n_cols=D_MODEL)
    (x2_p,) = grouped_matmul([dict(parts=[hid_p], tm=DOWN_TM, res=x1[2])], wts["w_down"],
                             tn=DOWN_TN, n_cols=D_MODEL)
    y_p = rmsnorm(x2_p, final_norm, F32, NORM_TM)
    y_s = to_batch_major(rmsnorm(x2_s, final_norm, F32, NORM_TM))

    return (y_p.reshape(batch, seq, D_MODEL), y_s.reshape(nb, sseq, D_MODEL),
            p_conv[None], p_h.reshape(1, batch, D_RNN), p_s[None], p_ffn[None],
            jnp.swapaxes(s_conv_t, 0, 1)[None], s_h[None], s_s[None],
            jnp.swapaxes(s_ffn_t.reshape(tail, nb, D_FF), 0, 1)[None])
```

```python
import functools
import math

import jax
import jax.numpy as jnp
from jax import lax
from jax.experimental import pallas as pl
from jax.experimental.pallas import tpu as pltpu

F32 = jnp.float32
BF16 = jnp.bfloat16

D_MODEL = 4096
N_META = 16
D_RNN = 2048
D_GLA = 2048
RG_BLOCKS = 16
RG_BLOCK = 128
CONV_W = 4
RG_C = 8.0
GLA_HEADS = 16
GLA_DK = 128
GLA_RANK = 16
GLA_GATE_TAU = 16.0
D_FF = 3 * D_MODEL
FFN_CONV_W = 3
EPS = 1e-6
PROJ_COLS = 2 * D_RNN + 4 * D_GLA

LANES = 128
SUB = 8
VMEM_LIMIT = 60 * 1024 * 1024
GLA_SUB = 16
GLA_G = 4


def _params(sem):
    return pltpu.CompilerParams(dimension_semantics=sem, vmem_limit_bytes=VMEM_LIMIT)


def _gelu(x):
    c = math.sqrt(2.0 / math.pi)
    return 0.5 * x * (1.0 + jnp.tanh(c * (x + 0.044715 * (x * x * x))))


def _softplus(x):
    return jnp.maximum(x, 0.0) + jnp.log1p(jnp.exp(-jnp.abs(x)))


def _sigmoid(x):
    return 0.5 * (1.0 + jnp.tanh(0.5 * x))


def _rmsnorm_kernel(x_ref, g_ref, o_ref):
    x = x_ref[...]
    ms = jnp.mean(x * x, axis=-1, keepdims=True)
    o_ref[...] = (x * lax.rsqrt(ms + EPS) * g_ref[...]).astype(o_ref.dtype)


def rmsnorm(x, g, out_dtype, tm):
    m, d = x.shape
    return pl.pallas_call(
        _rmsnorm_kernel,
        out_shape=jax.ShapeDtypeStruct((m, d), out_dtype),
        grid=(m // tm,),
        in_specs=[pl.BlockSpec((tm, d), lambda i: (i, 0)),
                  pl.BlockSpec((1, d), lambda i: (0, 0))],
        out_specs=pl.BlockSpec((tm, d), lambda i: (i, 0)),
        compiler_params=_params(("parallel",)),
        name="rmsnorm",
    )(x, g.reshape(1, d))


def _mm_kernel(x_ref, w_ref, *rest, nk, has_res):
    if has_res:
        r_ref, o_ref = rest
    else:
        (o_ref,) = rest
    part = jnp.dot(x_ref[...], w_ref[...], preferred_element_type=F32)
    if nk == 1:
        o_ref[...] = part + r_ref[...] if has_res else part
        return
    k = pl.program_id(2)

    @pl.when(k == 0)
    def _():
        o_ref[...] = part + r_ref[...] if has_res else part

    @pl.when(k > 0)
    def _():
        o_ref[...] += part


def matmul(x, w, *, tm, tn, tk, n_cols=None, res=None):
    m, kdim = x.shape
    n = w.shape[1] if n_cols is None else n_cols
    nk = kdim // tk
    in_specs = [pl.BlockSpec((tm, tk), lambda j, i, k: (i, k)),
                pl.BlockSpec((tk, tn), lambda j, i, k: (k, j))]
    args = [x, w]
    if res is not None:
        in_specs.append(pl.BlockSpec((tm, tn), lambda j, i, k: (i, j)))
        args.append(res)
    return pl.pallas_call(
        functools.partial(_mm_kernel, nk=nk, has_res=res is not None),
        out_shape=jax.ShapeDtypeStruct((m, n), F32),
        grid=(n // tn, m // tm, nk),
        in_specs=in_specs,
        out_specs=pl.BlockSpec((tm, tn), lambda j, i, k: (i, j)),
        compiler_params=_params(("parallel", "parallel", "arbitrary")),
        name="matmul",
    )(*args)


def _group_steps(groups):
    steps, first = [], 0
    for rows, tm in groups:
        steps.append((first, rows // tm))
        first += rows // tm
    return steps, first


def _tile_index(i, first, count):
    return jnp.clip(i - first, 0, count - 1)


def _gmm_kernel(*refs, layout, steps, nk, cast_w):
    i = pl.program_id(1)
    pos = 0
    g_in = []
    for n_parts, has_res in layout:
        xs = refs[pos:pos + n_parts]
        pos += n_parts
        r = refs[pos] if has_res else None
        pos += int(has_res)
        g_in.append((xs, r))
    w_ref = refs[pos]
    outs = refs[pos + 1:pos + 1 + len(layout)]
    if cast_w:
        wsrc = refs[pos + 1 + len(layout)]

        @pl.when(i == 0)
        def _():
            wsrc[...] = w_ref[...].astype(BF16)
    else:
        wsrc = w_ref
    for (xs, r), o_ref, (first, count) in zip(g_in, outs, steps):
        @pl.when((i >= first) & (i < first + count))
        def _(xs=xs, r=r, o_ref=o_ref):
            acc, off = None, 0
            for x_ref in xs:
                kk = x_ref.shape[1]
                part = jnp.dot(x_ref[...], wsrc[off:off + kk, :], preferred_element_type=F32)
                acc = part if acc is None else acc + part
                off += kk
            if r is not None:
                first_val = acc + r[...]
            else:
                first_val = acc
            if nk == 1:
                o_ref[...] = first_val
            else:
                k = pl.program_id(2)

                @pl.when(k == 0)
                def _():
                    o_ref[...] = first_val

                @pl.when(k > 0)
                def _():
                    o_ref[...] += acc


def grouped_matmul(groups, w, *, tn, n_cols, tk=None, cast_w=False):
    kdim = sum(p.shape[1] for p in groups[0]["parts"])
    nk = 1 if tk is None else kdim // tk
    assert not (cast_w and nk > 1)
    steps, ni = _group_steps([(g["parts"][0].shape[0], g["tm"]) for g in groups])
    in_specs, args, layout = [], [], []
    for g, (first, count) in zip(groups, steps):
        assert nk == 1 or len(g["parts"]) == 1
        last_k = nk - 1

        def kidx(i, k, first=first, count=count):
            if nk == 1:
                return 0
            return jnp.where(i < first, 0, jnp.where(i < first + count, k, last_k))
        once = dict(pipeline_mode=pl.Buffered(1)) if (count == 1 and nk == 1) else {}
        for p in g["parts"]:
            kp = p.shape[1] if nk == 1 else tk
            in_specs.append(pl.BlockSpec(
                (g["tm"], kp),
                lambda j, i, k, first=first, count=count, kidx=kidx:
                    (_tile_index(i, first, count), kidx(i, k)), **once))
            args.append(p)
        if g.get("res") is not None:
            in_specs.append(pl.BlockSpec(
                (g["tm"], tn),
                lambda j, i, k, first=first, count=count: (_tile_index(i, first, count), j)))
            args.append(g["res"])
        layout.append((len(g["parts"]), g.get("res") is not None))
    kblk = kdim if nk == 1 else tk
    scratch = [pltpu.VMEM((kblk, tn), BF16)] if cast_w else []
    if w.ndim == 3:
        in_specs.append(pl.BlockSpec((None, kblk, tn), lambda j, i, k: (0, k, j)))
    else:
        in_specs.append(pl.BlockSpec((kblk, tn), lambda j, i, k: (k, j)))
    args.append(w)
    out_specs = [pl.BlockSpec((g["tm"], tn),
                              lambda j, i, k, first=first, count=count:
                                  (_tile_index(i, first, count), j))
                 for g, (first, count) in zip(groups, steps)]
    return pl.pallas_call(
        functools.partial(_gmm_kernel, layout=tuple(layout), steps=tuple(steps), nk=nk,
                          cast_w=cast_w),
        out_shape=[jax.ShapeDtypeStruct((g["parts"][0].shape[0], n_cols), F32) for g in groups],
        grid=(n_cols // tn, ni, nk),
        in_specs=in_specs,
        out_specs=out_specs,
        scratch_shapes=scratch,
        compiler_params=_params(("parallel", "arbitrary", "arbitrary")),
        name="grouped_matmul",
    )(*args)


def _rg_gates(xc, n, wax_ref, ba_ref, bx_ref, sp_row):
    cs = slice(n * RG_BLOCK, (n + 1) * RG_BLOCK)
    pre = jnp.dot(xc.astype(BF16), wax_ref[n], preferred_element_type=F32)
    r = _sigmoid(pre[:, :RG_BLOCK] + ba_ref[:, cs])
    i = _sigmoid(pre[:, RG_BLOCK:] + bx_ref[:, cs])
    log_a = -RG_C * r * sp_row[:, cs]
    a = jnp.exp(log_a)
    th = jnp.tanh(log_a)
    p = -2.0 * th
    mult = jnp.where(p > 0.0, p * lax.rsqrt(p * (1.0 - th)), 0.0)
    return a, mult, i


def _rg_long_kernel(xr_ref, yr_ref, cb_ref, h0_ref, cw_ref, cbias_ref, wax_ref, ba_ref, bx_ref,
                    lam_ref, gn_ref, out_ref, nconv_ref, hlast_ref,
                    ext_scr, a_scr, u_scr, h_scr, *, tt, reset_first):
    t = pl.program_id(1)
    nt = pl.num_programs(1)
    pad = SUB
    tail = CONV_W - 1

    @pl.when(t == 0)
    def _():
        ext_scr[0:pad - tail, :] = jnp.zeros((pad - tail, D_RNN), F32)
        ext_scr[pad - tail:pad, :] = cb_ref[0]
        h_scr[...] = h0_ref[0]

    ext_scr[pad:pad + tt, :] = xr_ref[...]
    sp_row = _softplus(-lam_ref[...])
    row = lax.broadcasted_iota(jnp.int32, (tt, RG_BLOCK), 0)
    first_row = jnp.where(t == 0, 0, -1)
    for n in range(RG_BLOCKS):
        cs = slice(n * RG_BLOCK, (n + 1) * RG_BLOCK)
        x = ext_scr[pad:pad + tt, cs]
        hist = ext_scr[0:pad, cs]
        xc = cbias_ref[:, cs]
        for i in range(CONV_W):
            s = tail - i
            if s == 0:
                xs = x
            else:
                xs = pltpu.roll(x, s, 0)
                head = jnp.where(row[:SUB] < s, pltpu.roll(hist, s, 0), xs[:SUB])
                xs = jnp.concatenate([head, xs[SUB:]], axis=0) if tt > SUB else head
            xc = xc + xs * cw_ref[i:i + 1, cs]
        a, mult, gate_i = _rg_gates(xc, n, wax_ref, ba_ref, bx_ref, sp_row)
        if reset_first:
            mult = jnp.where(row == first_row, 1.0, mult)
        a_scr[:, cs] = a
        u_scr[:, cs] = mult * gate_i * xc

    row8 = lax.broadcasted_iota(jnp.int32, (SUB, 512), 0)
    ncol = D_RNN // 512

    def scan_body(j, hs):
        r0 = pl.multiple_of(j * SUB, SUB)
        new = []
        for c in range(ncol):
            cs = slice(c * 512, (c + 1) * 512)
            a = a_scr[pl.ds(r0, SUB), cs]
            u = u_scr[pl.ds(r0, SUB), cs]
            for s in (1, 2, 4):
                a_sh = jnp.where(row8 >= s, pltpu.roll(a, s, 0), 1.0)
                u_sh = jnp.where(row8 >= s, pltpu.roll(u, s, 0), 0.0)
                u = a * u_sh + u
                a = a * a_sh
            h = a * hs[c] + u
            u_scr[pl.ds(r0, SUB), cs] = h
            new.append(h[SUB - 1:SUB, :])
        return tuple(new)

    h_in = tuple(h_scr[:, c * 512:(c + 1) * 512] for c in range(ncol))
    h_fin = lax.fori_loop(0, tt // SUB, scan_body, h_in)
    for c in range(ncol):
        h_scr[:, c * 512:(c + 1) * 512] = h_fin[c]

    y = u_scr[...] * _gelu(yr_ref[...])
    ms = jnp.mean(y * y, axis=-1, keepdims=True)
    out_ref[...] = (y * lax.rsqrt(ms + EPS) * gn_ref[...]).astype(out_ref.dtype)
    ext_scr[pad - tail:pad, :] = ext_scr[pad + tt - tail:pad + tt, :]

    @pl.when(t == nt - 1)
    def _():
        nconv_ref[0] = ext_scr[pad + tt - tail:pad + tt, :]
        hlast_ref[0] = h_scr[...]


def rg_long(proj, conv_buf, h0, wts, *, batch, seq, tt, reset_first):
    nt = seq // tt
    bsel = (lambda b: b) if conv_buf.shape[0] == batch else (lambda b: 0)
    vec = lambda r: pl.BlockSpec((r, D_RNN), lambda b, t: (0, 0))
    return pl.pallas_call(
        functools.partial(_rg_long_kernel, tt=tt, reset_first=reset_first),
        out_shape=(jax.ShapeDtypeStruct((batch * seq, D_RNN), BF16),
                   jax.ShapeDtypeStruct((batch, CONV_W - 1, D_RNN), F32),
                   jax.ShapeDtypeStruct((batch, 1, D_RNN), F32)),
        grid=(batch, nt),
        in_specs=[pl.BlockSpec((tt, D_RNN), lambda b, t: (b * nt + t, 0)),
                  pl.BlockSpec((tt, D_RNN), lambda b, t: (b * nt + t, 1)),
                  pl.BlockSpec((1, CONV_W - 1, D_RNN), lambda b, t: (bsel(b), 0, 0)),
                  pl.BlockSpec((1, 1, D_RNN), lambda b, t: (bsel(b), 0, 0)),
                  vec(CONV_W), vec(1),
                  pl.BlockSpec((RG_BLOCKS, RG_BLOCK, 2 * RG_BLOCK), lambda b, t: (0, 0, 0)),
                  vec(1), vec(1), vec(1), vec(1)],
        out_specs=(pl.BlockSpec((tt, D_RNN), lambda b, t: (b * nt + t, 0)),
                   pl.BlockSpec((1, CONV_W - 1, D_RNN), lambda b, t: (b, 0, 0)),
                   pl.BlockSpec((1, 1, D_RNN), lambda b, t: (b, 0, 0))),
        scratch_shapes=[pltpu.VMEM((tt + SUB, D_RNN), F32),
                        pltpu.VMEM((tt, D_RNN), F32),
                        pltpu.VMEM((tt, D_RNN), F32),
                        pltpu.VMEM((1, D_RNN), F32)],
        compiler_params=_params(("parallel", "arbitrary")),
        name="rg_long",
    )(proj, proj, conv_buf, h0, wts["rg_conv_w"], wts["rg_conv_b"], wts["rg_wax"],
      wts["rg_ba"], wts["rg_bx"], wts["rg_lambda"], wts["rg_out_norm"])


def _rg_short_kernel(xr_ref, yr_ref, cb_ref, h0_ref, cw_ref, cbias_ref, wax_ref, ba_ref, bx_ref,
                     lam_ref, gn_ref, out_ref, nconv_ref, hlast_ref, y_scr, *, nb, seq):
    sp_row = _softplus(-lam_ref[...])
    tail = CONV_W - 1
    for n in range(RG_BLOCKS):
        cs = slice(n * RG_BLOCK, (n + 1) * RG_BLOCK)
        ext = [cb_ref[i, :, cs] for i in range(tail)]
        ext += [xr_ref[t * nb:(t + 1) * nb, cs] for t in range(seq)]
        h = h0_ref[:, cs]
        for t in range(seq):
            xc = cbias_ref[:, cs]
            for i in range(CONV_W):
                xc = xc + ext[t + i] * cw_ref[i:i + 1, cs]
            a, mult, gate_i = _rg_gates(xc, n, wax_ref, ba_ref, bx_ref, sp_row)
            h = a * h + mult * gate_i * xc
            y_scr[t, :, cs] = h * _gelu(yr_ref[t * nb:(t + 1) * nb, cs])
        hlast_ref[:, cs] = h
        for i in range(tail):
            nconv_ref[i, :, cs] = ext[seq + i]
    for t in range(seq):
        y = y_scr[t]
        ms = jnp.mean(y * y, axis=-1, keepdims=True)
        out_ref[t * nb:(t + 1) * nb, :] = (y * lax.rsqrt(ms + EPS) * gn_ref[...]).astype(out_ref.dtype)


def rg_short(proj, conv_buf_t, h0, wts, *, nb, seq):
    rows = nb * seq
    vec = lambda r: pl.BlockSpec((r, D_RNN), lambda i: (0, 0))
    return pl.pallas_call(
        functools.partial(_rg_short_kernel, nb=nb, seq=seq),
        out_shape=(jax.ShapeDtypeStruct((rows, D_RNN), BF16),
                   jax.ShapeDtypeStruct((CONV_W - 1, nb, D_RNN), F32),
                   jax.ShapeDtypeStruct((nb, D_RNN), F32)),
        grid=(1,),
        in_specs=[pl.BlockSpec((rows, D_RNN), lambda i: (0, 0)),
                  pl.BlockSpec((rows, D_RNN), lambda i: (0, 1)),
                  pl.BlockSpec((CONV_W - 1, nb, D_RNN), lambda i: (0, 0, 0)),
                  pl.BlockSpec((nb, D_RNN), lambda i: (0, 0)),
                  vec(CONV_W), vec(1),
                  pl.BlockSpec((RG_BLOCKS, RG_BLOCK, 2 * RG_BLOCK), lambda i: (0, 0, 0)),
                  vec(1), vec(1), vec(1), vec(1)],
        out_specs=(pl.BlockSpec((rows, D_RNN), lambda i: (0, 0)),
                   pl.BlockSpec((CONV_W - 1, nb, D_RNN), lambda i: (0, 0, 0)),
                   pl.BlockSpec((nb, D_RNN), lambda i: (0, 0))),
        scratch_shapes=[pltpu.VMEM((seq, nb, D_RNN), F32)],
        compiler_params=_params(("arbitrary",)),
        name="rg_short",
    )(proj, proj, conv_buf_t, h0, wts["rg_conv_w"], wts["rg_conv_b"], wts["rg_wax"],
      wts["rg_ba"], wts["rg_bx"], wts["rg_lambda"], wts["rg_out_norm"])


def _gla_consts():
    er = lax.broadcasted_iota(jnp.int32, (LANES, LANES), 0)
    ec = lax.broadcasted_iota(jnp.int32, (LANES, LANES), 1)
    eye = jnp.where(er == ec, 1.0, 0.0).astype(BF16)
    ones = jnp.ones((LANES, LANES), BF16)
    row = lax.broadcasted_iota(jnp.int32, (GLA_SUB, LANES), 0)
    row8 = lax.broadcasted_iota(jnp.int32, (SUB, LANES), 0)
    pr = lax.broadcasted_iota(jnp.int32, (2 * GLA_SUB, LANES), 0)
    pick = jnp.where((pr == GLA_SUB) | (pr == GLA_SUB + 1), 1.0, 0.0).astype(BF16)
    return eye, ones, row, row8, pick


def _gla_local(units, consts):
    eye, ones, row, row8, pick = consts
    assert len(units) == GLA_G

    def scan8(x):
        for s in (1, 2, 4):
            x = x + jnp.where(row8 >= s, pltpu.roll(x, s, 0), 0.0)
        return x

    halves = (slice(0, SUB), slice(SUB, GLA_SUB))
    nt = (((1,), (1,)), ((), ()))
    qes, xs, qas, kbs, cum_all = [], [], [], [], []
    for q, k, v, g, j_range in units:
        cum_lo = scan8(g[halves[0]])
        edge = cum_lo[SUB - 1:SUB, :]
        cum_hi = scan8(g[halves[1]]) + edge
        cums = (cum_lo, cum_hi)
        cum_all.append(cums)
        last = cum_hi[SUB - 1:SUB, :]
        qes.append(jnp.concatenate([q[h] * jnp.exp(c) for h, c in zip(halves, cums)],
                                   axis=0).astype(BF16))
        qas.append(q[halves[1]] * jnp.exp(cum_hi - edge))
        kbs.append(k[halves[0]] * jnp.exp(edge - cum_lo))
        ke = jnp.concatenate([k[h] * jnp.exp(last - c) for h, c in zip(halves, cums)], axis=0)
        el = jnp.exp(last)
        e_hi = el.astype(BF16).astype(F32)
        e_lo = el - e_hi
        extra = jnp.where(row == 0, e_hi, jnp.where(row == 1, e_lo, 0.0))
        xs.append(jnp.concatenate([ke, extra], axis=0).astype(BF16))
    crossing = [u[4][0] < SUB <= u[4][-1] for u in units]
    sc = None
    if any(crossing):
        qa = jnp.concatenate(qas, axis=0).astype(BF16)
        kb = jnp.concatenate(kbs, axis=0).astype(BF16)
        sc = lax.dot_general(qa, kb, nt, preferred_element_type=F32)
    xt = lax.dot_general(eye, jnp.concatenate(xs, axis=0), nt, preferred_element_type=F32)
    o_halves = []
    for (q, k, v, g, j_range), cums in zip(units, cum_all):
        ps = []
        for j in j_range:
            h, c, jj = halves[j // SUB], cums[j // SUB], j % SUB
            w = jnp.exp(jnp.where(row8 >= jj, c - c[jj:jj + 1, :], -jnp.inf))
            ps.append(q[h] * w * k[j:j + 1, :])
        att = jnp.dot(jnp.concatenate(ps, axis=0).astype(BF16), ones, preferred_element_type=F32)
        o_half = [None, None]
        for idx, j in enumerate(j_range):
            term = att[idx * SUB:(idx + 1) * SUB, :] * v[j:j + 1, :]
            o_half[j // SUB] = term if o_half[j // SUB] is None else o_half[j // SUB] + term
        o_halves.append(o_half)
    cross = None
    if sc is not None:
        n = GLA_G * SUB
        ur = lax.broadcasted_iota(jnp.int32, (n, n), 0) // SUB
        uc = lax.broadcasted_iota(jnp.int32, (n, n), 1) // SUB
        sc = jnp.where(ur == uc, sc, 0.0).astype(BF16)
        v_lo = jnp.concatenate([u[2][halves[0]] for u in units], axis=0).astype(BF16)
        cross = jnp.dot(sc, v_lo, preferred_element_type=F32)
    xt = xt.astype(BF16)
    zblk = jnp.zeros((2 * GLA_SUB, LANES), BF16)
    wrows = []
    for u, (_, _, v, _, _) in enumerate(units):
        vpad = jnp.concatenate([v, jnp.zeros_like(v)], axis=0).astype(BF16)
        wrows.append(jnp.concatenate([zblk] * (2 * u) + [vpad, pick]
                                     + [zblk] * (2 * (GLA_G - 1 - u)), axis=1))
    kd = jnp.dot(xt, jnp.concatenate(wrows, axis=0), preferred_element_type=F32)
    out = []
    for u, o_half in enumerate(o_halves):
        if cross is not None and crossing[u]:
            o_half[1] = o_half[1] + cross[u * SUB:(u + 1) * SUB, :]
        o = jnp.concatenate([jnp.zeros((SUB, LANES), F32) if t is None else t for t in o_half],
                            axis=0)
        kv = kd[:, (2 * u) * LANES:(2 * u + 1) * LANES]
        dec = kd[:, (2 * u + 1) * LANES:(2 * u + 2) * LANES]
        out.append((qes[u], o, kv, dec))
    return out


def _gla_log_alpha(z_ref, wa2_ref, ba2_ref):
    zz = jnp.dot(z_ref[...].astype(BF16), wa2_ref[...], preferred_element_type=F32) + ba2_ref[...]
    return (jnp.minimum(zz, 0.0) - jnp.log(1.0 + jnp.exp(-jnp.abs(zz)))) * (1.0 / GLA_GATE_TAU)


def _gla_finish(o, go, gain):
    ms = jnp.mean(o * o, axis=-1, keepdims=True)
    on = o * lax.rsqrt(ms + EPS) * gain
    return on * (go * _sigmoid(go))


def _gla_long_kernel(q_ref, k_ref, v_ref, go_ref, z_ref, wa2_ref, ba2_ref, gain_ref, s0_ref,
                     *rest, tt, n_casts):
    cast_in = rest[:n_casts]
    out_ref, sout_ref = rest[n_casts:n_casts + 2]
    cast_out = rest[n_casts + 2:2 * n_casts + 2]
    s_scr, gl_scr, o_scr, qe_scr, kv_scr, dec_scr = rest[2 * n_casts + 2:]
    t = pl.program_id(2)
    nt = pl.num_programs(2)
    nsub = tt // GLA_SUB
    consts = _gla_consts()
    scale = GLA_DK ** -0.5

    for src, dst in zip(cast_in, cast_out):
        dst[...] = src[...].astype(BF16)

    @pl.when(t == 0)
    def _():
        s_scr[...] = s0_ref[0]

    gl_scr[...] = _gla_log_alpha(z_ref, wa2_ref, ba2_ref)

    def local_body(i, carry):
        rows = pl.ds(pl.multiple_of(i * GLA_SUB, GLA_SUB), GLA_SUB)
        heads = [slice(h * LANES, (h + 1) * LANES) for h in range(GLA_G)]
        units = [(q_ref[rows, cs] * scale, k_ref[rows, cs], v_ref[rows, cs], gl_scr[rows, cs],
                  range(GLA_SUB)) for cs in heads]
        for h, (qe, o, kv, dec) in enumerate(_gla_local(units, consts)):
            cs = heads[h]
            qe_scr[rows, cs] = qe
            o_scr[rows, cs] = o
            kv_scr[i, h] = kv
            dec_scr[i, h] = dec
        return carry

    lax.fori_loop(0, nsub, local_body, 0, unroll=min(4, nsub))

    def state_body(i, carry):
        rows = pl.ds(pl.multiple_of(i * GLA_SUB, GLA_SUB), GLA_SUB)
        for h in range(GLA_G):
            cs = slice(h * LANES, (h + 1) * LANES)
            s = s_scr[h]
            o_scr[rows, cs] += jnp.dot(qe_scr[rows, cs], s.astype(BF16),
                                       preferred_element_type=F32)
            s_scr[h] = dec_scr[i, h] * s + kv_scr[i, h]
        return carry

    lax.fori_loop(0, nsub, state_body, 0, unroll=min(8, nsub))
    for h in range(GLA_G):
        cs = slice(h * LANES, (h + 1) * LANES)
        out_ref[:, cs] = _gla_finish(o_scr[:, cs], go_ref[:, cs], gain_ref[:, cs]).astype(out_ref.dtype)

    @pl.when(t == nt - 1)
    def _():
        sout_ref[0] = s_scr[...]


def gla_long(proj, z, s0, wts, *, batch, seq, tt, casts=()):
    nt = seq // tt
    gw = GLA_G * LANES
    base = 2 * D_RNN // gw
    per = D_GLA // gw
    nhg = GLA_HEADS // GLA_G
    bsel = (lambda b: b) if s0.shape[0] == batch else (lambda b: 0)
    col = lambda which: pl.BlockSpec((tt, gw), lambda b, h, t: (b * nt + t, base + which * per + h))
    n_steps = batch * nhg * nt
    cast_specs = []
    for a in casts:
        rblocks = math.gcd(n_steps, a.shape[0] // 16)
        cblocks = n_steps // rblocks
        blk = (a.shape[0] // rblocks, a.shape[1] // cblocks)
        assert blk[0] * rblocks == a.shape[0] and blk[1] * cblocks == a.shape[1]
        assert blk[0] % 16 == 0 and blk[1] % LANES == 0
        cast_specs.append(pl.BlockSpec(
            blk, lambda b, h, t, cblocks=cblocks:
                (((b * nhg + h) * nt + t) // cblocks, ((b * nhg + h) * nt + t) % cblocks)))
    return pl.pallas_call(
        functools.partial(_gla_long_kernel, tt=tt, n_casts=len(casts)),
        out_shape=(jax.ShapeDtypeStruct((batch * seq, D_GLA), BF16),
                   jax.ShapeDtypeStruct((batch, GLA_HEADS, GLA_DK, GLA_DK), F32),
                   *[jax.ShapeDtypeStruct(a.shape, BF16) for a in casts]),
        grid=(batch, nhg, nt),
        in_specs=[col(0), col(1), col(2), col(3),
                  pl.BlockSpec((tt, LANES), lambda b, h, t: (b * nt + t, 0)),
                  pl.BlockSpec((LANES, gw), lambda b, h, t: (0, h)),
                  pl.BlockSpec((1, gw), lambda b, h, t: (0, h)),
                  pl.BlockSpec((1, gw), lambda b, h, t: (0, h)),
                  pl.BlockSpec((1, GLA_G, GLA_DK, GLA_DK), lambda b, h, t: (bsel(b), h, 0, 0)),
                  *cast_specs],
        out_specs=(pl.BlockSpec((tt, gw), lambda b, h, t: (b * nt + t, h)),
                   pl.BlockSpec((1, GLA_G, GLA_DK, GLA_DK), lambda b, h, t: (b, h, 0, 0)),
                   *cast_specs),
        scratch_shapes=[pltpu.VMEM((GLA_G, GLA_DK, GLA_DK), F32),
                        pltpu.VMEM((tt, gw), F32),
                        pltpu.VMEM((tt, gw), F32),
                        pltpu.VMEM((tt, gw), BF16),
                        pltpu.VMEM((tt // GLA_SUB, GLA_G, GLA_DK, GLA_DK), F32),
                        pltpu.VMEM((tt // GLA_SUB, GLA_G, GLA_DK, GLA_DK), F32)],
        compiler_params=_params(("parallel", "parallel", "arbitrary")),
        name="gla_long",
    )(proj, proj, proj, proj, z, wts["gla_wa2"], wts["gla_ba2"], wts["gla_head_norm"], s0, *casts)


def _gla_short_kernel(q_ref, k_ref, v_ref, go_ref, z_ref, wa2_ref, ba2_ref, gain_ref, s0_ref,
                      out_ref, sout_ref, *, seq):
    consts = _gla_consts()
    row = consts[2]
    scale = GLA_DK ** -0.5
    gl = _gla_log_alpha(z_ref, wa2_ref, ba2_ref)
    nseq = GLA_SUB // seq
    assert nseq == GLA_G
    local = {}
    for h in range(GLA_G):
        cs = slice(h * LANES, (h + 1) * LANES)
        q = q_ref[:, cs] * scale
        k = k_ref[:, cs]
        v = v_ref[:, cs]
        g = gl[:, cs]
        units = []
        for b in range(nseq):
            mine = (row >= b * seq) & (row < (b + 1) * seq)
            zero = lambda a, mine=mine: jnp.where(mine, a, 0.0)
            units.append((zero(q), zero(k), zero(v), zero(g), range(b * seq, (b + 1) * seq)))
        for b, res in enumerate(_gla_local(units, consts)):
            local[b, h] = res
    for h in range(GLA_G):
        cs = slice(h * LANES, (h + 1) * LANES)
        o = jnp.zeros((GLA_SUB, LANES), F32)
        for b in range(nseq):
            qe, o_b, kv, dec = local[b, h]
            s = s0_ref[b, h]
            o = o + o_b + jnp.dot(qe, s.astype(BF16), preferred_element_type=F32)
            sout_ref[b, h] = dec * s + kv
        out_ref[:, cs] = _gla_finish(o, go_ref[:, cs], gain_ref[:, cs]).astype(out_ref.dtype)


def gla_short(proj, z, s0, wts, *, nb, seq):
    gw = GLA_G * LANES
    per = D_GLA // gw
    bpb = GLA_SUB // seq
    col = lambda which: pl.BlockSpec((GLA_SUB, gw), lambda i, h: (i, which * per + h))
    return pl.pallas_call(
        functools.partial(_gla_short_kernel, seq=seq),
        out_shape=(jax.ShapeDtypeStruct((nb * seq, D_GLA), BF16),
                   jax.ShapeDtypeStruct((nb, GLA_HEADS, GLA_DK, GLA_DK), F32)),
        grid=(nb // bpb, GLA_HEADS // GLA_G),
        in_specs=[col(0), col(1), col(2), col(3),
                  pl.BlockSpec((GLA_SUB, LANES), lambda i, h: (i, 0)),
                  pl.BlockSpec((LANES, gw), lambda i, h: (0, h)),
                  pl.BlockSpec((1, gw), lambda i, h: (0, h)),
                  pl.BlockSpec((1, gw), lambda i, h: (0, h)),
                  pl.BlockSpec((bpb, GLA_G, GLA_DK, GLA_DK), lambda i, h: (i, h, 0, 0))],
        out_specs=(pl.BlockSpec((GLA_SUB, gw), lambda i, h: (i, h)),
                   pl.BlockSpec((bpb, GLA_G, GLA_DK, GLA_DK), lambda i, h: (i, h, 0, 0))),
        compiler_params=_params(("parallel", "parallel")),
        name="gla_short",
    )(proj, proj, proj, proj, z, wts["gla_wa2"], wts["gla_ba2"], wts["gla_head_norm"], s0)


def _ffn_kernel(xm_ref, xs_ref, xp_ref, wg_ref, wu_ref, cw_ref, cb_ref, bufs_ref,
                hp_ref, hs_ref, nbufp_ref, nbufs_ref, extp_scr, exts_scr, mhist_scr,
                *, steps, tiles_per_seq, nb):
    i = pl.program_id(1)
    (m_first, _), (s_first, _), (p_first, p_count) = steps
    tail = FFN_CONV_W - 1
    pad = SUB
    tm = xp_ref.shape[0]
    rows_s = xs_ref.shape[0]
    hist_s = tail * nb

    def conv(ext_ref, start, shift, rows):
        gc = cb_ref[...]
        for c in range(FFN_CONV_W):
            lo = start - (tail - c) * shift
            gc = gc + ext_ref[lo:lo + rows, :] * cw_ref[c:c + 1, :]
        return gc

    @pl.when(i == m_first)
    def _():
        g = jnp.dot(xm_ref[...], wg_ref[...], preferred_element_type=F32)
        mhist_scr[...] = g[g.shape[0] - tail:, :]

    @pl.when((i >= p_first) & (i < p_first + p_count))
    def _():
        @pl.when((i - p_first) % tiles_per_seq == 0)
        def _():
            extp_scr[pad - tail:pad, :] = mhist_scr[...]

        x = xp_ref[...]
        extp_scr[pad:pad + tm, :] = jnp.dot(x, wg_ref[...], preferred_element_type=F32)
        up = jnp.dot(x, wu_ref[...], preferred_element_type=F32)
        hp_ref[...] = (_gelu(conv(extp_scr, pad, 1, tm)) * up).astype(hp_ref.dtype)
        last = extp_scr[pad + tm - tail:pad + tm, :]
        extp_scr[pad - tail:pad, :] = last
        nbufp_ref[0] = last

    @pl.when(i == s_first)
    def _():
        x = xs_ref[...]
        exts_scr[0:hist_s, :] = bufs_ref[...]
        exts_scr[hist_s:hist_s + rows_s, :] = jnp.dot(x, wg_ref[...], preferred_element_type=F32)
        up = jnp.dot(x, wu_ref[...], preferred_element_type=F32)
        hs_ref[...] = (_gelu(conv(exts_scr, hist_s, nb, rows_s)) * up).astype(hs_ref.dtype)
        nbufs_ref[...] = exts_scr[rows_s:rows_s + hist_s, :]


def ffn_gate(xm, xs, xp, bufs, wts, *, tm, tn, seq_rows, nb):
    rows_p, rows_s = xp.shape[0], xs.shape[0]
    tail = FFN_CONV_W - 1
    steps, ni = _group_steps([(xm.shape[0], xm.shape[0]), (rows_s, rows_s), (rows_p, tm)])
    (_, _), (_, _), (p_first, p_count) = steps
    tps = seq_rows // tm
    ptile = lambda i: _tile_index(i, p_first, p_count)
    const = lambda shape: pl.BlockSpec(shape, lambda j, i: (0, 0), pipeline_mode=pl.Buffered(1))
    colblk = lambda r: pl.BlockSpec((r, tn), lambda j, i: (0, j))
    return pl.pallas_call(
        functools.partial(_ffn_kernel, steps=tuple(steps), tiles_per_seq=tps, nb=nb),
        out_shape=(jax.ShapeDtypeStruct((rows_p, D_FF), BF16),
                   jax.ShapeDtypeStruct((rows_s, D_FF), BF16),
                   jax.ShapeDtypeStruct((rows_p // seq_rows, tail, D_FF), F32),
                   jax.ShapeDtypeStruct((tail * nb, D_FF), F32)),
        grid=(D_FF // tn, ni),
        in_specs=[const(xm.shape),
                  const(xs.shape),
                  pl.BlockSpec((tm, D_MODEL), lambda j, i: (ptile(i), 0)),
                  colblk(D_MODEL), colblk(D_MODEL), colblk(FFN_CONV_W), colblk(1),
                  colblk(tail * nb)],
        out_specs=(pl.BlockSpec((tm, tn), lambda j, i: (ptile(i), j)),
                   colblk(rows_s),
                   pl.BlockSpec((1, tail, tn), lambda j, i: (ptile(i) // tps, 0, j)),
                   colblk(tail * nb)),
        scratch_shapes=[pltpu.VMEM((tm + SUB, tn), F32),
                        pltpu.VMEM((rows_s + tail * nb, tn), F32),
                        pltpu.VMEM((tail, tn), F32)],
        compiler_params=_params(("parallel", "arbitrary")),
        name="ffn_gate",
    )(xm, xs, xp, wts["w_gate"], wts["w_up"], wts["ffn_conv_w"], wts["ffn_conv_b"], bufs)


PROMPT_TM = 1024
DOWN_TM = 512
DOWN_TN = 512
FFN_TM = 1024
FFN_TN = 512
SEQ_TT = 256
NORM_TM = 256


def kernel(x_prompt, x_sample, state_rglru_conv, state_rglru_h, state_gla_S, state_ffn_conv,
           meta_tokens, norm_mix, w_in, rg_conv_w, rg_conv_b, rg_wa, rg_ba, rg_wx, rg_bx, rg_lambda,
           rg_out_norm, gla_wa2, gla_ba2, gla_head_norm, w_out, norm_ffn, w_gate, w_up,
           ffn_conv_w, ffn_conv_b, w_down, final_norm):
    batch, seq, _ = x_prompt.shape
    nb, sseq, _ = x_sample.shape
    row = lambda a: a.reshape(1, -1)
    w_in_b = w_in[0].astype(BF16)
    wts = {
        "w_in": w_in_b,
        "w_z": jnp.pad(w_in_b[:, PROJ_COLS:], ((0, 0), (0, LANES - GLA_RANK))),
        "rg_conv_w": rg_conv_w[0], "rg_conv_b": row(rg_conv_b[0]),
        "rg_wax": jnp.concatenate([rg_wa[0], rg_wx[0]], axis=-1).astype(BF16),
        "rg_ba": row(rg_ba[0]), "rg_bx": row(rg_bx[0]), "rg_lambda": row(rg_lambda[0]),
        "rg_out_norm": row(rg_out_norm[0]),
        "gla_wa2": jnp.pad(gla_wa2[0].astype(BF16), ((0, LANES - GLA_RANK), (0, 0))),
        "gla_ba2": row(gla_ba2[0]), "gla_head_norm": row(gla_head_norm[0]),
        "ffn_conv_w": ffn_conv_w[0], "ffn_conv_b": row(ffn_conv_b[0]),
    }

    def to_batch_major(a):
        return jnp.swapaxes(a.reshape(sseq, nb, -1), 0, 1).reshape(nb * sseq, -1)

    def to_time_major(a):
        return jnp.swapaxes(a.reshape(nb, sseq, -1), 0, 1).reshape(nb * sseq, -1)

    rows_p, rows_s = batch * seq, nb * sseq
    xs = [meta_tokens, to_time_major(x_sample), x_prompt.reshape(rows_p, D_MODEL)]
    tms = [N_META, rows_s, PROMPT_TM]
    norm_tms = [N_META, NORM_TM, NORM_TM]

    def norm_all(arrs, g, dtype):
        return [rmsnorm(a, g, dtype, t) for a, t in zip(arrs, norm_tms)]

    xn = norm_all(xs, norm_mix[0], BF16)
    proj_m, proj_s, proj_p = grouped_matmul(
        [dict(parts=[a], tm=t) for a, t in zip(xn, tms)], wts["w_in"], tn=1024, n_cols=PROJ_COLS)
    z_m, z_s, z_p = [matmul(a, wts["w_z"], tm=t, tn=LANES, tk=D_MODEL) for a, t in zip(xn, tms)]

    zeros = lambda *s: jnp.zeros(s, F32)
    rnn_m, m_conv, m_h = rg_long(proj_m, zeros(1, CONV_W - 1, D_RNN), zeros(1, 1, D_RNN), wts,
                                 batch=1, seq=N_META, tt=N_META, reset_first=True)
    gla_m, m_s = gla_long(proj_m, z_m, zeros(1, GLA_HEADS, GLA_DK, GLA_DK), wts, batch=1,
                          seq=N_META, tt=N_META)
    rnn_p, p_conv, p_h = rg_long(proj_p, m_conv, m_h, wts, batch=batch, seq=seq, tt=SEQ_TT,
                                 reset_first=False)
    gla_p, p_s, wts["w_gate"], wts["w_up"], wts["w_down"] = gla_long(
        proj_p, z_p, m_s, wts, batch=batch, seq=seq, tt=SEQ_TT,
        casts=(w_gate.reshape(D_MODEL, D_FF), w_up.reshape(D_MODEL, D_FF),
               w_down.reshape(D_FF, D_MODEL)))
    rnn_s, s_conv_t, s_h = rg_short(proj_s, jnp.swapaxes(state_rglru_conv[0], 0, 1),
                                    state_rglru_h[0], wts, nb=nb, seq=sseq)
    gla_sb, s_s = gla_short(to_batch_major(proj_s[:, 2 * D_RNN:]), to_batch_major(z_s),
                            state_gla_S[0], wts, nb=nb, seq=sseq)
    gla_s = to_time_major(gla_sb)

    x1 = grouped_matmul(
        [dict(parts=[r, g], tm=t, res=x)
         for r, g, t, x in zip([rnn_m, rnn_s, rnn_p], [gla_m, gla_s, gla_p], tms, xs)],
        w_out, tn=512, n_cols=D_MODEL, cast_w=True)
    xn2 = norm_all(x1, norm_ffn[0], BF16)
    tail = FFN_CONV_W - 1
    bufs = jnp.swapaxes(state_ffn_conv[0], 0, 1).reshape(tail * nb, D_FF)
    hid_p, hid_s, p_ffn, s_ffn_t = ffn_gate(xn2[0], xn2[1], xn2[2], bufs, wts,
                                            tm=FFN_TM, tn=FFN_TN, seq_rows=seq, nb=nb)
    (x2_s,) = grouped_matmul([dict(parts=[hid_s], tm=rows_s, res=x1[1])], wts["w_down"],
                             tn=DOWN_TN, n_cols=D_MODEL)
    (x2_p,) = grouped_matmul([dict(parts=[hid_p], tm=DOWN_TM, res=x1[2])], wts["w_down"],
                             tn=DOWN_TN, n_cols=D_MODEL)
    y_p = rmsnorm(x2_p, final_norm, F32, NORM_TM)
    y_s = to_batch_major(rmsnorm(x2_s, final_norm, F32, NORM_TM))

    return (y_p.reshape(batch, seq, D_MODEL), y_s.reshape(nb, sseq, D_MODEL),
            p_conv[None], p_h.reshape(1, batch, D_RNN), p_s[None], p_ffn[None],
            jnp.swapaxes(s_conv_t, 0, 1)[None], s_h[None], s_s[None],
            jnp.swapaxes(s_ffn_t.reshape(tail, nb, D_FF), 0, 1)[None])
```

```python
import functools
import math

import jax
import jax.numpy as jnp
from jax import lax
from jax.experimental import pallas as pl
from jax.experimental.pallas import tpu as pltpu

F32 = jnp.float32
BF16 = jnp.bfloat16

D_MODEL = 4096
N_META = 16
D_RNN = 2048
D_GLA = 2048
RG_BLOCKS = 16
RG_BLOCK = 128
CONV_W = 4
RG_C = 8.0
GLA_HEADS = 16
GLA_DK = 128
GLA_RANK = 16
GLA_GATE_TAU = 16.0
D_FF = 3 * D_MODEL
FFN_CONV_W = 3
EPS = 1e-6
PROJ_COLS = 2 * D_RNN + 4 * D_GLA

LANES = 128
SUB = 8
VMEM_LIMIT = 60 * 1024 * 1024
GLA_SUB = 16
GLA_G = 4
LOCAL_PAIR = 2
GLA_SHORT_BLOCKS = 4


def _params(sem):
    return pltpu.CompilerParams(dimension_semantics=sem, vmem_limit_bytes=VMEM_LIMIT)


def _gelu(x):
    c = math.sqrt(2.0 / math.pi)
    return 0.5 * x * (1.0 + jnp.tanh(c * (x + 0.044715 * (x * x * x))))


def _softplus(x):
    return jnp.maximum(x, 0.0) + jnp.log1p(jnp.exp(-jnp.abs(x)))


def _sigmoid(x):
    return 0.5 * (1.0 + jnp.tanh(0.5 * x))


def _rmsnorm_kernel(x_ref, g_ref, o_ref):
    x = x_ref[...]
    ms = jnp.mean(x * x, axis=-1, keepdims=True)
    o_ref[...] = (x * lax.rsqrt(ms + EPS) * g_ref[...]).astype(o_ref.dtype)


def rmsnorm(x, g, out_dtype, tm):
    m, d = x.shape
    return pl.pallas_call(
        _rmsnorm_kernel,
        out_shape=jax.ShapeDtypeStruct((m, d), out_dtype),
        grid=(m // tm,),
        in_specs=[pl.BlockSpec((tm, d), lambda i: (i, 0)),
                  pl.BlockSpec((1, d), lambda i: (0, 0))],
        out_specs=pl.BlockSpec((tm, d), lambda i: (i, 0)),
        compiler_params=_params(("parallel",)),
        name="rmsnorm",
    )(x, g.reshape(1, d))


def _mm_kernel(x_ref, w_ref, *rest, nk, has_res):
    if has_res:
        r_ref, o_ref = rest
    else:
        (o_ref,) = rest
    part = jnp.dot(x_ref[...], w_ref[...], preferred_element_type=F32)
    if nk == 1:
        o_ref[...] = part + r_ref[...] if has_res else part
        return
    k = pl.program_id(2)

    @pl.when(k == 0)
    def _():
        o_ref[...] = part + r_ref[...] if has_res else part

    @pl.when(k > 0)
    def _():
        o_ref[...] += part


def matmul(x, w, *, tm, tn, tk, n_cols=None, res=None):
    m, kdim = x.shape
    n = w.shape[1] if n_cols is None else n_cols
    nk = kdim // tk
    in_specs = [pl.BlockSpec((tm, tk), lambda j, i, k: (i, k)),
                pl.BlockSpec((tk, tn), lambda j, i, k: (k, j))]
    args = [x, w]
    if res is not None:
        in_specs.append(pl.BlockSpec((tm, tn), lambda j, i, k: (i, j)))
        args.append(res)
    return pl.pallas_call(
        functools.partial(_mm_kernel, nk=nk, has_res=res is not None),
        out_shape=jax.ShapeDtypeStruct((m, n), F32),
        grid=(n // tn, m // tm, nk),
        in_specs=in_specs,
        out_specs=pl.BlockSpec((tm, tn), lambda j, i, k: (i, j)),
        compiler_params=_params(("parallel", "parallel", "arbitrary")),
        name="matmul",
    )(*args)


def _group_steps(groups):
    steps, first = [], 0
    for rows, tm in groups:
        steps.append((first, rows // tm))
        first += rows // tm
    return steps, first


def _tile_index(i, first, count):
    return jnp.clip(i - first, 0, count - 1)


def _gmm_kernel(*refs, layout, steps, nk, cast_w):
    i = pl.program_id(1)
    pos = 0
    g_in = []
    for n_parts, has_res in layout:
        xs = refs[pos:pos + n_parts]
        pos += n_parts
        r = refs[pos] if has_res else None
        pos += int(has_res)
        g_in.append((xs, r))
    w_ref = refs[pos]
    outs = refs[pos + 1:pos + 1 + len(layout)]
    if cast_w:
        wsrc = refs[pos + 1 + len(layout)]

        @pl.when(i == 0)
        def _():
            wsrc[...] = w_ref[...].astype(BF16)
    else:
        wsrc = w_ref
    for (xs, r), o_ref, (first, count) in zip(g_in, outs, steps):
        @pl.when((i >= first) & (i < first + count))
        def _(xs=xs, r=r, o_ref=o_ref):
            acc, off = None, 0
            for x_ref in xs:
                kk = x_ref.shape[1]
                part = jnp.dot(x_ref[...], wsrc[off:off + kk, :], preferred_element_type=F32)
                acc = part if acc is None else acc + part
                off += kk
            if r is not None:
                first_val = acc + r[...]
            else:
                first_val = acc
            if nk == 1:
                o_ref[...] = first_val
            else:
                k = pl.program_id(2)

                @pl.when(k == 0)
                def _():
                    o_ref[...] = first_val

                @pl.when(k > 0)
                def _():
                    o_ref[...] += acc


def grouped_matmul(groups, w, *, tn, n_cols, tk=None, cast_w=False):
    kdim = sum(p.shape[1] for p in groups[0]["parts"])
    nk = 1 if tk is None else kdim // tk
    assert not (cast_w and nk > 1)
    steps, ni = _group_steps([(g["parts"][0].shape[0], g["tm"]) for g in groups])
    in_specs, args, layout = [], [], []
    for g, (first, count) in zip(groups, steps):
        assert nk == 1 or len(g["parts"]) == 1
        last_k = nk - 1

        def kidx(i, k, first=first, count=count):
            if nk == 1:
                return 0
            return jnp.where(i < first, 0, jnp.where(i < first + count, k, last_k))
        once = dict(pipeline_mode=pl.Buffered(1)) if (count == 1 and nk == 1) else {}
        for p in g["parts"]:
            kp = p.shape[1] if nk == 1 else tk
            in_specs.append(pl.BlockSpec(
                (g["tm"], kp),
                lambda j, i, k, first=first, count=count, kidx=kidx:
                    (_tile_index(i, first, count), kidx(i, k)), **once))
            args.append(p)
        if g.get("res") is not None:
            in_specs.append(pl.BlockSpec(
                (g["tm"], tn),
                lambda j, i, k, first=first, count=count: (_tile_index(i, first, count), j)))
            args.append(g["res"])
        layout.append((len(g["parts"]), g.get("res") is not None))
    kblk = kdim if nk == 1 else tk
    scratch = [pltpu.VMEM((kblk, tn), BF16)] if cast_w else []
    if w.ndim == 3:
        in_specs.append(pl.BlockSpec((None, kblk, tn), lambda j, i, k: (0, k, j)))
    else:
        in_specs.append(pl.BlockSpec((kblk, tn), lambda j, i, k: (k, j)))
    args.append(w)
    out_specs = [pl.BlockSpec((g["tm"], tn),
                              lambda j, i, k, first=first, count=count:
                                  (_tile_index(i, first, count), j))
                 for g, (first, count) in zip(groups, steps)]
    return pl.pallas_call(
        functools.partial(_gmm_kernel, layout=tuple(layout), steps=tuple(steps), nk=nk,
                          cast_w=cast_w),
        out_shape=[jax.ShapeDtypeStruct((g["parts"][0].shape[0], n_cols), F32) for g in groups],
        grid=(n_cols // tn, ni, nk),
        in_specs=in_specs,
        out_specs=out_specs,
        scratch_shapes=scratch,
        compiler_params=_params(("parallel", "arbitrary", "arbitrary")),
        name="grouped_matmul",
    )(*args)


def _rg_gates(xc, n, wax_ref, ba_ref, bx_ref, sp_row):
    cs = slice(n * RG_BLOCK, (n + 1) * RG_BLOCK)
    pre = jnp.dot(xc.astype(BF16), wax_ref[n], preferred_element_type=F32)
    r = _sigmoid(pre[:, :RG_BLOCK] + ba_ref[:, cs])
    i = _sigmoid(pre[:, RG_BLOCK:] + bx_ref[:, cs])
    log_a = -RG_C * r * sp_row[:, cs]
    a = jnp.exp(log_a)
    th = jnp.tanh(log_a)
    p = -2.0 * th
    mult = jnp.where(p > 0.0, p * lax.rsqrt(p * (1.0 - th)), 0.0)
    return a, mult, i


def _rg_long_kernel(xr_ref, yr_ref, cb_ref, h0_ref, cw_ref, cbias_ref, wax_ref, ba_ref, bx_ref,
                    lam_ref, gn_ref, out_ref, nconv_ref, hlast_ref,
                    ext_scr, a_scr, u_scr, h_scr, *, tt, reset_first):
    t = pl.program_id(1)
    nt = pl.num_programs(1)
    pad = SUB
    tail = CONV_W - 1

    @pl.when(t == 0)
    def _():
        ext_scr[0:pad - tail, :] = jnp.zeros((pad - tail, D_RNN), F32)
        ext_scr[pad - tail:pad, :] = cb_ref[0]
        h_scr[...] = h0_ref[0]

    ext_scr[pad:pad + tt, :] = xr_ref[...]
    sp_row = _softplus(-lam_ref[...])
    row = lax.broadcasted_iota(jnp.int32, (tt, RG_BLOCK), 0)
    first_row = jnp.where(t == 0, 0, -1)
    for n in range(RG_BLOCKS):
        cs = slice(n * RG_BLOCK, (n + 1) * RG_BLOCK)
        x = ext_scr[pad:pad + tt, cs]
        hist = ext_scr[0:pad, cs]
        xc = cbias_ref[:, cs]
        for i in range(CONV_W):
            s = tail - i
            if s == 0:
                xs = x
            else:
                xs = pltpu.roll(x, s, 0)
                head = jnp.where(row[:SUB] < s, pltpu.roll(hist, s, 0), xs[:SUB])
                xs = jnp.concatenate([head, xs[SUB:]], axis=0) if tt > SUB else head
            xc = xc + xs * cw_ref[i:i + 1, cs]
        a, mult, gate_i = _rg_gates(xc, n, wax_ref, ba_ref, bx_ref, sp_row)
        if reset_first:
            mult = jnp.where(row == first_row, 1.0, mult)
        a_scr[:, cs] = a
        u_scr[:, cs] = mult * gate_i * xc

    row8 = lax.broadcasted_iota(jnp.int32, (SUB, 512), 0)
    ncol = D_RNN // 512

    def scan_body(j, hs):
        r0 = pl.multiple_of(j * SUB, SUB)
        new = []
        for c in range(ncol):
            cs = slice(c * 512, (c + 1) * 512)
            a = a_scr[pl.ds(r0, SUB), cs]
            u = u_scr[pl.ds(r0, SUB), cs]
            for s in (1, 2, 4):
                a_sh = jnp.where(row8 >= s, pltpu.roll(a, s, 0), 1.0)
                u_sh = jnp.where(row8 >= s, pltpu.roll(u, s, 0), 0.0)
                u = a * u_sh + u
                a = a * a_sh
            h = a * hs[c] + u
            u_scr[pl.ds(r0, SUB), cs] = h
            new.append(h[SUB - 1:SUB, :])
        return tuple(new)

    h_in = tuple(h_scr[:, c * 512:(c + 1) * 512] for c in range(ncol))
    h_fin = lax.fori_loop(0, tt // SUB, scan_body, h_in)
    for c in range(ncol):
        h_scr[:, c * 512:(c + 1) * 512] = h_fin[c]

    y = u_scr[...] * _gelu(yr_ref[...])
    ms = jnp.mean(y * y, axis=-1, keepdims=True)
    out_ref[...] = (y * lax.rsqrt(ms + EPS) * gn_ref[...]).astype(out_ref.dtype)
    ext_scr[pad - tail:pad, :] = ext_scr[pad + tt - tail:pad + tt, :]

    @pl.when(t == nt - 1)
    def _():
        nconv_ref[0] = ext_scr[pad + tt - tail:pad + tt, :]
        hlast_ref[0] = h_scr[...]


def rg_long(proj, conv_buf, h0, wts, *, batch, seq, tt, reset_first):
    nt = seq // tt
    bsel = (lambda b: b) if conv_buf.shape[0] == batch else (lambda b: 0)
    vec = lambda r: pl.BlockSpec((r, D_RNN), lambda b, t: (0, 0))
    return pl.pallas_call(
        functools.partial(_rg_long_kernel, tt=tt, reset_first=reset_first),
        out_shape=(jax.ShapeDtypeStruct((batch * seq, D_RNN), BF16),
                   jax.ShapeDtypeStruct((batch, CONV_W - 1, D_RNN), F32),
                   jax.ShapeDtypeStruct((batch, 1, D_RNN), F32)),
        grid=(batch, nt),
        in_specs=[pl.BlockSpec((tt, D_RNN), lambda b, t: (b * nt + t, 0)),
                  pl.BlockSpec((tt, D_RNN), lambda b, t: (b * nt + t, 1)),
                  pl.BlockSpec((1, CONV_W - 1, D_RNN), lambda b, t: (bsel(b), 0, 0)),
                  pl.BlockSpec((1, 1, D_RNN), lambda b, t: (bsel(b), 0, 0)),
                  vec(CONV_W), vec(1),
                  pl.BlockSpec((RG_BLOCKS, RG_BLOCK, 2 * RG_BLOCK), lambda b, t: (0, 0, 0)),
                  vec(1), vec(1), vec(1), vec(1)],
        out_specs=(pl.BlockSpec((tt, D_RNN), lambda b, t: (b * nt + t, 0)),
                   pl.BlockSpec((1, CONV_W - 1, D_RNN), lambda b, t: (b, 0, 0)),
                   pl.BlockSpec((1, 1, D_RNN), lambda b, t: (b, 0, 0))),
        scratch_shapes=[pltpu.VMEM((tt + SUB, D_RNN), F32),
                        pltpu.VMEM((tt, D_RNN), F32),
                        pltpu.VMEM((tt, D_RNN), F32),
                        pltpu.VMEM((1, D_RNN), F32)],
        compiler_params=_params(("parallel", "arbitrary")),
        name="rg_long",
    )(proj, proj, conv_buf, h0, wts["rg_conv_w"], wts["rg_conv_b"], wts["rg_wax"],
      wts["rg_ba"], wts["rg_bx"], wts["rg_lambda"], wts["rg_out_norm"])


def _rg_short_kernel(xr_ref, yr_ref, cb_ref, h0_ref, cw_ref, cbias_ref, wax_ref, ba_ref, bx_ref,
                     lam_ref, gn_ref, out_ref, nconv_ref, hlast_ref, y_scr, *, nb, seq):
    sp_row = _softplus(-lam_ref[...])
    tail = CONV_W - 1
    for n in range(RG_BLOCKS):
        cs = slice(n * RG_BLOCK, (n + 1) * RG_BLOCK)
        ext = [cb_ref[i, :, cs] for i in range(tail)]
        ext += [xr_ref[t * nb:(t + 1) * nb, cs] for t in range(seq)]
        h = h0_ref[:, cs]
        for t in range(seq):
            xc = cbias_ref[:, cs]
            for i in range(CONV_W):
                xc = xc + ext[t + i] * cw_ref[i:i + 1, cs]
            a, mult, gate_i = _rg_gates(xc, n, wax_ref, ba_ref, bx_ref, sp_row)
            h = a * h + mult * gate_i * xc
            y_scr[t, :, cs] = h * _gelu(yr_ref[t * nb:(t + 1) * nb, cs])
        hlast_ref[:, cs] = h
        for i in range(tail):
            nconv_ref[i, :, cs] = ext[seq + i]
    for t in range(seq):
        y = y_scr[t]
        ms = jnp.mean(y * y, axis=-1, keepdims=True)
        out_ref[t * nb:(t + 1) * nb, :] = (y * lax.rsqrt(ms + EPS) * gn_ref[...]).astype(out_ref.dtype)


def rg_short(proj, conv_buf_t, h0, wts, *, nb, seq):
    rows = nb * seq
    vec = lambda r: pl.BlockSpec((r, D_RNN), lambda i: (0, 0))
    return pl.pallas_call(
        functools.partial(_rg_short_kernel, nb=nb, seq=seq),
        out_shape=(jax.ShapeDtypeStruct((rows, D_RNN), BF16),
                   jax.ShapeDtypeStruct((CONV_W - 1, nb, D_RNN), F32),
                   jax.ShapeDtypeStruct((nb, D_RNN), F32)),
        grid=(1,),
        in_specs=[pl.BlockSpec((rows, D_RNN), lambda i: (0, 0)),
                  pl.BlockSpec((rows, D_RNN), lambda i: (0, 1)),
                  pl.BlockSpec((CONV_W - 1, nb, D_RNN), lambda i: (0, 0, 0)),
                  pl.BlockSpec((nb, D_RNN), lambda i: (0, 0)),
                  vec(CONV_W), vec(1),
                  pl.BlockSpec((RG_BLOCKS, RG_BLOCK, 2 * RG_BLOCK), lambda i: (0, 0, 0)),
                  vec(1), vec(1), vec(1), vec(1)],
        out_specs=(pl.BlockSpec((rows, D_RNN), lambda i: (0, 0)),
                   pl.BlockSpec((CONV_W - 1, nb, D_RNN), lambda i: (0, 0, 0)),
                   pl.BlockSpec((nb, D_RNN), lambda i: (0, 0))),
        scratch_shapes=[pltpu.VMEM((seq, nb, D_RNN), F32)],
        compiler_params=_params(("arbitrary",)),
        name="rg_short",
    )(proj, proj, conv_buf_t, h0, wts["rg_conv_w"], wts["rg_conv_b"], wts["rg_wax"],
      wts["rg_ba"], wts["rg_bx"], wts["rg_lambda"], wts["rg_out_norm"])


def _gla_consts():
    er = lax.broadcasted_iota(jnp.int32, (LANES, LANES), 0)
    ec = lax.broadcasted_iota(jnp.int32, (LANES, LANES), 1)
    eye = jnp.where(er == ec, 1.0, 0.0).astype(BF16)
    ones = jnp.ones((LANES, LANES), BF16)
    row = lax.broadcasted_iota(jnp.int32, (GLA_SUB, LANES), 0)
    row8 = lax.broadcasted_iota(jnp.int32, (SUB, LANES), 0)
    pr = lax.broadcasted_iota(jnp.int32, (2 * GLA_SUB, LANES), 0)
    pick = jnp.where((pr == GLA_SUB) | (pr == GLA_SUB + 1), 1.0, 0.0).astype(BF16)
    return eye, ones, row, row8, pick


def _gla_local_many(groups, consts):
    gens = [_gla_local_levels(units, consts) for units in groups]
    results = [None] * len(gens)
    live = list(range(len(gens)))
    while live:
        for idx in list(live):
            try:
                next(gens[idx])
            except StopIteration as done:
                results[idx] = done.value
                live.remove(idx)
    return results


def _gla_local(units, consts):
    return _gla_local_many([units], consts)[0]


def _gla_local_levels(units, consts):
    eye, ones, row, row8, pick = consts
    assert len(units) == GLA_G

    def scan8(x):
        for s in (1, 2, 4):
            x = x + jnp.where(row8 >= s, pltpu.roll(x, s, 0), 0.0)
        return x

    halves = (slice(0, SUB), slice(SUB, GLA_SUB))
    nt = (((1,), (1,)), ((), ()))
    qes, xs, qas, kbs, cum_all = [], [], [], [], []
    for q, k, v, g, j_range in units:
        cum_lo = scan8(g[halves[0]])
        edge = cum_lo[SUB - 1:SUB, :]
        cum_hi = scan8(g[halves[1]]) + edge
        cums = (cum_lo, cum_hi)
        cum_all.append(cums)
        last = cum_hi[SUB - 1:SUB, :]
        qes.append(jnp.concatenate([q[h] * jnp.exp(c) for h, c in zip(halves, cums)],
                                   axis=0).astype(BF16))
        qas.append(q[halves[1]] * jnp.exp(cum_hi - edge))
        kbs.append(k[halves[0]] * jnp.exp(edge - cum_lo))
        ke = jnp.concatenate([k[h] * jnp.exp(last - c) for h, c in zip(halves, cums)], axis=0)
        el = jnp.exp(last)
        e_hi = el.astype(BF16).astype(F32)
        e_lo = el - e_hi
        extra = jnp.where(row == 0, e_hi, jnp.where(row == 1, e_lo, 0.0))
        xs.append(jnp.concatenate([ke, extra], axis=0).astype(BF16))
    crossing = [u[4][0] < SUB <= u[4][-1] for u in units]
    sc = None
    if any(crossing):
        qa = jnp.concatenate(qas, axis=0).astype(BF16)
        kb = jnp.concatenate(kbs, axis=0).astype(BF16)
        sc = lax.dot_general(qa, kb, nt, preferred_element_type=F32)
    xt = lax.dot_general(eye, jnp.concatenate(xs, axis=0), nt, preferred_element_type=F32)
    yield
    o_halves = []
    for (q, k, v, g, j_range), cums in zip(units, cum_all):
        ps = []
        for j in j_range:
            h, c, jj = halves[j // SUB], cums[j // SUB], j % SUB
            w = jnp.exp(jnp.where(row8 >= jj, c - c[jj:jj + 1, :], -jnp.inf))
            ps.append(q[h] * w * k[j:j + 1, :])
        att = jnp.dot(jnp.concatenate(ps, axis=0).astype(BF16), ones, preferred_element_type=F32)
        o_half = [None, None]
        for idx, j in enumerate(j_range):
            term = att[idx * SUB:(idx + 1) * SUB, :] * v[j:j + 1, :]
            o_half[j // SUB] = term if o_half[j // SUB] is None else o_half[j // SUB] + term
        o_halves.append(o_half)
    yield
    cross = None
    if sc is not None:
        n = GLA_G * SUB
        ur = lax.broadcasted_iota(jnp.int32, (n, n), 0) // SUB
        uc = lax.broadcasted_iota(jnp.int32, (n, n), 1) // SUB
        sc = jnp.where(ur == uc, sc, 0.0).astype(BF16)
        v_lo = jnp.concatenate([u[2][halves[0]] for u in units], axis=0).astype(BF16)
        cross = jnp.dot(sc, v_lo, preferred_element_type=F32)
    xt = xt.astype(BF16)
    zblk = jnp.zeros((2 * GLA_SUB, LANES), BF16)
    wrows = []
    for u, (_, _, v, _, _) in enumerate(units):
        vpad = jnp.concatenate([v, jnp.zeros_like(v)], axis=0).astype(BF16)
        wrows.append(jnp.concatenate([zblk] * (2 * u) + [vpad, pick]
                                     + [zblk] * (2 * (GLA_G - 1 - u)), axis=1))
    kd = jnp.dot(xt, jnp.concatenate(wrows, axis=0), preferred_element_type=F32)
    out = []
    for u, o_half in enumerate(o_halves):
        if cross is not None and crossing[u]:
            o_half[1] = o_half[1] + cross[u * SUB:(u + 1) * SUB, :]
        o = jnp.concatenate([jnp.zeros((SUB, LANES), F32) if t is None else t for t in o_half],
                            axis=0)
        kv = kd[:, (2 * u) * LANES:(2 * u + 1) * LANES]
        dec = kd[:, (2 * u + 1) * LANES:(2 * u + 2) * LANES]
        out.append((qes[u], o, kv, dec))
    return out


def _gla_log_alpha(z_ref, wa2_ref, ba2_ref):
    zz = jnp.dot(z_ref[...].astype(BF16), wa2_ref[...], preferred_element_type=F32) + ba2_ref[...]
    return (jnp.minimum(zz, 0.0) - jnp.log(1.0 + jnp.exp(-jnp.abs(zz)))) * (1.0 / GLA_GATE_TAU)


def _gla_finish(o, go, gain):
    ms = jnp.mean(o * o, axis=-1, keepdims=True)
    on = o * lax.rsqrt(ms + EPS) * gain
    return on * (go * _sigmoid(go))


def _gla_long_kernel(q_ref, k_ref, v_ref, go_ref, z_ref, wa2_ref, ba2_ref, gain_ref, s0_ref,
                     *rest, tt, n_casts):
    cast_in = rest[:n_casts]
    out_ref, sout_ref = rest[n_casts:n_casts + 2]
    cast_out = rest[n_casts + 2:2 * n_casts + 2]
    s_scr, gl_scr, o_scr, qe_scr, kv_scr, dec_scr = rest[2 * n_casts + 2:]
    t = pl.program_id(2)
    nt = pl.num_programs(2)
    nsub = tt // GLA_SUB
    consts = _gla_consts()
    scale = GLA_DK ** -0.5

    for src, dst in zip(cast_in, cast_out):
        dst[...] = src[...].astype(BF16)

    @pl.when(t == 0)
    def _():
        s_scr[...] = s0_ref[0]

    gl_scr[...] = _gla_log_alpha(z_ref, wa2_ref, ba2_ref)

    pair = min(LOCAL_PAIR, nsub)

    def local_body(ip, carry):
        heads = [slice(h * LANES, (h + 1) * LANES) for h in range(GLA_G)]
        subs = [ip * pair + p for p in range(pair)]
        rows = [pl.ds(pl.multiple_of(i * GLA_SUB, GLA_SUB), GLA_SUB) for i in subs]
        groups = [[(q_ref[r, cs] * scale, k_ref[r, cs], v_ref[r, cs], gl_scr[r, cs],
                    range(GLA_SUB)) for cs in heads] for r in rows]
        for i, r, res in zip(subs, rows, _gla_local_many(groups, consts)):
            for h, (qe, o, kv, dec) in enumerate(res):
                cs = heads[h]
                qe_scr[r, cs] = qe
                o_scr[r, cs] = o
                kv_scr[i, h] = kv
                dec_scr[i, h] = dec
        return carry

    lax.fori_loop(0, nsub // pair, local_body, 0, unroll=min(4, nsub // pair))

    def state_body(i, carry):
        rows = pl.ds(pl.multiple_of(i * GLA_SUB, GLA_SUB), GLA_SUB)
        for h in range(GLA_G):
            cs = slice(h * LANES, (h + 1) * LANES)
            s = s_scr[h]
            o_scr[rows, cs] += jnp.dot(qe_scr[rows, cs], s.astype(BF16),
                                       preferred_element_type=F32)
            s_scr[h] = dec_scr[i, h] * s + kv_scr[i, h]
        return carry

    lax.fori_loop(0, nsub, state_body, 0, unroll=min(8, nsub))
    for h in range(GLA_G):
        cs = slice(h * LANES, (h + 1) * LANES)
        out_ref[:, cs] = _gla_finish(o_scr[:, cs], go_ref[:, cs], gain_ref[:, cs]).astype(out_ref.dtype)

    @pl.when(t == nt - 1)
    def _():
        sout_ref[0] = s_scr[...]


def gla_long(proj, z, s0, wts, *, batch, seq, tt, casts=()):
    nt = seq // tt
    gw = GLA_G * LANES
    base = 2 * D_RNN // gw
    per = D_GLA // gw
    nhg = GLA_HEADS // GLA_G
    bsel = (lambda b: b) if s0.shape[0] == batch else (lambda b: 0)
    col = lambda which: pl.BlockSpec((tt, gw), lambda b, h, t: (b * nt + t, base + which * per + h))
    n_steps = batch * nhg * nt
    cast_specs = []
    for a in casts:
        rblocks = math.gcd(n_steps, a.shape[0] // 16)
        cblocks = n_steps // rblocks
        blk = (a.shape[0] // rblocks, a.shape[1] // cblocks)
        assert blk[0] * rblocks == a.shape[0] and blk[1] * cblocks == a.shape[1]
        assert blk[0] % 16 == 0 and blk[1] % LANES == 0
        cast_specs.append(pl.BlockSpec(
            blk, lambda b, h, t, cblocks=cblocks:
                (((b * nhg + h) * nt + t) // cblocks, ((b * nhg + h) * nt + t) % cblocks)))
    return pl.pallas_call(
        functools.partial(_gla_long_kernel, tt=tt, n_casts=len(casts)),
        out_shape=(jax.ShapeDtypeStruct((batch * seq, D_GLA), BF16),
                   jax.ShapeDtypeStruct((batch, GLA_HEADS, GLA_DK, GLA_DK), F32),
                   *[jax.ShapeDtypeStruct(a.shape, BF16) for a in casts]),
        grid=(batch, nhg, nt),
        in_specs=[col(0), col(1), col(2), col(3),
                  pl.BlockSpec((tt, LANES), lambda b, h, t: (b * nt + t, 0)),
                  pl.BlockSpec((LANES, gw), lambda b, h, t: (0, h)),
                  pl.BlockSpec((1, gw), lambda b, h, t: (0, h)),
                  pl.BlockSpec((1, gw), lambda b, h, t: (0, h)),
                  pl.BlockSpec((1, GLA_G, GLA_DK, GLA_DK), lambda b, h, t: (bsel(b), h, 0, 0)),
                  *cast_specs],
        out_specs=(pl.BlockSpec((tt, gw), lambda b, h, t: (b * nt + t, h)),
                   pl.BlockSpec((1, GLA_G, GLA_DK, GLA_DK), lambda b, h, t: (b, h, 0, 0)),
                   *cast_specs),
        scratch_shapes=[pltpu.VMEM((GLA_G, GLA_DK, GLA_DK), F32),
                        pltpu.VMEM((tt, gw), F32),
                        pltpu.VMEM((tt, gw), F32),
                        pltpu.VMEM((tt, gw), BF16),
                        pltpu.VMEM((tt // GLA_SUB, GLA_G, GLA_DK, GLA_DK), F32),
                        pltpu.VMEM((tt // GLA_SUB, GLA_G, GLA_DK, GLA_DK), F32)],
        compiler_params=_params(("parallel", "parallel", "arbitrary")),
        name="gla_long",
    )(proj, proj, proj, proj, z, wts["gla_wa2"], wts["gla_ba2"], wts["gla_head_norm"], s0, *casts)


def _gla_short_kernel(q_ref, k_ref, v_ref, go_ref, z_ref, wa2_ref, ba2_ref, gain_ref, s0_ref,
                      out_ref, sout_ref, *, seq):
    consts = _gla_consts()
    row = consts[2]
    scale = GLA_DK ** -0.5
    gl = _gla_log_alpha(z_ref, wa2_ref, ba2_ref)
    nseq = GLA_SUB // seq
    assert nseq == GLA_G
    nblk = q_ref.shape[0] // GLA_SUB
    keys = [(r, h) for r in range(nblk) for h in range(GLA_G)]
    groups = []
    for r, h in keys:
        rows = slice(r * GLA_SUB, (r + 1) * GLA_SUB)
        cs = slice(h * LANES, (h + 1) * LANES)
        q = q_ref[rows, cs] * scale
        k = k_ref[rows, cs]
        v = v_ref[rows, cs]
        g = gl[rows, cs]
        units = []
        for b in range(nseq):
            mine = (row >= b * seq) & (row < (b + 1) * seq)
            zero = lambda a, mine=mine: jnp.where(mine, a, 0.0)
            units.append((zero(q), zero(k), zero(v), zero(g), range(b * seq, (b + 1) * seq)))
        groups.append(units)
    local = dict(zip(keys, _gla_local_many(groups, consts)))
    inter = {(r, h): [jnp.dot(local[r, h][b][0], s0_ref[r * nseq + b, h].astype(BF16),
                              preferred_element_type=F32) for b in range(nseq)]
             for r, h in keys}
    for r, h in keys:
        rows = slice(r * GLA_SUB, (r + 1) * GLA_SUB)
        cs = slice(h * LANES, (h + 1) * LANES)
        o = jnp.zeros((GLA_SUB, LANES), F32)
        for b in range(nseq):
            _, o_b, kv, dec = local[r, h][b]
            o = o + o_b + inter[r, h][b]
            sout_ref[r * nseq + b, h] = dec * s0_ref[r * nseq + b, h] + kv
        out_ref[rows, cs] = _gla_finish(o, go_ref[rows, cs], gain_ref[:, cs]).astype(out_ref.dtype)


def gla_short(proj, z, s0, wts, *, nb, seq):
    gw = GLA_G * LANES
    per = D_GLA // gw
    rows = GLA_SHORT_BLOCKS * GLA_SUB
    bpb = rows // seq
    col = lambda which: pl.BlockSpec((rows, gw), lambda i, h: (i, which * per + h))
    return pl.pallas_call(
        functools.partial(_gla_short_kernel, seq=seq),
        out_shape=(jax.ShapeDtypeStruct((nb * seq, D_GLA), BF16),
                   jax.ShapeDtypeStruct((nb, GLA_HEADS, GLA_DK, GLA_DK), F32)),
        grid=(nb // bpb, GLA_HEADS // GLA_G),
        in_specs=[col(0), col(1), col(2), col(3),
                  pl.BlockSpec((rows, LANES), lambda i, h: (i, 0)),
                  pl.BlockSpec((LANES, gw), lambda i, h: (0, h)),
                  pl.BlockSpec((1, gw), lambda i, h: (0, h)),
                  pl.BlockSpec((1, gw), lambda i, h: (0, h)),
                  pl.BlockSpec((bpb, GLA_G, GLA_DK, GLA_DK), lambda i, h: (i, h, 0, 0))],
        out_specs=(pl.BlockSpec((rows, gw), lambda i, h: (i, h)),
                   pl.BlockSpec((bpb, GLA_G, GLA_DK, GLA_DK), lambda i, h: (i, h, 0, 0))),
        compiler_params=_params(("parallel", "parallel")),
        name="gla_short",
    )(proj, proj, proj, proj, z, wts["gla_wa2"], wts["gla_ba2"], wts["gla_head_norm"], s0)


def _ffn_kernel(xm_ref, xs_ref, xp_ref, wg_ref, wu_ref, cw_ref, cb_ref, bufs_ref,
                hp_ref, hs_ref, nbufp_ref, nbufs_ref, extp_scr, exts_scr, mhist_scr,
                *, steps, tiles_per_seq, nb):
    i = pl.program_id(1)
    (m_first, _), (s_first, _), (p_first, p_count) = steps
    tail = FFN_CONV_W - 1
    pad = SUB
    tm = xp_ref.shape[0]
    rows_s = xs_ref.shape[0]
    hist_s = tail * nb

    def conv(ext_ref, start, shift, rows):
        gc = cb_ref[...]
        for c in range(FFN_CONV_W):
            lo = start - (tail - c) * shift
            gc = gc + ext_ref[lo:lo + rows, :] * cw_ref[c:c + 1, :]
        return gc

    @pl.when(i == m_first)
    def _():
        g = jnp.dot(xm_ref[...], wg_ref[...], preferred_element_type=F32)
        mhist_scr[...] = g[g.shape[0] - tail:, :]

    @pl.when((i >= p_first) & (i < p_first + p_count))
    def _():
        @pl.when((i - p_first) % tiles_per_seq == 0)
        def _():
            extp_scr[pad - tail:pad, :] = mhist_scr[...]

        x = xp_ref[...]
        extp_scr[pad:pad + tm, :] = jnp.dot(x, wg_ref[...], preferred_element_type=F32)
        up = jnp.dot(x, wu_ref[...], preferred_element_type=F32)
        hp_ref[...] = (_gelu(conv(extp_scr, pad, 1, tm)) * up).astype(hp_ref.dtype)
        last = extp_scr[pad + tm - tail:pad + tm, :]
        extp_scr[pad - tail:pad, :] = last
        nbufp_ref[0] = last

    @pl.when(i == s_first)
    def _():
        x = xs_ref[...]
        exts_scr[0:hist_s, :] = bufs_ref[...]
        exts_scr[hist_s:hist_s + rows_s, :] = jnp.dot(x, wg_ref[...], preferred_element_type=F32)
        up = jnp.dot(x, wu_ref[...], preferred_element_type=F32)
        hs_ref[...] = (_gelu(conv(exts_scr, hist_s, nb, rows_s)) * up).astype(hs_ref.dtype)
        nbufs_ref[...] = exts_scr[rows_s:rows_s + hist_s, :]


def ffn_gate(xm, xs, xp, bufs, wts, *, tm, tn, seq_rows, nb):
    rows_p, rows_s = xp.shape[0], xs.shape[0]
    tail = FFN_CONV_W - 1
    steps, ni = _group_steps([(xm.shape[0], xm.shape[0]), (rows_s, rows_s), (rows_p, tm)])
    (_, _), (_, _), (p_first, p_count) = steps
    tps = seq_rows // tm
    ptile = lambda i: _tile_index(i, p_first, p_count)
    const = lambda shape: pl.BlockSpec(shape, lambda j, i: (0, 0), pipeline_mode=pl.Buffered(1))
    colblk = lambda r: pl.BlockSpec((r, tn), lambda j, i: (0, j))
    return pl.pallas_call(
        functools.partial(_ffn_kernel, steps=tuple(steps), tiles_per_seq=tps, nb=nb),
        out_shape=(jax.ShapeDtypeStruct((rows_p, D_FF), BF16),
                   jax.ShapeDtypeStruct((rows_s, D_FF), BF16),
                   jax.ShapeDtypeStruct((rows_p // seq_rows, tail, D_FF), F32),
                   jax.ShapeDtypeStruct((tail * nb, D_FF), F32)),
        grid=(D_FF // tn, ni),
        in_specs=[const(xm.shape),
                  const(xs.shape),
                  pl.BlockSpec((tm, D_MODEL), lambda j, i: (ptile(i), 0)),
                  colblk(D_MODEL), colblk(D_MODEL), colblk(FFN_CONV_W), colblk(1),
                  colblk(tail * nb)],
        out_specs=(pl.BlockSpec((tm, tn), lambda j, i: (ptile(i), j)),
                   colblk(rows_s),
                   pl.BlockSpec((1, tail, tn), lambda j, i: (ptile(i) // tps, 0, j)),
                   colblk(tail * nb)),
        scratch_shapes=[pltpu.VMEM((tm + SUB, tn), F32),
                        pltpu.VMEM((rows_s + tail * nb, tn), F32),
                        pltpu.VMEM((tail, tn), F32)],
        compiler_params=_params(("parallel", "arbitrary")),
        name="ffn_gate",
    )(xm, xs, xp, wts["w_gate"], wts["w_up"], wts["ffn_conv_w"], wts["ffn_conv_b"], bufs)


PROMPT_TM = 1024
DOWN_TM = 512
DOWN_TN = 512
FFN_TM = 1024
FFN_TN = 512
SEQ_TT = 256
NORM_TM = 256


def kernel(x_prompt, x_sample, state_rglru_conv, state_rglru_h, state_gla_S, state_ffn_conv,
           meta_tokens, norm_mix, w_in, rg_conv_w, rg_conv_b, rg_wa, rg_ba, rg_wx, rg_bx, rg_lambda,
           rg_out_norm, gla_wa2, gla_ba2, gla_head_norm, w_out, norm_ffn, w_gate, w_up,
           ffn_conv_w, ffn_conv_b, w_down, final_norm):
    batch, seq, _ = x_prompt.shape
    nb, sseq, _ = x_sample.shape
    row = lambda a: a.reshape(1, -1)
    w_in_b = w_in[0].astype(BF16)
    wts = {
        "w_in": w_in_b,
        "w_z": jnp.pad(w_in_b[:, PROJ_COLS:], ((0, 0), (0, LANES - GLA_RANK))),
        "rg_conv_w": rg_conv_w[0], "rg_conv_b": row(rg_conv_b[0]),
        "rg_wax": jnp.concatenate([rg_wa[0], rg_wx[0]], axis=-1).astype(BF16),
        "rg_ba": row(rg_ba[0]), "rg_bx": row(rg_bx[0]), "rg_lambda": row(rg_lambda[0]),
        "rg_out_norm": row(rg_out_norm[0]),
        "gla_wa2": jnp.pad(gla_wa2[0].astype(BF16), ((0, LANES - GLA_RANK), (0, 0))),
        "gla_ba2": row(gla_ba2[0]), "gla_head_norm": row(gla_head_norm[0]),
        "ffn_conv_w": ffn_conv_w[0], "ffn_conv_b": row(ffn_conv_b[0]),
    }

    def to_batch_major(a):
        return jnp.swapaxes(a.reshape(sseq, nb, -1), 0, 1).reshape(nb * sseq, -1)

    def to_time_major(a):
        return jnp.swapaxes(a.reshape(nb, sseq, -1), 0, 1).reshape(nb * sseq, -1)

    rows_p, rows_s = batch * seq, nb * sseq
    xs = [meta_tokens, to_time_major(x_sample), x_prompt.reshape(rows_p, D_MODEL)]
    tms = [N_META, rows_s, PROMPT_TM]
    norm_tms = [N_META, NORM_TM, NORM_TM]

    def norm_all(arrs, g, dtype):
        return [rmsnorm(a, g, dtype, t) for a, t in zip(arrs, norm_tms)]

    xn = norm_all(xs, norm_mix[0], BF16)
    proj_m, proj_s, proj_p = grouped_matmul(
        [dict(parts=[a], tm=t) for a, t in zip(xn, tms)], wts["w_in"], tn=1024, n_cols=PROJ_COLS)
    z_m, z_s, z_p = [matmul(a, wts["w_z"], tm=t, tn=LANES, tk=D_MODEL) for a, t in zip(xn, tms)]

    zeros = lambda *s: jnp.zeros(s, F32)
    rnn_m, m_conv, m_h = rg_long(proj_m, zeros(1, CONV_W - 1, D_RNN), zeros(1, 1, D_RNN), wts,
                                 batch=1, seq=N_META, tt=N_META, reset_first=True)
    gla_m, m_s = gla_long(proj_m, z_m, zeros(1, GLA_HEADS, GLA_DK, GLA_DK), wts, batch=1,
                          seq=N_META, tt=N_META)
    rnn_p, p_conv, p_h = rg_long(proj_p, m_conv, m_h, wts, batch=batch, seq=seq, tt=SEQ_TT,
                                 reset_first=False)
    gla_p, p_s, wts["w_gate"], wts["w_up"], wts["w_down"] = gla_long(
        proj_p, z_p, m_s, wts, batch=batch, seq=seq, tt=SEQ_TT,
        casts=(w_gate.reshape(D_MODEL, D_FF), w_up.reshape(D_MODEL, D_FF),
               w_down.reshape(D_FF, D_MODEL)))
    rnn_s, s_conv_t, s_h = rg_short(proj_s, jnp.swapaxes(state_rglru_conv[0], 0, 1),
                                    state_rglru_h[0], wts, nb=nb, seq=sseq)
    gla_sb, s_s = gla_short(to_batch_major(proj_s[:, 2 * D_RNN:]), to_batch_major(z_s),
                            state_gla_S[0], wts, nb=nb, seq=sseq)
    gla_s = to_time_major(gla_sb)

    x1 = grouped_matmul(
        [dict(parts=[r, g], tm=t, res=x)
         for r, g, t, x in zip([rnn_m, rnn_s, rnn_p], [gla_m, gla_s, gla_p], tms, xs)],
        w_out, tn=512, n_cols=D_MODEL, cast_w=True)
    xn2 = norm_all(x1, norm_ffn[0], BF16)
    tail = FFN_CONV_W - 1
    bufs = jnp.swapaxes(state_ffn_conv[0], 0, 1).reshape(tail * nb, D_FF)
    hid_p, hid_s, p_ffn, s_ffn_t = ffn_gate(xn2[0], xn2[1], xn2[2], bufs, wts,
                                            tm=FFN_TM, tn=FFN_TN, seq_rows=seq, nb=nb)
    (x2_s,) = grouped_matmul([dict(parts=[hid_s], tm=rows_s, res=x1[1])], wts["w_down"],
                             tn=DOWN_TN, n_cols=D_MODEL)
    (x2_p,) = grouped_matmul([dict(parts=[hid_p], tm=DOWN_TM, res=x1[2])], wts["w_down"],
                             tn=DOWN_TN, n_cols=D_MODEL)
    y_p = rmsnorm(x2_p, final_norm, F32, NORM_TM)
    y_s = to_batch_major(rmsnorm(x2_s, final_norm, F32, NORM_TM))

    return (y_p.reshape(batch, seq, D_MODEL), y_s.reshape(nb, sseq, D_MODEL),
            p_conv[None], p_h.reshape(1, batch, D_RNN), p_s[None], p_ffn[None],
            jnp.swapaxes(s_conv_t, 0, 1)[None], s_h[None], s_s[None],
            jnp.swapaxes(s_ffn_t.reshape(tail, nb, D_FF), 0, 1)[None])
```

```python
import functools
import math

import jax
import jax.numpy as jnp
from jax import lax
from jax.experimental import pallas as pl
from jax.experimental.pallas import tpu as pltpu

F32 = jnp.float32
BF16 = jnp.bfloat16

D_MODEL = 4096
N_META = 16
D_RNN = 2048
D_GLA = 2048
RG_BLOCKS = 16
RG_BLOCK = 128
CONV_W = 4
RG_C = 8.0
GLA_HEADS = 16
GLA_DK = 128
GLA_RANK = 16
GLA_GATE_TAU = 16.0
D_FF = 3 * D_MODEL
FFN_CONV_W = 3
EPS = 1e-6
PROJ_COLS = 2 * D_RNN + 4 * D_GLA

LANES = 128
SUB = 8
SUB_BF16 = 16
RG_SCAN_COLS = 512
VMEM_LIMIT = 60 * 1024 * 1024
GLA_SUB = 16
GLA_G = 4
LOCAL_PAIR = 2
GLA_SHORT_BLOCKS = 4


def _params(sem):
    return pltpu.CompilerParams(dimension_semantics=sem, vmem_limit_bytes=VMEM_LIMIT)


def _gelu(x):
    c = math.sqrt(2.0 / math.pi)
    return 0.5 * x * (1.0 + jnp.tanh(c * (x + 0.044715 * (x * x * x))))


def _softplus(x):
    return jnp.maximum(x, 0.0) + jnp.log1p(jnp.exp(-jnp.abs(x)))


def _sigmoid(x):
    return 0.5 * (1.0 + jnp.tanh(0.5 * x))


def _rmsnorm_kernel(x_ref, g_ref, o_ref):
    x = x_ref[...]
    ms = jnp.mean(x * x, axis=-1, keepdims=True)
    o_ref[...] = (x * lax.rsqrt(ms + EPS) * g_ref[...]).astype(o_ref.dtype)


def rmsnorm(x, g, out_dtype, tm):
    m, d = x.shape
    return pl.pallas_call(
        _rmsnorm_kernel,
        out_shape=jax.ShapeDtypeStruct((m, d), out_dtype),
        grid=(m // tm,),
        in_specs=[pl.BlockSpec((tm, d), lambda i: (i, 0)),
                  pl.BlockSpec((1, d), lambda i: (0, 0))],
        out_specs=pl.BlockSpec((tm, d), lambda i: (i, 0)),
        compiler_params=_params(("parallel",)),
        name="rmsnorm",
    )(x, g.reshape(1, d))


def _group_steps(groups):
    steps, first = [], 0
    for rows, tm in groups:
        steps.append((first, rows // tm))
        first += rows // tm
    return steps, first


def _tile_index(i, first, count):
    return jnp.clip(i - first, 0, count - 1)


def _gmm_kernel(*refs, layout, steps, cast_w):
    i = pl.program_id(1)
    pos = 0
    g_in = []
    for n_parts, has_res in layout:
        xs = refs[pos:pos + n_parts]
        pos += n_parts
        r = refs[pos] if has_res else None
        pos += int(has_res)
        g_in.append((xs, r))
    w_ref = refs[pos]
    outs = refs[pos + 1:pos + 1 + len(layout)]
    if cast_w:
        wsrc = refs[pos + 1 + len(layout)]

        @pl.when(i == 0)
        def _():
            wsrc[...] = w_ref[...].astype(BF16)
    else:
        wsrc = w_ref
    for (xs, r), o_ref, (first, count) in zip(g_in, outs, steps):
        @pl.when((i >= first) & (i < first + count))
        def _(xs=xs, r=r, o_ref=o_ref):
            acc, off = None, 0
            for x_ref in xs:
                kk = x_ref.shape[1]
                part = jnp.dot(x_ref[...], wsrc[off:off + kk, :], preferred_element_type=F32)
                acc = part if acc is None else acc + part
                off += kk
            o_ref[...] = acc if r is None else acc + r[...]


def grouped_matmul(groups, w, *, tn, n_cols, cast_w=False):
    kdim = sum(p.shape[1] for p in groups[0]["parts"])
    steps, ni = _group_steps([(g["parts"][0].shape[0], g["tm"]) for g in groups])
    in_specs, args, layout = [], [], []
    for g, (first, count) in zip(groups, steps):
        rows_of = lambda j, i, first=first, count=count: _tile_index(i, first, count)
        once = dict(pipeline_mode=pl.Buffered(1)) if count == 1 else {}
        for p in g["parts"]:
            in_specs.append(pl.BlockSpec((g["tm"], p.shape[1]),
                                         lambda j, i, rows_of=rows_of: (rows_of(j, i), 0), **once))
            args.append(p)
        if g.get("res") is not None:
            in_specs.append(pl.BlockSpec((g["tm"], tn),
                                         lambda j, i, rows_of=rows_of: (rows_of(j, i), j)))
            args.append(g["res"])
        layout.append((len(g["parts"]), g.get("res") is not None))
    scratch = [pltpu.VMEM((kdim, tn), BF16)] if cast_w else []
    if w.ndim == 3:
        in_specs.append(pl.BlockSpec((None, kdim, tn), lambda j, i: (0, 0, j)))
    else:
        in_specs.append(pl.BlockSpec((kdim, tn), lambda j, i: (0, j)))
    args.append(w)
    out_specs = [pl.BlockSpec((g["tm"], tn),
                              lambda j, i, first=first, count=count:
                                  (_tile_index(i, first, count), j))
                 for g, (first, count) in zip(groups, steps)]
    return pl.pallas_call(
        functools.partial(_gmm_kernel, layout=tuple(layout), steps=tuple(steps), cast_w=cast_w),
        out_shape=[jax.ShapeDtypeStruct((g["parts"][0].shape[0], n_cols), F32) for g in groups],
        grid=(n_cols // tn, ni),
        in_specs=in_specs,
        out_specs=out_specs,
        scratch_shapes=scratch,
        compiler_params=_params(("parallel", "arbitrary")),
        name="grouped_matmul",
    )(*args)


def _rg_gates(xc, n, wax_ref, ba_ref, bx_ref, sp_row):
    cs = slice(n * RG_BLOCK, (n + 1) * RG_BLOCK)
    pre = jnp.dot(xc.astype(BF16), wax_ref[n], preferred_element_type=F32)
    r = _sigmoid(pre[:, :RG_BLOCK] + ba_ref[:, cs])
    i = _sigmoid(pre[:, RG_BLOCK:] + bx_ref[:, cs])
    log_a = -RG_C * r * sp_row[:, cs]
    a = jnp.exp(log_a)
    th = jnp.tanh(log_a)
    p = -2.0 * th
    mult = jnp.where(p > 0.0, p * lax.rsqrt(p * (1.0 - th)), 0.0)
    return a, mult, i


def _rg_long_kernel(xr_ref, yr_ref, cb_ref, h0_ref, cw_ref, cbias_ref, wax_ref, ba_ref, bx_ref,
                    lam_ref, gn_ref, out_ref, nconv_ref, hlast_ref,
                    ext_scr, a_scr, u_scr, h_scr, *, tt, reset_first):
    t = pl.program_id(1)
    nt = pl.num_programs(1)
    pad = SUB
    tail = CONV_W - 1

    @pl.when(t == 0)
    def _():
        ext_scr[0:pad - tail, :] = jnp.zeros((pad - tail, D_RNN), F32)
        ext_scr[pad - tail:pad, :] = cb_ref[0]
        h_scr[...] = h0_ref[0]

    ext_scr[pad:pad + tt, :] = xr_ref[...]
    sp_row = _softplus(-lam_ref[...])
    row = lax.broadcasted_iota(jnp.int32, (tt, RG_BLOCK), 0)
    first_row = jnp.where(t == 0, 0, -1)
    for n in range(RG_BLOCKS):
        cs = slice(n * RG_BLOCK, (n + 1) * RG_BLOCK)
        x = ext_scr[pad:pad + tt, cs]
        hist = ext_scr[0:pad, cs]
        xc = cbias_ref[:, cs]
        for i in range(CONV_W):
            s = tail - i
            if s == 0:
                xs = x
            else:
                xs = pltpu.roll(x, s, 0)
                head = jnp.where(row[:SUB] < s, pltpu.roll(hist, s, 0), xs[:SUB])
                xs = jnp.concatenate([head, xs[SUB:]], axis=0) if tt > SUB else head
            xc = xc + xs * cw_ref[i:i + 1, cs]
        a, mult, gate_i = _rg_gates(xc, n, wax_ref, ba_ref, bx_ref, sp_row)
        if reset_first:
            mult = jnp.where(row == first_row, 1.0, mult)
        a_scr[:, cs] = a
        u_scr[:, cs] = mult * gate_i * xc

    row8 = lax.broadcasted_iota(jnp.int32, (SUB, RG_SCAN_COLS), 0)
    ncol = D_RNN // RG_SCAN_COLS

    def scan_body(j, hs):
        r0 = pl.multiple_of(j * SUB, SUB)
        new = []
        for c in range(ncol):
            cs = slice(c * RG_SCAN_COLS, (c + 1) * RG_SCAN_COLS)
            a = a_scr[pl.ds(r0, SUB), cs]
            u = u_scr[pl.ds(r0, SUB), cs]
            for s in (1, 2, 4):
                a_sh = jnp.where(row8 >= s, pltpu.roll(a, s, 0), 1.0)
                u_sh = jnp.where(row8 >= s, pltpu.roll(u, s, 0), 0.0)
                u = a * u_sh + u
                a = a * a_sh
            h = a * hs[c] + u
            u_scr[pl.ds(r0, SUB), cs] = h
            new.append(h[SUB - 1:SUB, :])
        return tuple(new)

    h_in = tuple(h_scr[:, c * RG_SCAN_COLS:(c + 1) * RG_SCAN_COLS] for c in range(ncol))
    h_fin = lax.fori_loop(0, tt // SUB, scan_body, h_in)
    for c in range(ncol):
        h_scr[:, c * RG_SCAN_COLS:(c + 1) * RG_SCAN_COLS] = h_fin[c]

    y = u_scr[...] * _gelu(yr_ref[...])
    ms = jnp.mean(y * y, axis=-1, keepdims=True)
    out_ref[...] = (y * lax.rsqrt(ms + EPS) * gn_ref[...]).astype(out_ref.dtype)
    ext_scr[pad - tail:pad, :] = ext_scr[pad + tt - tail:pad + tt, :]

    @pl.when(t == nt - 1)
    def _():
        nconv_ref[0] = ext_scr[pad + tt - tail:pad + tt, :]
        hlast_ref[0] = h_scr[...]


def rg_long(proj, conv_buf, h0, wts, *, batch, seq, tt, reset_first):
    nt = seq // tt
    bsel = (lambda b: b) if conv_buf.shape[0] == batch else (lambda b: 0)
    vec = lambda r: pl.BlockSpec((r, D_RNN), lambda b, t: (0, 0))
    return pl.pallas_call(
        functools.partial(_rg_long_kernel, tt=tt, reset_first=reset_first),
        out_shape=(jax.ShapeDtypeStruct((batch * seq, D_RNN), BF16),
                   jax.ShapeDtypeStruct((batch, CONV_W - 1, D_RNN), F32),
                   jax.ShapeDtypeStruct((batch, 1, D_RNN), F32)),
        grid=(batch, nt),
        in_specs=[pl.BlockSpec((tt, D_RNN), lambda b, t: (b * nt + t, 0)),
                  pl.BlockSpec((tt, D_RNN), lambda b, t: (b * nt + t, 1)),
                  pl.BlockSpec((1, CONV_W - 1, D_RNN), lambda b, t: (bsel(b), 0, 0)),
                  pl.BlockSpec((1, 1, D_RNN), lambda b, t: (bsel(b), 0, 0)),
                  vec(CONV_W), vec(1),
                  pl.BlockSpec((RG_BLOCKS, RG_BLOCK, 2 * RG_BLOCK), lambda b, t: (0, 0, 0)),
                  vec(1), vec(1), vec(1), vec(1)],
        out_specs=(pl.BlockSpec((tt, D_RNN), lambda b, t: (b * nt + t, 0)),
                   pl.BlockSpec((1, CONV_W - 1, D_RNN), lambda b, t: (b, 0, 0)),
                   pl.BlockSpec((1, 1, D_RNN), lambda b, t: (b, 0, 0))),
        scratch_shapes=[pltpu.VMEM((tt + SUB, D_RNN), F32),
                        pltpu.VMEM((tt, D_RNN), F32),
                        pltpu.VMEM((tt, D_RNN), F32),
                        pltpu.VMEM((1, D_RNN), F32)],
        compiler_params=_params(("parallel", "arbitrary")),
        name="rg_long",
    )(proj, proj, conv_buf, h0, wts["rg_conv_w"], wts["rg_conv_b"], wts["rg_wax"],
      wts["rg_ba"], wts["rg_bx"], wts["rg_lambda"], wts["rg_out_norm"])


def _rg_short_kernel(xr_ref, yr_ref, cb_ref, h0_ref, cw_ref, cbias_ref, wax_ref, ba_ref, bx_ref,
                     lam_ref, gn_ref, out_ref, nconv_ref, hlast_ref, y_scr, *, nb, seq):
    sp_row = _softplus(-lam_ref[...])
    tail = CONV_W - 1
    for n in range(RG_BLOCKS):
        cs = slice(n * RG_BLOCK, (n + 1) * RG_BLOCK)
        ext = [cb_ref[i, :, cs] for i in range(tail)]
        ext += [xr_ref[t * nb:(t + 1) * nb, cs] for t in range(seq)]
        h = h0_ref[:, cs]
        for t in range(seq):
            xc = cbias_ref[:, cs]
            for i in range(CONV_W):
                xc = xc + ext[t + i] * cw_ref[i:i + 1, cs]
            a, mult, gate_i = _rg_gates(xc, n, wax_ref, ba_ref, bx_ref, sp_row)
            h = a * h + mult * gate_i * xc
            y_scr[t, :, cs] = h * _gelu(yr_ref[t * nb:(t + 1) * nb, cs])
        hlast_ref[:, cs] = h
        for i in range(tail):
            nconv_ref[i, :, cs] = ext[seq + i]
    for t in range(seq):
        y = y_scr[t]
        ms = jnp.mean(y * y, axis=-1, keepdims=True)
        out_ref[t * nb:(t + 1) * nb, :] = (y * lax.rsqrt(ms + EPS) * gn_ref[...]).astype(out_ref.dtype)


def rg_short(proj, conv_buf_t, h0, wts, *, nb, seq):
    rows = nb * seq
    vec = lambda r: pl.BlockSpec((r, D_RNN), lambda i: (0, 0))
    return pl.pallas_call(
        functools.partial(_rg_short_kernel, nb=nb, seq=seq),
        out_shape=(jax.ShapeDtypeStruct((rows, D_RNN), BF16),
                   jax.ShapeDtypeStruct((CONV_W - 1, nb, D_RNN), F32),
                   jax.ShapeDtypeStruct((nb, D_RNN), F32)),
        grid=(1,),
        in_specs=[pl.BlockSpec((rows, D_RNN), lambda i: (0, 0)),
                  pl.BlockSpec((rows, D_RNN), lambda i: (0, 1)),
                  pl.BlockSpec((CONV_W - 1, nb, D_RNN), lambda i: (0, 0, 0)),
                  pl.BlockSpec((nb, D_RNN), lambda i: (0, 0)),
                  vec(CONV_W), vec(1),
                  pl.BlockSpec((RG_BLOCKS, RG_BLOCK, 2 * RG_BLOCK), lambda i: (0, 0, 0)),
                  vec(1), vec(1), vec(1), vec(1)],
        out_specs=(pl.BlockSpec((rows, D_RNN), lambda i: (0, 0)),
                   pl.BlockSpec((CONV_W - 1, nb, D_RNN), lambda i: (0, 0, 0)),
                   pl.BlockSpec((nb, D_RNN), lambda i: (0, 0))),
        scratch_shapes=[pltpu.VMEM((seq, nb, D_RNN), F32)],
        compiler_params=_params(("arbitrary",)),
        name="rg_short",
    )(proj, proj, conv_buf_t, h0, wts["rg_conv_w"], wts["rg_conv_b"], wts["rg_wax"],
      wts["rg_ba"], wts["rg_bx"], wts["rg_lambda"], wts["rg_out_norm"])


def _gla_consts():
    er = lax.broadcasted_iota(jnp.int32, (LANES, LANES), 0)
    ec = lax.broadcasted_iota(jnp.int32, (LANES, LANES), 1)
    eye = jnp.where(er == ec, 1.0, 0.0).astype(BF16)
    ones = jnp.ones((LANES, LANES), BF16)
    row = lax.broadcasted_iota(jnp.int32, (GLA_SUB, LANES), 0)
    row8 = lax.broadcasted_iota(jnp.int32, (SUB, LANES), 0)
    pr = lax.broadcasted_iota(jnp.int32, (2 * GLA_SUB, LANES), 0)
    pick = jnp.where((pr == GLA_SUB) | (pr == GLA_SUB + 1), 1.0, 0.0).astype(BF16)
    return eye, ones, row, row8, pick


def _gla_local_many(groups, consts):
    gens = [_gla_local_levels(units, consts) for units in groups]
    results = [None] * len(gens)
    live = list(range(len(gens)))
    while live:
        for idx in list(live):
            try:
                next(gens[idx])
            except StopIteration as done:
                results[idx] = done.value
                live.remove(idx)
    return results


def _gla_local(units, consts):
    return _gla_local_many([units], consts)[0]


def _gla_local_levels(units, consts):
    eye, ones, row, row8, pick = consts
    assert len(units) == GLA_G

    def scan8(x):
        for s in (1, 2, 4):
            x = x + jnp.where(row8 >= s, pltpu.roll(x, s, 0), 0.0)
        return x

    halves = (slice(0, SUB), slice(SUB, GLA_SUB))
    nt = (((1,), (1,)), ((), ()))
    qes, xs, qas, kbs, cum_all = [], [], [], [], []
    for q, k, v, g, j_range in units:
        cum_lo = scan8(g[halves[0]])
        edge = cum_lo[SUB - 1:SUB, :]
        cum_hi = scan8(g[halves[1]]) + edge
        cums = (cum_lo, cum_hi)
        cum_all.append(cums)
        last = cum_hi[SUB - 1:SUB, :]
        qes.append(jnp.concatenate([q[h] * jnp.exp(c) for h, c in zip(halves, cums)],
                                   axis=0).astype(BF16))
        qas.append(q[halves[1]] * jnp.exp(cum_hi - edge))
        kbs.append(k[halves[0]] * jnp.exp(edge - cum_lo))
        ke = jnp.concatenate([k[h] * jnp.exp(last - c) for h, c in zip(halves, cums)], axis=0)
        el = jnp.exp(last)
        e_hi = el.astype(BF16).astype(F32)
        e_lo = el - e_hi
        extra = jnp.where(row == 0, e_hi, jnp.where(row == 1, e_lo, 0.0))
        xs.append(jnp.concatenate([ke, extra], axis=0).astype(BF16))
    crossing = [u[4][0] < SUB <= u[4][-1] for u in units]
    sc = None
    if any(crossing):
        qa = jnp.concatenate(qas, axis=0).astype(BF16)
        kb = jnp.concatenate(kbs, axis=0).astype(BF16)
        sc = lax.dot_general(qa, kb, nt, preferred_element_type=F32)
    xt = lax.dot_general(eye, jnp.concatenate(xs, axis=0), nt, preferred_element_type=F32)
    yield
    o_halves = []
    for (q, k, v, g, j_range), cums in zip(units, cum_all):
        ps = []
        for j in j_range:
            h, c, jj = halves[j // SUB], cums[j // SUB], j % SUB
            w = jnp.exp(jnp.where(row8 >= jj, c - c[jj:jj + 1, :], -jnp.inf))
            ps.append(q[h] * w * k[j:j + 1, :])
        att = jnp.dot(jnp.concatenate(ps, axis=0).astype(BF16), ones, preferred_element_type=F32)
        o_half = [None, None]
        for idx, j in enumerate(j_range):
            term = att[idx * SUB:(idx + 1) * SUB, :] * v[j:j + 1, :]
            o_half[j // SUB] = term if o_half[j // SUB] is None else o_half[j // SUB] + term
        o_halves.append(o_half)
    yield
    cross = None
    if sc is not None:
        n = GLA_G * SUB
        ur = lax.broadcasted_iota(jnp.int32, (n, n), 0) // SUB
        uc = lax.broadcasted_iota(jnp.int32, (n, n), 1) // SUB
        sc = jnp.where(ur == uc, sc, 0.0).astype(BF16)
        v_lo = jnp.concatenate([u[2][halves[0]] for u in units], axis=0).astype(BF16)
        cross = jnp.dot(sc, v_lo, preferred_element_type=F32)
    xt = xt.astype(BF16)
    zblk = jnp.zeros((2 * GLA_SUB, LANES), BF16)
    wrows = []
    for u, (_, _, v, _, _) in enumerate(units):
        vpad = jnp.concatenate([v, jnp.zeros_like(v)], axis=0).astype(BF16)
        wrows.append(jnp.concatenate([zblk] * (2 * u) + [vpad, pick]
                                     + [zblk] * (2 * (GLA_G - 1 - u)), axis=1))
    kd = jnp.dot(xt, jnp.concatenate(wrows, axis=0), preferred_element_type=F32)
    out = []
    for u, o_half in enumerate(o_halves):
        if cross is not None and crossing[u]:
            o_half[1] = o_half[1] + cross[u * SUB:(u + 1) * SUB, :]
        o = jnp.concatenate([jnp.zeros((SUB, LANES), F32) if t is None else t for t in o_half],
                            axis=0)
        kv = kd[:, (2 * u) * LANES:(2 * u + 1) * LANES]
        dec = kd[:, (2 * u + 1) * LANES:(2 * u + 2) * LANES]
        out.append((qes[u], o, kv, dec))
    return out


def _gla_log_alpha(z_ref, wa2_ref, ba2_ref):
    zz = jnp.dot(z_ref[...].astype(BF16), wa2_ref[...], preferred_element_type=F32) + ba2_ref[...]
    return (jnp.minimum(zz, 0.0) - jnp.log(1.0 + jnp.exp(-jnp.abs(zz)))) * (1.0 / GLA_GATE_TAU)


def _gla_finish(o, go, gain):
    ms = jnp.mean(o * o, axis=-1, keepdims=True)
    on = o * lax.rsqrt(ms + EPS) * gain
    return on * (go * _sigmoid(go))


def _gla_long_kernel(q_ref, k_ref, v_ref, go_ref, z_ref, wa2_ref, ba2_ref, gain_ref, s0_ref,
                     *rest, tt, n_casts):
    cast_in = rest[:n_casts]
    out_ref, sout_ref = rest[n_casts:n_casts + 2]
    cast_out = rest[n_casts + 2:2 * n_casts + 2]
    s_scr, gl_scr, o_scr, qe_scr, kv_scr, dec_scr = rest[2 * n_casts + 2:]
    t = pl.program_id(2)
    nt = pl.num_programs(2)
    nsub = tt // GLA_SUB
    consts = _gla_consts()
    scale = GLA_DK ** -0.5

    for src, dst in zip(cast_in, cast_out):
        dst[...] = src[...].astype(BF16)

    @pl.when(t == 0)
    def _():
        s_scr[...] = s0_ref[0]

    gl_scr[...] = _gla_log_alpha(z_ref, wa2_ref, ba2_ref)

    pair = min(LOCAL_PAIR, nsub)

    def local_body(ip, carry):
        heads = [slice(h * LANES, (h + 1) * LANES) for h in range(GLA_G)]
        subs = [ip * pair + p for p in range(pair)]
        rows = [pl.ds(pl.multiple_of(i * GLA_SUB, GLA_SUB), GLA_SUB) for i in subs]
        groups = [[(q_ref[r, cs] * scale, k_ref[r, cs], v_ref[r, cs], gl_scr[r, cs],
                    range(GLA_SUB)) for cs in heads] for r in rows]
        for i, r, res in zip(subs, rows, _gla_local_many(groups, consts)):
            for h, (qe, o, kv, dec) in enumerate(res):
                cs = heads[h]
                qe_scr[r, cs] = qe
                o_scr[r, cs] = o
                kv_scr[i, h] = kv
                dec_scr[i, h] = dec
        return carry

    lax.fori_loop(0, nsub // pair, local_body, 0, unroll=min(4, nsub // pair))

    def state_body(i, carry):
        rows = pl.ds(pl.multiple_of(i * GLA_SUB, GLA_SUB), GLA_SUB)
        for h in range(GLA_G):
            cs = slice(h * LANES, (h + 1) * LANES)
            s = s_scr[h]
            o_scr[rows, cs] += jnp.dot(qe_scr[rows, cs], s.astype(BF16),
                                       preferred_element_type=F32)
            s_scr[h] = dec_scr[i, h] * s + kv_scr[i, h]
        return carry

    lax.fori_loop(0, nsub, state_body, 0, unroll=min(8, nsub))
    for h in range(GLA_G):
        cs = slice(h * LANES, (h + 1) * LANES)
        out_ref[:, cs] = _gla_finish(o_scr[:, cs], go_ref[:, cs], gain_ref[:, cs]).astype(out_ref.dtype)

    @pl.when(t == nt - 1)
    def _():
        sout_ref[0] = s_scr[...]


def gla_long(proj, z, s0, wts, *, batch, seq, tt, casts=()):
    nt = seq // tt
    gw = GLA_G * LANES
    base = 2 * D_RNN // gw
    per = D_GLA // gw
    nhg = GLA_HEADS // GLA_G
    bsel = (lambda b: b) if s0.shape[0] == batch else (lambda b: 0)
    col = lambda which: pl.BlockSpec((tt, gw), lambda b, h, t: (b * nt + t, base + which * per + h))
    n_steps = batch * nhg * nt
    cast_specs = []
    for a in casts:
        rblocks = math.gcd(n_steps, a.shape[0] // SUB_BF16)
        cblocks = n_steps // rblocks
        blk = (a.shape[0] // rblocks, a.shape[1] // cblocks)
        assert blk[0] * rblocks == a.shape[0] and blk[1] * cblocks == a.shape[1]
        assert blk[0] % SUB_BF16 == 0 and blk[1] % LANES == 0
        cast_specs.append(pl.BlockSpec(
            blk, lambda b, h, t, cblocks=cblocks:
                (((b * nhg + h) * nt + t) // cblocks, ((b * nhg + h) * nt + t) % cblocks)))
    return pl.pallas_call(
        functools.partial(_gla_long_kernel, tt=tt, n_casts=len(casts)),
        out_shape=(jax.ShapeDtypeStruct((batch * seq, D_GLA), BF16),
                   jax.ShapeDtypeStruct((batch, GLA_HEADS, GLA_DK, GLA_DK), F32),
                   *[jax.ShapeDtypeStruct(a.shape, BF16) for a in casts]),
        grid=(batch, nhg, nt),
        in_specs=[col(0), col(1), col(2), col(3),
                  pl.BlockSpec((tt, LANES), lambda b, h, t: (b * nt + t, 0)),
                  pl.BlockSpec((LANES, gw), lambda b, h, t: (0, h)),
                  pl.BlockSpec((1, gw), lambda b, h, t: (0, h)),
                  pl.BlockSpec((1, gw), lambda b, h, t: (0, h)),
                  pl.BlockSpec((1, GLA_G, GLA_DK, GLA_DK), lambda b, h, t: (bsel(b), h, 0, 0)),
                  *cast_specs],
        out_specs=(pl.BlockSpec((tt, gw), lambda b, h, t: (b * nt + t, h)),
                   pl.BlockSpec((1, GLA_G, GLA_DK, GLA_DK), lambda b, h, t: (b, h, 0, 0)),
                   *cast_specs),
        scratch_shapes=[pltpu.VMEM((GLA_G, GLA_DK, GLA_DK), F32),
                        pltpu.VMEM((tt, gw), F32),
                        pltpu.VMEM((tt, gw), F32),
                        pltpu.VMEM((tt, gw), BF16),
                        pltpu.VMEM((tt // GLA_SUB, GLA_G, GLA_DK, GLA_DK), F32),
                        pltpu.VMEM((tt // GLA_SUB, GLA_G, GLA_DK, GLA_DK), F32)],
        compiler_params=_params(("parallel", "parallel", "arbitrary")),
        name="gla_long",
    )(proj, proj, proj, proj, z, wts["gla_wa2"], wts["gla_ba2"], wts["gla_head_norm"], s0, *casts)


def _gla_short_kernel(q_ref, k_ref, v_ref, go_ref, z_ref, wa2_ref, ba2_ref, gain_ref, s0_ref,
                      out_ref, sout_ref, *, seq):
    consts = _gla_consts()
    row = consts[2]
    scale = GLA_DK ** -0.5
    gl = _gla_log_alpha(z_ref, wa2_ref, ba2_ref)
    nseq = GLA_SUB // seq
    assert nseq == GLA_G
    nblk = q_ref.shape[0] // GLA_SUB
    keys = [(r, h) for r in range(nblk) for h in range(GLA_G)]
    groups = []
    for r, h in keys:
        rows = slice(r * GLA_SUB, (r + 1) * GLA_SUB)
        cs = slice(h * LANES, (h + 1) * LANES)
        q = q_ref[rows, cs] * scale
        k = k_ref[rows, cs]
        v = v_ref[rows, cs]
        g = gl[rows, cs]
        units = []
        for b in range(nseq):
            mine = (row >= b * seq) & (row < (b + 1) * seq)
            zero = lambda a, mine=mine: jnp.where(mine, a, 0.0)
            units.append((zero(q), zero(k), zero(v), zero(g), range(b * seq, (b + 1) * seq)))
        groups.append(units)
    local = dict(zip(keys, _gla_local_many(groups, consts)))
    inter = {(r, h): [jnp.dot(local[r, h][b][0], s0_ref[r * nseq + b, h].astype(BF16),
                              preferred_element_type=F32) for b in range(nseq)]
             for r, h in keys}
    for r, h in keys:
        rows = slice(r * GLA_SUB, (r + 1) * GLA_SUB)
        cs = slice(h * LANES, (h + 1) * LANES)
        o = jnp.zeros((GLA_SUB, LANES), F32)
        for b in range(nseq):
            _, o_b, kv, dec = local[r, h][b]
            o = o + o_b + inter[r, h][b]
            sout_ref[r * nseq + b, h] = dec * s0_ref[r * nseq + b, h] + kv
        out_ref[rows, cs] = _gla_finish(o, go_ref[rows, cs], gain_ref[:, cs]).astype(out_ref.dtype)


def gla_short(proj, z, s0, wts, *, nb, seq):
    gw = GLA_G * LANES
    per = D_GLA // gw
    rows = GLA_SHORT_BLOCKS * GLA_SUB
    bpb = rows // seq
    col = lambda which: pl.BlockSpec((rows, gw), lambda i, h: (i, which * per + h))
    return pl.pallas_call(
        functools.partial(_gla_short_kernel, seq=seq),
        out_shape=(jax.ShapeDtypeStruct((nb * seq, D_GLA), BF16),
                   jax.ShapeDtypeStruct((nb, GLA_HEADS, GLA_DK, GLA_DK), F32)),
        grid=(nb // bpb, GLA_HEADS // GLA_G),
        in_specs=[col(0), col(1), col(2), col(3),
                  pl.BlockSpec((rows, LANES), lambda i, h: (i, 0)),
                  pl.BlockSpec((LANES, gw), lambda i, h: (0, h)),
                  pl.BlockSpec((1, gw), lambda i, h: (0, h)),
                  pl.BlockSpec((1, gw), lambda i, h: (0, h)),
                  pl.BlockSpec((bpb, GLA_G, GLA_DK, GLA_DK), lambda i, h: (i, h, 0, 0))],
        out_specs=(pl.BlockSpec((rows, gw), lambda i, h: (i, h)),
                   pl.BlockSpec((bpb, GLA_G, GLA_DK, GLA_DK), lambda i, h: (i, h, 0, 0))),
        compiler_params=_params(("parallel", "parallel")),
        name="gla_short",
    )(proj, proj, proj, proj, z, wts["gla_wa2"], wts["gla_ba2"], wts["gla_head_norm"], s0)


def _ffn_kernel(xm_ref, xs_ref, xp_ref, wg_ref, wu_ref, cw_ref, cb_ref, bufs_ref,
                hp_ref, hs_ref, nbufp_ref, nbufs_ref, extp_scr, exts_scr, mhist_scr,
                *, steps, tiles_per_seq, nb):
    i = pl.program_id(1)
    (m_first, _), (s_first, _), (p_first, p_count) = steps
    tail = FFN_CONV_W - 1
    pad = SUB
    tm = xp_ref.shape[0]
    rows_s = xs_ref.shape[0]
    hist_s = tail * nb

    def conv(ext_ref, start, shift, rows):
        gc = cb_ref[...]
        for c in range(FFN_CONV_W):
            lo = start - (tail - c) * shift
            gc = gc + ext_ref[lo:lo + rows, :] * cw_ref[c:c + 1, :]
        return gc

    @pl.when(i == m_first)
    def _():
        g = jnp.dot(xm_ref[...], wg_ref[...], preferred_element_type=F32)
        mhist_scr[...] = g[g.shape[0] - tail:, :]

    @pl.when((i >= p_first) & (i < p_first + p_count))
    def _():
        @pl.when((i - p_first) % tiles_per_seq == 0)
        def _():
            extp_scr[pad - tail:pad, :] = mhist_scr[...]

        x = xp_ref[...]
        extp_scr[pad:pad + tm, :] = jnp.dot(x, wg_ref[...], preferred_element_type=F32)
        up = jnp.dot(x, wu_ref[...], preferred_element_type=F32)
        hp_ref[...] = (_gelu(conv(extp_scr, pad, 1, tm)) * up).astype(hp_ref.dtype)
        last = extp_scr[pad + tm - tail:pad + tm, :]
        extp_scr[pad - tail:pad, :] = last
        nbufp_ref[0] = last

    @pl.when(i == s_first)
    def _():
        x = xs_ref[...]
        exts_scr[0:hist_s, :] = bufs_ref[...]
        exts_scr[hist_s:hist_s + rows_s, :] = jnp.dot(x, wg_ref[...], preferred_element_type=F32)
        up = jnp.dot(x, wu_ref[...], preferred_element_type=F32)
        hs_ref[...] = (_gelu(conv(exts_scr, hist_s, nb, rows_s)) * up).astype(hs_ref.dtype)
        nbufs_ref[...] = exts_scr[rows_s:rows_s + hist_s, :]


def ffn_gate(xm, xs, xp, bufs, wts, *, tm, tn, seq_rows, nb):
    rows_p, rows_s = xp.shape[0], xs.shape[0]
    tail = FFN_CONV_W - 1
    steps, ni = _group_steps([(xm.shape[0], xm.shape[0]), (rows_s, rows_s), (rows_p, tm)])
    (_, _), (_, _), (p_first, p_count) = steps
    tps = seq_rows // tm
    ptile = lambda i: _tile_index(i, p_first, p_count)
    const = lambda shape: pl.BlockSpec(shape, lambda j, i: (0, 0), pipeline_mode=pl.Buffered(1))
    colblk = lambda r: pl.BlockSpec((r, tn), lambda j, i: (0, j))
    return pl.pallas_call(
        functools.partial(_ffn_kernel, steps=tuple(steps), tiles_per_seq=tps, nb=nb),
        out_shape=(jax.ShapeDtypeStruct((rows_p, D_FF), BF16),
                   jax.ShapeDtypeStruct((rows_s, D_FF), BF16),
                   jax.ShapeDtypeStruct((rows_p // seq_rows, tail, D_FF), F32),
                   jax.ShapeDtypeStruct((tail * nb, D_FF), F32)),
        grid=(D_FF // tn, ni),
        in_specs=[const(xm.shape),
                  const(xs.shape),
                  pl.BlockSpec((tm, D_MODEL), lambda j, i: (ptile(i), 0)),
                  colblk(D_MODEL), colblk(D_MODEL), colblk(FFN_CONV_W), colblk(1),
                  colblk(tail * nb)],
        out_specs=(pl.BlockSpec((tm, tn), lambda j, i: (ptile(i), j)),
                   colblk(rows_s),
                   pl.BlockSpec((1, tail, tn), lambda j, i: (ptile(i) // tps, 0, j)),
                   colblk(tail * nb)),
        scratch_shapes=[pltpu.VMEM((tm + SUB, tn), F32),
                        pltpu.VMEM((rows_s + tail * nb, tn), F32),
                        pltpu.VMEM((tail, tn), F32)],
        compiler_params=_params(("parallel", "arbitrary")),
        name="ffn_gate",
    )(xm, xs, xp, wts["w_gate"], wts["w_up"], wts["ffn_conv_w"], wts["ffn_conv_b"], bufs)


PROMPT_TM = 1024
IN_TN = 1024
OUT_TN = 512
DOWN_TM = 512
DOWN_TN = 512
FFN_TM = 1024
FFN_TN = 512
SEQ_TT = 256
NORM_TM = 512


def kernel(x_prompt, x_sample, state_rglru_conv, state_rglru_h, state_gla_S, state_ffn_conv,
           meta_tokens, norm_mix, w_in, rg_conv_w, rg_conv_b, rg_wa, rg_ba, rg_wx, rg_bx, rg_lambda,
           rg_out_norm, gla_wa2, gla_ba2, gla_head_norm, w_out, norm_ffn, w_gate, w_up,
           ffn_conv_w, ffn_conv_b, w_down, final_norm):
    batch, seq, _ = x_prompt.shape
    nb, sseq, _ = x_sample.shape
    row = lambda a: a.reshape(1, -1)
    w_in_b = w_in[0].astype(BF16)
    wts = {
        "w_in": w_in_b,
        "w_z": jnp.pad(w_in_b[:, PROJ_COLS:], ((0, 0), (0, LANES - GLA_RANK))),
        "rg_conv_w": rg_conv_w[0], "rg_conv_b": row(rg_conv_b[0]),
        "rg_wax": jnp.concatenate([rg_wa[0], rg_wx[0]], axis=-1).astype(BF16),
        "rg_ba": row(rg_ba[0]), "rg_bx": row(rg_bx[0]), "rg_lambda": row(rg_lambda[0]),
        "rg_out_norm": row(rg_out_norm[0]),
        "gla_wa2": jnp.pad(gla_wa2[0].astype(BF16), ((0, LANES - GLA_RANK), (0, 0))),
        "gla_ba2": row(gla_ba2[0]), "gla_head_norm": row(gla_head_norm[0]),
        "ffn_conv_w": ffn_conv_w[0], "ffn_conv_b": row(ffn_conv_b[0]),
    }

    def to_batch_major(a):
        return jnp.swapaxes(a.reshape(sseq, nb, -1), 0, 1).reshape(nb * sseq, -1)

    def to_time_major(a):
        return jnp.swapaxes(a.reshape(nb, sseq, -1), 0, 1).reshape(nb * sseq, -1)

    rows_p, rows_s = batch * seq, nb * sseq
    xs = [meta_tokens, to_time_major(x_sample), x_prompt.reshape(rows_p, D_MODEL)]
    tms = [N_META, rows_s, PROMPT_TM]
    norm_tms = [N_META, NORM_TM, NORM_TM]

    def norm_all(arrs, g, dtype):
        return [rmsnorm(a, g, dtype, t) for a, t in zip(arrs, norm_tms)]

    xn = norm_all(xs, norm_mix[0], BF16)
    proj_m, proj_s, proj_p = grouped_matmul(
        [dict(parts=[a], tm=t) for a, t in zip(xn, tms)], wts["w_in"], tn=IN_TN, n_cols=PROJ_COLS)
    z_m, z_s, z_p = grouped_matmul(
        [dict(parts=[a], tm=t) for a, t in zip(xn, tms)], wts["w_z"], tn=LANES, n_cols=LANES)

    zeros = lambda *s: jnp.zeros(s, F32)
    rnn_m, m_conv, m_h = rg_long(proj_m, zeros(1, CONV_W - 1, D_RNN), zeros(1, 1, D_RNN), wts,
                                 batch=1, seq=N_META, tt=N_META, reset_first=True)
    gla_m, m_s = gla_long(proj_m, z_m, zeros(1, GLA_HEADS, GLA_DK, GLA_DK), wts, batch=1,
                          seq=N_META, tt=N_META)
    rnn_p, p_conv, p_h = rg_long(proj_p, m_conv, m_h, wts, batch=batch, seq=seq, tt=SEQ_TT,
                                 reset_first=False)
    gla_p, p_s, wts["w_gate"], wts["w_up"], wts["w_down"] = gla_long(
        proj_p, z_p, m_s, wts, batch=batch, seq=seq, tt=SEQ_TT,
        casts=(w_gate.reshape(D_MODEL, D_FF), w_up.reshape(D_MODEL, D_FF),
               w_down.reshape(D_FF, D_MODEL)))
    rnn_s, s_conv_t, s_h = rg_short(proj_s, jnp.swapaxes(state_rglru_conv[0], 0, 1),
                                    state_rglru_h[0], wts, nb=nb, seq=sseq)
    gla_sb, s_s = gla_short(to_batch_major(proj_s[:, 2 * D_RNN:]), to_batch_major(z_s),
                            state_gla_S[0], wts, nb=nb, seq=sseq)
    gla_s = to_time_major(gla_sb)

    x1 = grouped_matmul(
        [dict(parts=[r, g], tm=t, res=x)
         for r, g, t, x in zip([rnn_m, rnn_s, rnn_p], [gla_m, gla_s, gla_p], tms, xs)],
        w_out, tn=OUT_TN, n_cols=D_MODEL, cast_w=True)
    xn2 = norm_all(x1, norm_ffn[0], BF16)
    tail = FFN_CONV_W - 1
    bufs = jnp.swapaxes(state_ffn_conv[0], 0, 1).reshape(tail * nb, D_FF)
    hid_p, hid_s, p_ffn, s_ffn_t = ffn_gate(xn2[0], xn2[1], xn2[2], bufs, wts,
                                            tm=FFN_TM, tn=FFN_TN, seq_rows=seq, nb=nb)
    (x2_s,) = grouped_matmul([dict(parts=[hid_s], tm=rows_s, res=x1[1])], wts["w_down"],
                             tn=DOWN_TN, n_cols=D_MODEL)
    (x2_p,) = grouped_matmul([dict(parts=[hid_p], tm=DOWN_TM, res=x1[2])], wts["w_down"],
                             tn=DOWN_TN, n_cols=D_MODEL)
    y_p = rmsnorm(x2_p, final_norm, F32, NORM_TM)
    y_s = to_batch_major(rmsnorm(x2_s, final_norm, F32, NORM_TM))

    return (y_p.reshape(batch, seq, D_MODEL), y_s.reshape(nb, sseq, D_MODEL),
            p_conv[None], p_h.reshape(1, batch, D_RNN), p_s[None], p_ffn[None],
            jnp.swapaxes(s_conv_t, 0, 1)[None], s_h[None], s_s[None],
            jnp.swapaxes(s_ffn_t.reshape(tail, nb, D_FF), 0, 1)[None])
```

```python
import functools
import math

import jax
import jax.numpy as jnp
from jax import lax
from jax.experimental import pallas as pl
from jax.experimental.pallas import tpu as pltpu

F32 = jnp.float32
BF16 = jnp.bfloat16

D_MODEL = 4096
N_META = 16
D_RNN = 2048
D_GLA = 2048
RG_BLOCKS = 16
RG_BLOCK = 128
CONV_W = 4
RG_C = 8.0
GLA_HEADS = 16
GLA_DK = 128
GLA_RANK = 16
GLA_GATE_TAU = 16.0
D_FF = 3 * D_MODEL
FFN_CONV_W = 3
EPS = 1e-6
PROJ_COLS = 2 * D_RNN + 4 * D_GLA

LANES = 128
SUB = 8
SUB_BF16 = 16
RG_SCAN_COLS = 512
VMEM_LIMIT = 60 * 1024 * 1024
GLA_SUB = 16
GLA_G = 4
LOCAL_PAIR = 2
GLA_SHORT_BLOCKS = 4


def _params(sem):
    return pltpu.CompilerParams(dimension_semantics=sem, vmem_limit_bytes=VMEM_LIMIT)


def _gelu(x):
    c = math.sqrt(2.0 / math.pi)
    return 0.5 * x * (1.0 + jnp.tanh(c * (x + 0.044715 * (x * x * x))))


def _softplus(x):
    return jnp.maximum(x, 0.0) + jnp.log1p(jnp.exp(-jnp.abs(x)))


def _sigmoid(x):
    return 0.5 * (1.0 + jnp.tanh(0.5 * x))


def _rmsnorm_kernel(x_ref, g_ref, o_ref):
    x = x_ref[...]
    ms = jnp.mean(x * x, axis=-1, keepdims=True)
    o_ref[...] = (x * lax.rsqrt(ms + EPS) * g_ref[...]).astype(o_ref.dtype)


def rmsnorm(x, g, out_dtype, tm):
    m, d = x.shape
    return pl.pallas_call(
        _rmsnorm_kernel,
        out_shape=jax.ShapeDtypeStruct((m, d), out_dtype),
        grid=(m // tm,),
        in_specs=[pl.BlockSpec((tm, d), lambda i: (i, 0)),
                  pl.BlockSpec((1, d), lambda i: (0, 0))],
        out_specs=pl.BlockSpec((tm, d), lambda i: (i, 0)),
        compiler_params=_params(("parallel",)),
        name="rmsnorm",
    )(x, g.reshape(1, d))


def _group_steps(groups):
    steps, first = [], 0
    for rows, tm in groups:
        steps.append((first, rows // tm))
        first += rows // tm
    return steps, first


def _tile_index(i, first, count):
    return jnp.clip(i - first, 0, count - 1)


def _gmm_kernel(*refs, layout, steps, cast_w):
    i = pl.program_id(1)
    pos = 0
    g_in = []
    for n_parts, has_res in layout:
        xs = refs[pos:pos + n_parts]
        pos += n_parts
        r = refs[pos] if has_res else None
        pos += int(has_res)
        g_in.append((xs, r))
    w_ref = refs[pos]
    outs = refs[pos + 1:pos + 1 + len(layout)]
    if cast_w:
        wsrc = refs[pos + 1 + len(layout)]

        @pl.when(i == 0)
        def _():
            wsrc[...] = w_ref[...].astype(BF16)
    else:
        wsrc = w_ref
    for (xs, r), o_ref, (first, count) in zip(g_in, outs, steps):
        @pl.when((i >= first) & (i < first + count))
        def _(xs=xs, r=r, o_ref=o_ref):
            acc, off = None, 0
            for x_ref in xs:
                kk = x_ref.shape[1]
                part = jnp.dot(x_ref[...], wsrc[off:off + kk, :], preferred_element_type=F32)
                acc = part if acc is None else acc + part
                off += kk
            o_ref[...] = acc if r is None else acc + r[...]


def grouped_matmul(groups, w, *, tn, n_cols, cast_w=False):
    kdim = sum(p.shape[1] for p in groups[0]["parts"])
    steps, ni = _group_steps([(g["parts"][0].shape[0], g["tm"]) for g in groups])
    in_specs, args, layout = [], [], []
    for g, (first, count) in zip(groups, steps):
        rows_of = lambda j, i, first=first, count=count: _tile_index(i, first, count)
        once = dict(pipeline_mode=pl.Buffered(1)) if count == 1 else {}
        for p in g["parts"]:
            in_specs.append(pl.BlockSpec((g["tm"], p.shape[1]),
                                         lambda j, i, rows_of=rows_of: (rows_of(j, i), 0), **once))
            args.append(p)
        if g.get("res") is not None:
            in_specs.append(pl.BlockSpec((g["tm"], tn),
                                         lambda j, i, rows_of=rows_of: (rows_of(j, i), j)))
            args.append(g["res"])
        layout.append((len(g["parts"]), g.get("res") is not None))
    scratch = [pltpu.VMEM((kdim, tn), BF16)] if cast_w else []
    if w.ndim == 3:
        in_specs.append(pl.BlockSpec((None, kdim, tn), lambda j, i: (0, 0, j)))
    else:
        in_specs.append(pl.BlockSpec((kdim, tn), lambda j, i: (0, j)))
    args.append(w)
    out_specs = [pl.BlockSpec((g["tm"], tn),
                              lambda j, i, first=first, count=count:
                                  (_tile_index(i, first, count), j))
                 for g, (first, count) in zip(groups, steps)]
    return pl.pallas_call(
        functools.partial(_gmm_kernel, layout=tuple(layout), steps=tuple(steps), cast_w=cast_w),
        out_shape=[jax.ShapeDtypeStruct((g["parts"][0].shape[0], n_cols), F32) for g in groups],
        grid=(n_cols // tn, ni),
        in_specs=in_specs,
        out_specs=out_specs,
        scratch_shapes=scratch,
        compiler_params=_params(("parallel", "arbitrary")),
        name="grouped_matmul",
    )(*args)


def _rg_gates(xc, n, wax_ref, ba_ref, bx_ref, sp_row):
    cs = slice(n * RG_BLOCK, (n + 1) * RG_BLOCK)
    pre = jnp.dot(xc.astype(BF16), wax_ref[n], preferred_element_type=F32)
    r = _sigmoid(pre[:, :RG_BLOCK] + ba_ref[:, cs])
    i = _sigmoid(pre[:, RG_BLOCK:] + bx_ref[:, cs])
    log_a = -RG_C * r * sp_row[:, cs]
    a = jnp.exp(log_a)
    th = jnp.tanh(log_a)
    p = -2.0 * th
    mult = jnp.where(p > 0.0, p * lax.rsqrt(p * (1.0 - th)), 0.0)
    return a, mult, i


def _rg_long_kernel(xr_ref, yr_ref, cb_ref, h0_ref, cw_ref, cbias_ref, wax_ref, ba_ref, bx_ref,
                    lam_ref, gn_ref, out_ref, nconv_ref, hlast_ref,
                    ext_scr, a_scr, u_scr, h_scr, *, tt, reset_first):
    t = pl.program_id(1)
    nt = pl.num_programs(1)
    pad = SUB
    tail = CONV_W - 1

    @pl.when(t == 0)
    def _():
        ext_scr[0:pad - tail, :] = jnp.zeros((pad - tail, D_RNN), F32)
        ext_scr[pad - tail:pad, :] = cb_ref[0]
        h_scr[...] = h0_ref[0]

    ext_scr[pad:pad + tt, :] = xr_ref[...]
    sp_row = _softplus(-lam_ref[...])
    row = lax.broadcasted_iota(jnp.int32, (tt, RG_BLOCK), 0)
    first_row = jnp.where(t == 0, 0, -1)
    for n in range(RG_BLOCKS):
        cs = slice(n * RG_BLOCK, (n + 1) * RG_BLOCK)
        x = ext_scr[pad:pad + tt, cs]
        hist = ext_scr[0:pad, cs]
        xc = cbias_ref[:, cs]
        for i in range(CONV_W):
            s = tail - i
            if s == 0:
                xs = x
            else:
                xs = pltpu.roll(x, s, 0)
                head = jnp.where(row[:SUB] < s, pltpu.roll(hist, s, 0), xs[:SUB])
                xs = jnp.concatenate([head, xs[SUB:]], axis=0) if tt > SUB else head
            xc = xc + xs * cw_ref[i:i + 1, cs]
        a, mult, gate_i = _rg_gates(xc, n, wax_ref, ba_ref, bx_ref, sp_row)
        if reset_first:
            mult = jnp.where(row == first_row, 1.0, mult)
        a_scr[:, cs] = a
        u_scr[:, cs] = mult * gate_i * xc

    row8 = lax.broadcasted_iota(jnp.int32, (SUB, RG_SCAN_COLS), 0)
    ncol = D_RNN // RG_SCAN_COLS

    def scan_body(j, hs):
        r0 = pl.multiple_of(j * SUB, SUB)
        new = []
        for c in range(ncol):
            cs = slice(c * RG_SCAN_COLS, (c + 1) * RG_SCAN_COLS)
            a = a_scr[pl.ds(r0, SUB), cs]
            u = u_scr[pl.ds(r0, SUB), cs]
            for s in (1, 2, 4):
                a_sh = jnp.where(row8 >= s, pltpu.roll(a, s, 0), 1.0)
                u_sh = jnp.where(row8 >= s, pltpu.roll(u, s, 0), 0.0)
                u = a * u_sh + u
                a = a * a_sh
            h = a * hs[c] + u
            u_scr[pl.ds(r0, SUB), cs] = h
            new.append(h[SUB - 1:SUB, :])
        return tuple(new)

    h_in = tuple(h_scr[:, c * RG_SCAN_COLS:(c + 1) * RG_SCAN_COLS] for c in range(ncol))
    h_fin = lax.fori_loop(0, tt // SUB, scan_body, h_in)
    for c in range(ncol):
        h_scr[:, c * RG_SCAN_COLS:(c + 1) * RG_SCAN_COLS] = h_fin[c]

    y = u_scr[...] * _gelu(yr_ref[...])
    ms = jnp.mean(y * y, axis=-1, keepdims=True)
    out_ref[...] = (y * lax.rsqrt(ms + EPS) * gn_ref[...]).astype(out_ref.dtype)
    ext_scr[pad - tail:pad, :] = ext_scr[pad + tt - tail:pad + tt, :]

    @pl.when(t == nt - 1)
    def _():
        nconv_ref[0] = ext_scr[pad + tt - tail:pad + tt, :]
        hlast_ref[0] = h_scr[...]


def rg_long(proj, conv_buf, h0, wts, *, batch, seq, tt, reset_first):
    nt = seq // tt
    bsel = (lambda b: b) if conv_buf.shape[0] == batch else (lambda b: 0)
    vec = lambda r: pl.BlockSpec((r, D_RNN), lambda b, t: (0, 0))
    return pl.pallas_call(
        functools.partial(_rg_long_kernel, tt=tt, reset_first=reset_first),
        out_shape=(jax.ShapeDtypeStruct((batch * seq, D_RNN), BF16),
                   jax.ShapeDtypeStruct((batch, CONV_W - 1, D_RNN), F32),
                   jax.ShapeDtypeStruct((batch, 1, D_RNN), F32)),
        grid=(batch, nt),
        in_specs=[pl.BlockSpec((tt, D_RNN), lambda b, t: (b * nt + t, 0)),
                  pl.BlockSpec((tt, D_RNN), lambda b, t: (b * nt + t, 1)),
                  pl.BlockSpec((1, CONV_W - 1, D_RNN), lambda b, t: (bsel(b), 0, 0)),
                  pl.BlockSpec((1, 1, D_RNN), lambda b, t: (bsel(b), 0, 0)),
                  vec(CONV_W), vec(1),
                  pl.BlockSpec((RG_BLOCKS, RG_BLOCK, 2 * RG_BLOCK), lambda b, t: (0, 0, 0)),
                  vec(1), vec(1), vec(1), vec(1)],
        out_specs=(pl.BlockSpec((tt, D_RNN), lambda b, t: (b * nt + t, 0)),
                   pl.BlockSpec((1, CONV_W - 1, D_RNN), lambda b, t: (b, 0, 0)),
                   pl.BlockSpec((1, 1, D_RNN), lambda b, t: (b, 0, 0))),
        scratch_shapes=[pltpu.VMEM((tt + SUB, D_RNN), F32),
                        pltpu.VMEM((tt, D_RNN), F32),
                        pltpu.VMEM((tt, D_RNN), F32),
                        pltpu.VMEM((1, D_RNN), F32)],
        compiler_params=_params(("parallel", "arbitrary")),
        name="rg_long",
    )(proj, proj, conv_buf, h0, wts["rg_conv_w"], wts["rg_conv_b"], wts["rg_wax"],
      wts["rg_ba"], wts["rg_bx"], wts["rg_lambda"], wts["rg_out_norm"])


def _rg_short_kernel(xr_ref, yr_ref, cb_ref, h0_ref, cw_ref, cbias_ref, wax_ref, ba_ref, bx_ref,
                     lam_ref, gn_ref, out_ref, nconv_ref, hlast_ref, y_scr, *, nb, seq):
    sp_row = _softplus(-lam_ref[...])
    tail = CONV_W - 1
    for n in range(RG_BLOCKS):
        cs = slice(n * RG_BLOCK, (n + 1) * RG_BLOCK)
        ext = [cb_ref[i, :, cs] for i in range(tail)]
        ext += [xr_ref[t * nb:(t + 1) * nb, cs] for t in range(seq)]
        h = h0_ref[:, cs]
        for t in range(seq):
            xc = cbias_ref[:, cs]
            for i in range(CONV_W):
                xc = xc + ext[t + i] * cw_ref[i:i + 1, cs]
            a, mult, gate_i = _rg_gates(xc, n, wax_ref, ba_ref, bx_ref, sp_row)
            h = a * h + mult * gate_i * xc
            y_scr[t, :, cs] = h * _gelu(yr_ref[t * nb:(t + 1) * nb, cs])
        hlast_ref[:, cs] = h
        for i in range(tail):
            nconv_ref[i, :, cs] = ext[seq + i]
    for t in range(seq):
        y = y_scr[t]
        ms = jnp.mean(y * y, axis=-1, keepdims=True)
        out_ref[t * nb:(t + 1) * nb, :] = (y * lax.rsqrt(ms + EPS) * gn_ref[...]).astype(out_ref.dtype)


def rg_short(proj, conv_buf_t, h0, wts, *, nb, seq):
    rows = nb * seq
    vec = lambda r: pl.BlockSpec((r, D_RNN), lambda i: (0, 0))
    return pl.pallas_call(
        functools.partial(_rg_short_kernel, nb=nb, seq=seq),
        out_shape=(jax.ShapeDtypeStruct((rows, D_RNN), BF16),
                   jax.ShapeDtypeStruct((CONV_W - 1, nb, D_RNN), F32),
                   jax.ShapeDtypeStruct((nb, D_RNN), F32)),
        grid=(1,),
        in_specs=[pl.BlockSpec((rows, D_RNN), lambda i: (0, 0)),
                  pl.BlockSpec((rows, D_RNN), lambda i: (0, 1)),
                  pl.BlockSpec((CONV_W - 1, nb, D_RNN), lambda i: (0, 0, 0)),
                  pl.BlockSpec((nb, D_RNN), lambda i: (0, 0)),
                  vec(CONV_W), vec(1),
                  pl.BlockSpec((RG_BLOCKS, RG_BLOCK, 2 * RG_BLOCK), lambda i: (0, 0, 0)),
                  vec(1), vec(1), vec(1), vec(1)],
        out_specs=(pl.BlockSpec((rows, D_RNN), lambda i: (0, 0)),
                   pl.BlockSpec((CONV_W - 1, nb, D_RNN), lambda i: (0, 0, 0)),
                   pl.BlockSpec((nb, D_RNN), lambda i: (0, 0))),
        scratch_shapes=[pltpu.VMEM((seq, nb, D_RNN), F32)],
        compiler_params=_params(("arbitrary",)),
        name="rg_short",
    )(proj, proj, conv_buf_t, h0, wts["rg_conv_w"], wts["rg_conv_b"], wts["rg_wax"],
      wts["rg_ba"], wts["rg_bx"], wts["rg_lambda"], wts["rg_out_norm"])


def _gla_consts():
    er = lax.broadcasted_iota(jnp.int32, (LANES, LANES), 0)
    ec = lax.broadcasted_iota(jnp.int32, (LANES, LANES), 1)
    eye = jnp.where(er == ec, 1.0, 0.0).astype(BF16)
    ones = jnp.ones((LANES, LANES), BF16)
    row = lax.broadcasted_iota(jnp.int32, (GLA_SUB, LANES), 0)
    row8 = lax.broadcasted_iota(jnp.int32, (SUB, LANES), 0)
    pr = lax.broadcasted_iota(jnp.int32, (2 * GLA_SUB, LANES), 0)
    pick = jnp.where((pr == GLA_SUB) | (pr == GLA_SUB + 1), 1.0, 0.0).astype(BF16)
    return eye, ones, row, row8, pick


def _gla_local_many(groups, consts):
    gens = [_gla_local_levels(units, consts) for units in groups]
    results = [None] * len(gens)
    live = list(range(len(gens)))
    while live:
        for idx in list(live):
            try:
                next(gens[idx])
            except StopIteration as done:
                results[idx] = done.value
                live.remove(idx)
    return results


def _gla_local(units, consts):
    return _gla_local_many([units], consts)[0]


def _gla_local_levels(units, consts):
    eye, ones, row, row8, pick = consts
    assert len(units) == GLA_G

    def scan8(x):
        for s in (1, 2, 4):
            x = x + jnp.where(row8 >= s, pltpu.roll(x, s, 0), 0.0)
        return x

    halves = (slice(0, SUB), slice(SUB, GLA_SUB))
    nt = (((1,), (1,)), ((), ()))
    qes, xs, qas, kbs, cum_all = [], [], [], [], []
    for q, k, v, g, j_range in units:
        cum_lo = scan8(g[halves[0]])
        edge = cum_lo[SUB - 1:SUB, :]
        cum_hi = scan8(g[halves[1]]) + edge
        cums = (cum_lo, cum_hi)
        cum_all.append(cums)
        last = cum_hi[SUB - 1:SUB, :]
        qes.append(jnp.concatenate([q[h] * jnp.exp(c) for h, c in zip(halves, cums)],
                                   axis=0).astype(BF16))
        qas.append(q[halves[1]] * jnp.exp(cum_hi - edge))
        kbs.append(k[halves[0]] * jnp.exp(edge - cum_lo))
        ke = jnp.concatenate([k[h] * jnp.exp(last - c) for h, c in zip(halves, cums)], axis=0)
        el = jnp.exp(last)
        e_hi = el.astype(BF16).astype(F32)
        e_lo = el - e_hi
        extra = jnp.where(row == 0, e_hi, jnp.where(row == 1, e_lo, 0.0))
        xs.append(jnp.concatenate([ke, extra], axis=0).astype(BF16))
    crossing = [u[4][0] < SUB <= u[4][-1] for u in units]
    sc = None
    if any(crossing):
        qa = jnp.concatenate(qas, axis=0).astype(BF16)
        kb = jnp.concatenate(kbs, axis=0).astype(BF16)
        sc = lax.dot_general(qa, kb, nt, preferred_element_type=F32)
    xt = lax.dot_general(eye, jnp.concatenate(xs, axis=0), nt, preferred_element_type=F32)
    yield
    o_halves = []
    for (q, k, v, g, j_range), cums in zip(units, cum_all):
        ps = []
        for j in j_range:
            h, c, jj = halves[j // SUB], cums[j // SUB], j % SUB
            w = jnp.exp(jnp.where(row8 >= jj, c - c[jj:jj + 1, :], -jnp.inf))
            ps.append(q[h] * w * k[j:j + 1, :])
        att = jnp.dot(jnp.concatenate(ps, axis=0).astype(BF16), ones, preferred_element_type=F32)
        o_half = [None, None]
        for idx, j in enumerate(j_range):
            term = att[idx * SUB:(idx + 1) * SUB, :] * v[j:j + 1, :]
            o_half[j // SUB] = term if o_half[j // SUB] is None else o_half[j // SUB] + term
        o_halves.append(o_half)
    yield
    cross = None
    if sc is not None:
        n = GLA_G * SUB
        ur = lax.broadcasted_iota(jnp.int32, (n, n), 0) // SUB
        uc = lax.broadcasted_iota(jnp.int32, (n, n), 1) // SUB
        sc = jnp.where(ur == uc, sc, 0.0).astype(BF16)
        v_lo = jnp.concatenate([u[2][halves[0]] for u in units], axis=0).astype(BF16)
        cross = jnp.dot(sc, v_lo, preferred_element_type=F32)
    xt = xt.astype(BF16)
    zblk = jnp.zeros((2 * GLA_SUB, LANES), BF16)
    wrows = []
    for u, (_, _, v, _, _) in enumerate(units):
        vpad = jnp.concatenate([v, jnp.zeros_like(v)], axis=0).astype(BF16)
        wrows.append(jnp.concatenate([zblk] * (2 * u) + [vpad, pick]
                                     + [zblk] * (2 * (GLA_G - 1 - u)), axis=1))
    kd = jnp.dot(xt, jnp.concatenate(wrows, axis=0), preferred_element_type=F32)
    out = []
    for u, o_half in enumerate(o_halves):
        if cross is not None and crossing[u]:
            o_half[1] = o_half[1] + cross[u * SUB:(u + 1) * SUB, :]
        o = jnp.concatenate([jnp.zeros((SUB, LANES), F32) if t is None else t for t in o_half],
                            axis=0)
        kv = kd[:, (2 * u) * LANES:(2 * u + 1) * LANES]
        dec = kd[:, (2 * u + 1) * LANES:(2 * u + 2) * LANES]
        out.append((qes[u], o, kv, dec))
    return out


def _gla_log_alpha(z_ref, wa2_ref, ba2_ref):
    zz = jnp.dot(z_ref[...].astype(BF16), wa2_ref[...], preferred_element_type=F32) + ba2_ref[...]
    return (jnp.minimum(zz, 0.0) - jnp.log(1.0 + jnp.exp(-jnp.abs(zz)))) * (1.0 / GLA_GATE_TAU)


def _gla_finish(o, go, gain):
    ms = jnp.mean(o * o, axis=-1, keepdims=True)
    on = o * lax.rsqrt(ms + EPS) * gain
    return on * (go * _sigmoid(go))


def _gla_long_kernel(q_ref, k_ref, v_ref, go_ref, z_ref, wa2_ref, ba2_ref, gain_ref, s0_ref,
                     *rest, tt, n_casts):
    cast_in = rest[:n_casts]
    out_ref, sout_ref = rest[n_casts:n_casts + 2]
    cast_out = rest[n_casts + 2:2 * n_casts + 2]
    s_scr, gl_scr, o_scr, qe_scr, kv_scr, dec_scr = rest[2 * n_casts + 2:]
    t = pl.program_id(2)
    nt = pl.num_programs(2)
    nsub = tt // GLA_SUB
    consts = _gla_consts()
    scale = GLA_DK ** -0.5

    for src, dst in zip(cast_in, cast_out):
        dst[...] = src[...].astype(BF16)

    @pl.when(t == 0)
    def _():
        s_scr[...] = s0_ref[0]

    gl_scr[...] = _gla_log_alpha(z_ref, wa2_ref, ba2_ref)

    pair = min(LOCAL_PAIR, nsub)

    def local_body(ip, carry):
        heads = [slice(h * LANES, (h + 1) * LANES) for h in range(GLA_G)]
        subs = [ip * pair + p for p in range(pair)]
        rows = [pl.ds(pl.multiple_of(i * GLA_SUB, GLA_SUB), GLA_SUB) for i in subs]
        groups = [[(q_ref[r, cs] * scale, k_ref[r, cs], v_ref[r, cs], gl_scr[r, cs],
                    range(GLA_SUB)) for cs in heads] for r in rows]
        for i, r, res in zip(subs, rows, _gla_local_many(groups, consts)):
            for h, (qe, o, kv, dec) in enumerate(res):
                cs = heads[h]
                qe_scr[r, cs] = qe
                o_scr[r, cs] = o
                kv_scr[i, h] = kv
                dec_scr[i, h] = dec
        return carry

    lax.fori_loop(0, nsub // pair, local_body, 0, unroll=min(8, nsub // pair))

    def state_body(i, carry):
        rows = pl.ds(pl.multiple_of(i * GLA_SUB, GLA_SUB), GLA_SUB)
        for h in range(GLA_G):
            cs = slice(h * LANES, (h + 1) * LANES)
            s = s_scr[h]
            o_scr[rows, cs] += jnp.dot(qe_scr[rows, cs], s.astype(BF16),
                                       preferred_element_type=F32)
            s_scr[h] = dec_scr[i, h] * s + kv_scr[i, h]
        return carry

    lax.fori_loop(0, nsub, state_body, 0, unroll=min(8, nsub))
    for h in range(GLA_G):
        cs = slice(h * LANES, (h + 1) * LANES)
        out_ref[:, cs] = _gla_finish(o_scr[:, cs], go_ref[:, cs], gain_ref[:, cs]).astype(out_ref.dtype)

    @pl.when(t == nt - 1)
    def _():
        sout_ref[0] = s_scr[...]


def gla_long(proj, z, s0, wts, *, batch, seq, tt, casts=()):
    nt = seq // tt
    gw = GLA_G * LANES
    base = 2 * D_RNN // gw
    per = D_GLA // gw
    nhg = GLA_HEADS // GLA_G
    bsel = (lambda b: b) if s0.shape[0] == batch else (lambda b: 0)
    col = lambda which: pl.BlockSpec((tt, gw), lambda b, h, t: (b * nt + t, base + which * per + h))
    n_steps = batch * nhg * nt
    cast_specs = []
    for a in casts:
        rblocks = math.gcd(n_steps, a.shape[0] // SUB_BF16)
        cblocks = n_steps // rblocks
        blk = (a.shape[0] // rblocks, a.shape[1] // cblocks)
        assert blk[0] * rblocks == a.shape[0] and blk[1] * cblocks == a.shape[1]
        assert blk[0] % SUB_BF16 == 0 and blk[1] % LANES == 0
        cast_specs.append(pl.BlockSpec(
            blk, lambda b, h, t, cblocks=cblocks:
                (((b * nhg + h) * nt + t) // cblocks, ((b * nhg + h) * nt + t) % cblocks)))
    return pl.pallas_call(
        functools.partial(_gla_long_kernel, tt=tt, n_casts=len(casts)),
        out_shape=(jax.ShapeDtypeStruct((batch * seq, D_GLA), BF16),
                   jax.ShapeDtypeStruct((batch, GLA_HEADS, GLA_DK, GLA_DK), F32),
                   *[jax.ShapeDtypeStruct(a.shape, BF16) for a in casts]),
        grid=(batch, nhg, nt),
        in_specs=[col(0), col(1), col(2), col(3),
                  pl.BlockSpec((tt, LANES), lambda b, h, t: (b * nt + t, 0)),
                  pl.BlockSpec((LANES, gw), lambda b, h, t: (0, h)),
                  pl.BlockSpec((1, gw), lambda b, h, t: (0, h)),
                  pl.BlockSpec((1, gw), lambda b, h, t: (0, h)),
                  pl.BlockSpec((1, GLA_G, GLA_DK, GLA_DK), lambda b, h, t: (bsel(b), h, 0, 0)),
                  *cast_specs],
        out_specs=(pl.BlockSpec((tt, gw), lambda b, h, t: (b * nt + t, h)),
                   pl.BlockSpec((1, GLA_G, GLA_DK, GLA_DK), lambda b, h, t: (b, h, 0, 0)),
                   *cast_specs),
        scratch_shapes=[pltpu.VMEM((GLA_G, GLA_DK, GLA_DK), F32),
                        pltpu.VMEM((tt, gw), F32),
                        pltpu.VMEM((tt, gw), F32),
                        pltpu.VMEM((tt, gw), BF16),
                        pltpu.VMEM((tt // GLA_SUB, GLA_G, GLA_DK, GLA_DK), F32),
                        pltpu.VMEM((tt // GLA_SUB, GLA_G, GLA_DK, GLA_DK), F32)],
        compiler_params=_params(("parallel", "parallel", "arbitrary")),
        name="gla_long",
    )(proj, proj, proj, proj, z, wts["gla_wa2"], wts["gla_ba2"], wts["gla_head_norm"], s0, *casts)


def _gla_short_kernel(q_ref, k_ref, v_ref, go_ref, z_ref, wa2_ref, ba2_ref, gain_ref, s0_ref,
                      out_ref, sout_ref, *, seq):
    consts = _gla_consts()
    row = consts[2]
    scale = GLA_DK ** -0.5
    gl = _gla_log_alpha(z_ref, wa2_ref, ba2_ref)
    nseq = GLA_SUB // seq
    assert nseq == GLA_G
    nblk = q_ref.shape[0] // GLA_SUB
    keys = [(r, h) for r in range(nblk) for h in range(GLA_G)]
    groups = []
    for r, h in keys:
        rows = slice(r * GLA_SUB, (r + 1) * GLA_SUB)
        cs = slice(h * LANES, (h + 1) * LANES)
        q = q_ref[rows, cs] * scale
        k = k_ref[rows, cs]
        v = v_ref[rows, cs]
        g = gl[rows, cs]
        units = []
        for b in range(nseq):
            mine = (row >= b * seq) & (row < (b + 1) * seq)
            zero = lambda a, mine=mine: jnp.where(mine, a, 0.0)
            units.append((zero(q), zero(k), zero(v), zero(g), range(b * seq, (b + 1) * seq)))
        groups.append(units)
    local = dict(zip(keys, _gla_local_many(groups, consts)))
    inter = {(r, h): [jnp.dot(local[r, h][b][0], s0_ref[r * nseq + b, h].astype(BF16),
                              preferred_element_type=F32) for b in range(nseq)]
             for r, h in keys}
    for r, h in keys:
        rows = slice(r * GLA_SUB, (r + 1) * GLA_SUB)
        cs = slice(h * LANES, (h + 1) * LANES)
        o = jnp.zeros((GLA_SUB, LANES), F32)
        for b in range(nseq):
            _, o_b, kv, dec = local[r, h][b]
            o = o + o_b + inter[r, h][b]
            sout_ref[r * nseq + b, h] = dec * s0_ref[r * nseq + b, h] + kv
        out_ref[rows, cs] = _gla_finish(o, go_ref[rows, cs], gain_ref[:, cs]).astype(out_ref.dtype)


def gla_short(proj, z, s0, wts, *, nb, seq):
    gw = GLA_G * LANES
    per = D_GLA // gw
    rows = GLA_SHORT_BLOCKS * GLA_SUB
    bpb = rows // seq
    col = lambda which: pl.BlockSpec((rows, gw), lambda i, h: (i, which * per + h))
    return pl.pallas_call(
        functools.partial(_gla_short_kernel, seq=seq),
        out_shape=(jax.ShapeDtypeStruct((nb * seq, D_GLA), BF16),
                   jax.ShapeDtypeStruct((nb, GLA_HEADS, GLA_DK, GLA_DK), F32)),
        grid=(nb // bpb, GLA_HEADS // GLA_G),
        in_specs=[col(0), col(1), col(2), col(3),
                  pl.BlockSpec((rows, LANES), lambda i, h: (i, 0)),
                  pl.BlockSpec((LANES, gw), lambda i, h: (0, h)),
                  pl.BlockSpec((1, gw), lambda i, h: (0, h)),
                  pl.BlockSpec((1, gw), lambda i, h: (0, h)),
                  pl.BlockSpec((bpb, GLA_G, GLA_DK, GLA_DK), lambda i, h: (i, h, 0, 0))],
        out_specs=(pl.BlockSpec((rows, gw), lambda i, h: (i, h)),
                   pl.BlockSpec((bpb, GLA_G, GLA_DK, GLA_DK), lambda i, h: (i, h, 0, 0))),
        compiler_params=_params(("parallel", "parallel")),
        name="gla_short",
    )(proj, proj, proj, proj, z, wts["gla_wa2"], wts["gla_ba2"], wts["gla_head_norm"], s0)


def _ffn_kernel(xm_ref, xs_ref, xp_ref, wg_ref, wu_ref, cw_ref, cb_ref, bufs_ref,
                hp_ref, hs_ref, nbufp_ref, nbufs_ref, extp_scr, exts_scr, mhist_scr,
                *, steps, tiles_per_seq, nb):
    i = pl.program_id(1)
    (m_first, _), (s_first, _), (p_first, p_count) = steps
    tail = FFN_CONV_W - 1
    pad = SUB
    tm = xp_ref.shape[0]
    rows_s = xs_ref.shape[0]
    hist_s = tail * nb

    def conv(ext_ref, start, shift, rows):
        gc = cb_ref[...]
        for c in range(FFN_CONV_W):
            lo = start - (tail - c) * shift
            gc = gc + ext_ref[lo:lo + rows, :] * cw_ref[c:c + 1, :]
        return gc

    @pl.when(i == m_first)
    def _():
        g = jnp.dot(xm_ref[...], wg_ref[...], preferred_element_type=F32)
        mhist_scr[...] = g[g.shape[0] - tail:, :]
        extp_scr[0:pad - tail, :] = jnp.zeros((pad - tail, g.shape[1]), F32)

    @pl.when((i >= p_first) & (i < p_first + p_count))
    def _():
        @pl.when((i - p_first) % tiles_per_seq == 0)
        def _():
            extp_scr[pad - tail:pad, :] = mhist_scr[...]

        x = xp_ref[...]
        g = jnp.dot(x, wg_ref[...], preferred_element_type=F32)
        extp_scr[pad:pad + tm, :] = g
        up = jnp.dot(x, wu_ref[...], preferred_element_type=F32)
        hist = extp_scr[0:pad, :]
        row8 = lax.broadcasted_iota(jnp.int32, (SUB, g.shape[1]), 0)
        gc = cb_ref[...]
        for c in range(FFN_CONV_W):
            s = tail - c
            if s == 0:
                gs = g
            else:
                gs = pltpu.roll(g, s, 0)
                head = jnp.where(row8 < s, pltpu.roll(hist, s, 0), gs[:SUB])
                gs = jnp.concatenate([head, gs[SUB:]], axis=0)
            gc = gc + gs * cw_ref[c:c + 1, :]
        hp_ref[...] = (_gelu(gc) * up).astype(hp_ref.dtype)
        last = extp_scr[pad + tm - tail:pad + tm, :]
        extp_scr[pad - tail:pad, :] = last
        nbufp_ref[0] = last

    @pl.when(i == s_first)
    def _():
        x = xs_ref[...]
        exts_scr[0:hist_s, :] = bufs_ref[...]
        exts_scr[hist_s:hist_s + rows_s, :] = jnp.dot(x, wg_ref[...], preferred_element_type=F32)
        up = jnp.dot(x, wu_ref[...], preferred_element_type=F32)
        hs_ref[...] = (_gelu(conv(exts_scr, hist_s, nb, rows_s)) * up).astype(hs_ref.dtype)
        nbufs_ref[...] = exts_scr[rows_s:rows_s + hist_s, :]


def ffn_gate(xm, xs, xp, bufs, wts, *, tm, tn, seq_rows, nb):
    rows_p, rows_s = xp.shape[0], xs.shape[0]
    tail = FFN_CONV_W - 1
    steps, ni = _group_steps([(xm.shape[0], xm.shape[0]), (rows_s, rows_s), (rows_p, tm)])
    (_, _), (_, _), (p_first, p_count) = steps
    tps = seq_rows // tm
    ptile = lambda i: _tile_index(i, p_first, p_count)
    const = lambda shape: pl.BlockSpec(shape, lambda j, i: (0, 0), pipeline_mode=pl.Buffered(1))
    colblk = lambda r: pl.BlockSpec((r, tn), lambda j, i: (0, j))
    return pl.pallas_call(
        functools.partial(_ffn_kernel, steps=tuple(steps), tiles_per_seq=tps, nb=nb),
        out_shape=(jax.ShapeDtypeStruct((rows_p, D_FF), BF16),
                   jax.ShapeDtypeStruct((rows_s, D_FF), BF16),
                   jax.ShapeDtypeStruct((rows_p // seq_rows, tail, D_FF), F32),
                   jax.ShapeDtypeStruct((tail * nb, D_FF), F32)),
        grid=(D_FF // tn, ni),
        in_specs=[const(xm.shape),
                  const(xs.shape),
                  pl.BlockSpec((tm, D_MODEL), lambda j, i: (ptile(i), 0)),
                  colblk(D_MODEL), colblk(D_MODEL), colblk(FFN_CONV_W), colblk(1),
                  colblk(tail * nb)],
        out_specs=(pl.BlockSpec((tm, tn), lambda j, i: (ptile(i), j)),
                   colblk(rows_s),
                   pl.BlockSpec((1, tail, tn), lambda j, i: (ptile(i) // tps, 0, j)),
                   colblk(tail * nb)),
        scratch_shapes=[pltpu.VMEM((tm + SUB, tn), F32),
                        pltpu.VMEM((rows_s + tail * nb, tn), F32),
                        pltpu.VMEM((tail, tn), F32)],
        compiler_params=_params(("parallel", "arbitrary")),
        name="ffn_gate",
    )(xm, xs, xp, wts["w_gate"], wts["w_up"], wts["ffn_conv_w"], wts["ffn_conv_b"], bufs)


PROMPT_TM = 1024
IN_TN = 1024
OUT_TN = 512
DOWN_TM = 512
DOWN_TN = 512
FFN_TM = 1024
FFN_TN = 512
SEQ_TT = 256
NORM_TM = 512


def kernel(x_prompt, x_sample, state_rglru_conv, state_rglru_h, state_gla_S, state_ffn_conv,
           meta_tokens, norm_mix, w_in, rg_conv_w, rg_conv_b, rg_wa, rg_ba, rg_wx, rg_bx, rg_lambda,
           rg_out_norm, gla_wa2, gla_ba2, gla_head_norm, w_out, norm_ffn, w_gate, w_up,
           ffn_conv_w, ffn_conv_b, w_down, final_norm):
    batch, seq, _ = x_prompt.shape
    nb, sseq, _ = x_sample.shape
    row = lambda a: a.reshape(1, -1)
    w_in_b = w_in[0].astype(BF16)
    wts = {
        "w_in": w_in_b,
        "w_z": jnp.pad(w_in_b[:, PROJ_COLS:], ((0, 0), (0, LANES - GLA_RANK))),
        "rg_conv_w": rg_conv_w[0], "rg_conv_b": row(rg_conv_b[0]),
        "rg_wax": jnp.concatenate([rg_wa[0], rg_wx[0]], axis=-1).astype(BF16),
        "rg_ba": row(rg_ba[0]), "rg_bx": row(rg_bx[0]), "rg_lambda": row(rg_lambda[0]),
        "rg_out_norm": row(rg_out_norm[0]),
        "gla_wa2": jnp.pad(gla_wa2[0].astype(BF16), ((0, LANES - GLA_RANK), (0, 0))),
        "gla_ba2": row(gla_ba2[0]), "gla_head_norm": row(gla_head_norm[0]),
        "ffn_conv_w": ffn_conv_w[0], "ffn_conv_b": row(ffn_conv_b[0]),
    }

    def to_batch_major(a):
        return jnp.swapaxes(a.reshape(sseq, nb, -1), 0, 1).reshape(nb * sseq, -1)

    def to_time_major(a):
        return jnp.swapaxes(a.reshape(nb, sseq, -1), 0, 1).reshape(nb * sseq, -1)

    rows_p, rows_s = batch * seq, nb * sseq
    xs = [meta_tokens, to_time_major(x_sample), x_prompt.reshape(rows_p, D_MODEL)]
    tms = [N_META, rows_s, PROMPT_TM]
    norm_tms = [N_META, NORM_TM, NORM_TM]

    def norm_all(arrs, g, dtype):
        return [rmsnorm(a, g, dtype, t) for a, t in zip(arrs, norm_tms)]

    xn = norm_all(xs, norm_mix[0], BF16)
    proj_m, proj_s, proj_p = grouped_matmul(
        [dict(parts=[a], tm=t) for a, t in zip(xn, tms)], wts["w_in"], tn=IN_TN, n_cols=PROJ_COLS)
    z_m, z_s, z_p = grouped_matmul(
        [dict(parts=[a], tm=t) for a, t in zip(xn, tms)], wts["w_z"], tn=LANES, n_cols=LANES)

    zeros = lambda *s: jnp.zeros(s, F32)
    rnn_m, m_conv, m_h = rg_long(proj_m, zeros(1, CONV_W - 1, D_RNN), zeros(1, 1, D_RNN), wts,
                                 batch=1, seq=N_META, tt=N_META, reset_first=True)
    gla_m, m_s = gla_long(proj_m, z_m, zeros(1, GLA_HEADS, GLA_DK, GLA_DK), wts, batch=1,
                          seq=N_META, tt=N_META)
    rnn_p, p_conv, p_h = rg_long(proj_p, m_conv, m_h, wts, batch=batch, seq=seq, tt=SEQ_TT,
                                 reset_first=False)
    gla_p, p_s, wts["w_gate"], wts["w_up"], wts["w_down"] = gla_long(
        proj_p, z_p, m_s, wts, batch=batch, seq=seq, tt=SEQ_TT,
        casts=(w_gate.reshape(D_MODEL, D_FF), w_up.reshape(D_MODEL, D_FF),
               w_down.reshape(D_FF, D_MODEL)))
    rnn_s, s_conv_t, s_h = rg_short(proj_s, jnp.swapaxes(state_rglru_conv[0], 0, 1),
                                    state_rglru_h[0], wts, nb=nb, seq=sseq)
    gla_sb, s_s = gla_short(to_batch_major(proj_s[:, 2 * D_RNN:]), to_batch_major(z_s),
                            state_gla_S[0], wts, nb=nb, seq=sseq)
    gla_s = to_time_major(gla_sb)

    x1 = grouped_matmul(
        [dict(parts=[r, g], tm=t, res=x)
         for r, g, t, x in zip([rnn_m, rnn_s, rnn_p], [gla_m, gla_s, gla_p], tms, xs)],
        w_out, tn=OUT_TN, n_cols=D_MODEL, cast_w=True)
    xn2 = norm_all(x1, norm_ffn[0], BF16)
    tail = FFN_CONV_W - 1
    bufs = jnp.swapaxes(state_ffn_conv[0], 0, 1).reshape(tail * nb, D_FF)
    hid_p, hid_s, p_ffn, s_ffn_t = ffn_gate(xn2[0], xn2[1], xn2[2], bufs, wts,
                                            tm=FFN_TM, tn=FFN_TN, seq_rows=seq, nb=nb)
    (x2_s,) = grouped_matmul([dict(parts=[hid_s], tm=rows_s, res=x1[1])], wts["w_down"],
                             tn=DOWN_TN, n_cols=D_MODEL)
    (x2_p,) = grouped_matmul([dict(parts=[hid_p], tm=DOWN_TM, res=x1[2])], wts["w_down"],
                             tn=DOWN_TN, n_cols=D_MODEL)
    y_p = rmsnorm(x2_p, final_norm, F32, NORM_TM)
    y_s = to_batch_major(rmsnorm(x2_s, final_norm, F32, NORM_TM))

    return (y_p.reshape(batch, seq, D_MODEL), y_s.reshape(nb, sseq, D_MODEL),
            p_conv[None], p_h.reshape(1, batch, D_RNN), p_s[None], p_ffn[None],
            jnp.swapaxes(s_conv_t, 0, 1)[None], s_h[None], s_s[None],
            jnp.swapaxes(s_ffn_t.reshape(tail, nb, D_FF), 0, 1)[None])
```

```python
import functools
import math

import jax
import jax.numpy as jnp
from jax import lax
from jax.experimental import pallas as pl
from jax.experimental.pallas import tpu as pltpu

F32 = jnp.float32
BF16 = jnp.bfloat16

D_MODEL = 4096
N_META = 16
D_RNN = 2048
D_GLA = 2048
RG_BLOCKS = 16
RG_BLOCK = 128
CONV_W = 4
RG_C = 8.0
GLA_HEADS = 16
GLA_DK = 128
GLA_RANK = 16
GLA_GATE_TAU = 16.0
D_FF = 3 * D_MODEL
FFN_CONV_W = 3
EPS = 1e-6
PROJ_COLS = 2 * D_RNN + 4 * D_GLA

LANES = 128
SUB = 8
SUB_BF16 = 16
RG_SCAN_COLS = 512
VMEM_LIMIT = 60 * 1024 * 1024
GLA_SUB = 16
GLA_G = 4
LOCAL_PAIR = 2
GLA_SHORT_BLOCKS = 4


def _params(sem):
    return pltpu.CompilerParams(dimension_semantics=sem, vmem_limit_bytes=VMEM_LIMIT)


def _gelu(x):
    c = math.sqrt(2.0 / math.pi)
    return 0.5 * x * (1.0 + jnp.tanh(c * (x + 0.044715 * (x * x * x))))


def _softplus(x):
    return jnp.maximum(x, 0.0) + jnp.log1p(jnp.exp(-jnp.abs(x)))


def _sigmoid(x):
    return 0.5 * (1.0 + jnp.tanh(0.5 * x))


def _rmsnorm_kernel(x_ref, g_ref, o_ref):
    x = x_ref[...]
    ms = jnp.mean(x * x, axis=-1, keepdims=True)
    o_ref[...] = (x * lax.rsqrt(ms + EPS) * g_ref[...]).astype(o_ref.dtype)


def rmsnorm(x, g, out_dtype, tm):
    m, d = x.shape
    return pl.pallas_call(
        _rmsnorm_kernel,
        out_shape=jax.ShapeDtypeStruct((m, d), out_dtype),
        grid=(m // tm,),
        in_specs=[pl.BlockSpec((tm, d), lambda i: (i, 0)),
                  pl.BlockSpec((1, d), lambda i: (0, 0))],
        out_specs=pl.BlockSpec((tm, d), lambda i: (i, 0)),
        compiler_params=_params(("parallel",)),
        name="rmsnorm",
    )(x, g.reshape(1, d))


def _group_steps(groups):
    steps, first = [], 0
    for rows, tm in groups:
        steps.append((first, rows // tm))
        first += rows // tm
    return steps, first


def _tile_index(i, first, count):
    return jnp.clip(i - first, 0, count - 1)


def _gmm_kernel(*refs, layout, steps, cast_w):
    i = pl.program_id(1)
    pos = 0
    g_in = []
    for n_parts, has_res in layout:
        xs = refs[pos:pos + n_parts]
        pos += n_parts
        r = refs[pos] if has_res else None
        pos += int(has_res)
        g_in.append((xs, r))
    w_ref = refs[pos]
    outs = refs[pos + 1:pos + 1 + len(layout)]
    if cast_w:
        wsrc = refs[pos + 1 + len(layout)]

        @pl.when(i == 0)
        def _():
            wsrc[...] = w_ref[...].astype(BF16)
    else:
        wsrc = w_ref
    for (xs, r), o_ref, (first, count) in zip(g_in, outs, steps):
        @pl.when((i >= first) & (i < first + count))
        def _(xs=xs, r=r, o_ref=o_ref):
            acc, off = None, 0
            for x_ref in xs:
                kk = x_ref.shape[1]
                part = jnp.dot(x_ref[...], wsrc[off:off + kk, :], preferred_element_type=F32)
                acc = part if acc is None else acc + part
                off += kk
            o_ref[...] = acc if r is None else acc + r[...]


def grouped_matmul(groups, w, *, tn, n_cols, cast_w=False, fuse_w=False):
    kdim = sum(p.shape[1] for p in groups[0]["parts"])
    steps, ni = _group_steps([(g["parts"][0].shape[0], g["tm"]) for g in groups])
    in_specs, args, layout = [], [], []
    for g, (first, count) in zip(groups, steps):
        rows_of = lambda j, i, first=first, count=count: _tile_index(i, first, count)
        once = dict(pipeline_mode=pl.Buffered(1)) if count == 1 else {}
        for p in g["parts"]:
            in_specs.append(pl.BlockSpec((g["tm"], p.shape[1]),
                                         lambda j, i, rows_of=rows_of: (rows_of(j, i), 0), **once))
            args.append(p)
        if g.get("res") is not None:
            in_specs.append(pl.BlockSpec((g["tm"], tn),
                                         lambda j, i, rows_of=rows_of: (rows_of(j, i), j)))
            args.append(g["res"])
        layout.append((len(g["parts"]), g.get("res") is not None))
    scratch = [pltpu.VMEM((kdim, tn), BF16)] if cast_w else []
    if w.ndim == 3:
        in_specs.append(pl.BlockSpec((None, kdim, tn), lambda j, i: (0, 0, j)))
    else:
        in_specs.append(pl.BlockSpec((kdim, tn), lambda j, i: (0, j)))
    args.append(w)
    out_specs = [pl.BlockSpec((g["tm"], tn),
                              lambda j, i, first=first, count=count:
                                  (_tile_index(i, first, count), j))
                 for g, (first, count) in zip(groups, steps)]
    return pl.pallas_call(
        functools.partial(_gmm_kernel, layout=tuple(layout), steps=tuple(steps), cast_w=cast_w),
        out_shape=[jax.ShapeDtypeStruct((g["parts"][0].shape[0], n_cols), F32) for g in groups],
        grid=(n_cols // tn, ni),
        in_specs=in_specs,
        out_specs=out_specs,
        scratch_shapes=scratch,
        compiler_params=pltpu.CompilerParams(
            dimension_semantics=("parallel", "arbitrary"), vmem_limit_bytes=VMEM_LIMIT,
            allow_input_fusion=[False] * (len(args) - 1) + [fuse_w]),
        name="grouped_matmul",
    )(*args)


def _rg_gates(xc, n, wax_ref, ba_ref, bx_ref, sp_row):
    cs = slice(n * RG_BLOCK, (n + 1) * RG_BLOCK)
    pre = jnp.dot(xc.astype(BF16), wax_ref[n], preferred_element_type=F32)
    r = _sigmoid(pre[:, :RG_BLOCK] + ba_ref[:, cs])
    i = _sigmoid(pre[:, RG_BLOCK:] + bx_ref[:, cs])
    log_a = -RG_C * r * sp_row[:, cs]
    a = jnp.exp(log_a)
    th = jnp.tanh(log_a)
    p = -2.0 * th
    mult = jnp.where(p > 0.0, p * lax.rsqrt(p * (1.0 - th)), 0.0)
    return a, mult, i


def _rg_long_kernel(xr_ref, yr_ref, cb_ref, h0_ref, cw_ref, cbias_ref, wax_ref, ba_ref, bx_ref,
                    lam_ref, gn_ref, out_ref, nconv_ref, hlast_ref,
                    ext_scr, a_scr, u_scr, h_scr, *, tt, reset_first):
    t = pl.program_id(1)
    nt = pl.num_programs(1)
    pad = SUB
    tail = CONV_W - 1

    @pl.when(t == 0)
    def _():
        ext_scr[0:pad - tail, :] = jnp.zeros((pad - tail, D_RNN), F32)
        ext_scr[pad - tail:pad, :] = cb_ref[0]
        h_scr[...] = h0_ref[0]

    ext_scr[pad:pad + tt, :] = xr_ref[...]
    sp_row = _softplus(-lam_ref[...])
    row = lax.broadcasted_iota(jnp.int32, (tt, RG_BLOCK), 0)
    first_row = jnp.where(t == 0, 0, -1)
    for n in range(RG_BLOCKS):
        cs = slice(n * RG_BLOCK, (n + 1) * RG_BLOCK)
        x = ext_scr[pad:pad + tt, cs]
        hist = ext_scr[0:pad, cs]
        xc = cbias_ref[:, cs]
        for i in range(CONV_W):
            s = tail - i
            if s == 0:
                xs = x
            else:
                xs = pltpu.roll(x, s, 0)
                head = jnp.where(row[:SUB] < s, pltpu.roll(hist, s, 0), xs[:SUB])
                xs = jnp.concatenate([head, xs[SUB:]], axis=0) if tt > SUB else head
            xc = xc + xs * cw_ref[i:i + 1, cs]
        a, mult, gate_i = _rg_gates(xc, n, wax_ref, ba_ref, bx_ref, sp_row)
        if reset_first:
            mult = jnp.where(row == first_row, 1.0, mult)
        a_scr[:, cs] = a
        u_scr[:, cs] = mult * gate_i * xc

    row8 = lax.broadcasted_iota(jnp.int32, (SUB, RG_SCAN_COLS), 0)
    ncol = D_RNN // RG_SCAN_COLS

    def scan_body(j, hs):
        r0 = pl.multiple_of(j * SUB, SUB)
        new = []
        for c in range(ncol):
            cs = slice(c * RG_SCAN_COLS, (c + 1) * RG_SCAN_COLS)
            a = a_scr[pl.ds(r0, SUB), cs]
            u = u_scr[pl.ds(r0, SUB), cs]
            for s in (1, 2, 4):
                a_sh = jnp.where(row8 >= s, pltpu.roll(a, s, 0), 1.0)
                u_sh = jnp.where(row8 >= s, pltpu.roll(u, s, 0), 0.0)
                u = a * u_sh + u
                a = a * a_sh
            h = a * hs[c] + u
            u_scr[pl.ds(r0, SUB), cs] = h
            new.append(h[SUB - 1:SUB, :])
        return tuple(new)

    h_in = tuple(h_scr[:, c * RG_SCAN_COLS:(c + 1) * RG_SCAN_COLS] for c in range(ncol))
    h_fin = lax.fori_loop(0, tt // SUB, scan_body, h_in)
    for c in range(ncol):
        h_scr[:, c * RG_SCAN_COLS:(c + 1) * RG_SCAN_COLS] = h_fin[c]

    y = u_scr[...] * _gelu(yr_ref[...])
    ms = jnp.mean(y * y, axis=-1, keepdims=True)
    out_ref[...] = (y * lax.rsqrt(ms + EPS) * gn_ref[...]).astype(out_ref.dtype)
    ext_scr[pad - tail:pad, :] = ext_scr[pad + tt - tail:pad + tt, :]

    @pl.when(t == nt - 1)
    def _():
        nconv_ref[0] = ext_scr[pad + tt - tail:pad + tt, :]
        hlast_ref[0] = h_scr[...]


def rg_long(proj, conv_buf, h0, wts, *, batch, seq, tt, reset_first):
    nt = seq // tt
    bsel = (lambda b: b) if conv_buf.shape[0] == batch else (lambda b: 0)
    vec = lambda r: pl.BlockSpec((r, D_RNN), lambda b, t: (0, 0))
    return pl.pallas_call(
        functools.partial(_rg_long_kernel, tt=tt, reset_first=reset_first),
        out_shape=(jax.ShapeDtypeStruct((batch * seq, D_RNN), BF16),
                   jax.ShapeDtypeStruct((batch, CONV_W - 1, D_RNN), F32),
                   jax.ShapeDtypeStruct((batch, 1, D_RNN), F32)),
        grid=(batch, nt),
        in_specs=[pl.BlockSpec((tt, D_RNN), lambda b, t: (b * nt + t, 0)),
                  pl.BlockSpec((tt, D_RNN), lambda b, t: (b * nt + t, 1)),
                  pl.BlockSpec((1, CONV_W - 1, D_RNN), lambda b, t: (bsel(b), 0, 0)),
                  pl.BlockSpec((1, 1, D_RNN), lambda b, t: (bsel(b), 0, 0)),
                  vec(CONV_W), vec(1),
                  pl.BlockSpec((RG_BLOCKS, RG_BLOCK, 2 * RG_BLOCK), lambda b, t: (0, 0, 0)),
                  vec(1), vec(1), vec(1), vec(1)],
        out_specs=(pl.BlockSpec((tt, D_RNN), lambda b, t: (b * nt + t, 0)),
                   pl.BlockSpec((1, CONV_W - 1, D_RNN), lambda b, t: (b, 0, 0)),
                   pl.BlockSpec((1, 1, D_RNN), lambda b, t: (b, 0, 0))),
        scratch_shapes=[pltpu.VMEM((tt + SUB, D_RNN), F32),
                        pltpu.VMEM((tt, D_RNN), F32),
                        pltpu.VMEM((tt, D_RNN), F32),
                        pltpu.VMEM((1, D_RNN), F32)],
        compiler_params=_params(("parallel", "arbitrary")),
        name="rg_long",
    )(proj, proj, conv_buf, h0, wts["rg_conv_w"], wts["rg_conv_b"], wts["rg_wax"],
      wts["rg_ba"], wts["rg_bx"], wts["rg_lambda"], wts["rg_out_norm"])


def _rg_short_kernel(xr_ref, yr_ref, cb_ref, h0_ref, cw_ref, cbias_ref, wax_ref, ba_ref, bx_ref,
                     lam_ref, gn_ref, out_ref, nconv_ref, hlast_ref, y_scr, *, nb, seq):
    sp_row = _softplus(-lam_ref[...])
    tail = CONV_W - 1
    for n in range(RG_BLOCKS):
        cs = slice(n * RG_BLOCK, (n + 1) * RG_BLOCK)
        ext = [cb_ref[i, :, cs] for i in range(tail)]
        ext += [xr_ref[t * nb:(t + 1) * nb, cs] for t in range(seq)]
        h = h0_ref[:, cs]
        for t in range(seq):
            xc = cbias_ref[:, cs]
            for i in range(CONV_W):
                xc = xc + ext[t + i] * cw_ref[i:i + 1, cs]
            a, mult, gate_i = _rg_gates(xc, n, wax_ref, ba_ref, bx_ref, sp_row)
            h = a * h + mult * gate_i * xc
            y_scr[t, :, cs] = h * _gelu(yr_ref[t * nb:(t + 1) * nb, cs])
        hlast_ref[:, cs] = h
        for i in range(tail):
            nconv_ref[i, :, cs] = ext[seq + i]
    for t in range(seq):
        y = y_scr[t]
        ms = jnp.mean(y * y, axis=-1, keepdims=True)
        out_ref[t * nb:(t + 1) * nb, :] = (y * lax.rsqrt(ms + EPS) * gn_ref[...]).astype(out_ref.dtype)


def rg_short(proj, conv_buf_t, h0, wts, *, nb, seq):
    rows = nb * seq
    vec = lambda r: pl.BlockSpec((r, D_RNN), lambda i: (0, 0))
    return pl.pallas_call(
        functools.partial(_rg_short_kernel, nb=nb, seq=seq),
        out_shape=(jax.ShapeDtypeStruct((rows, D_RNN), BF16),
                   jax.ShapeDtypeStruct((CONV_W - 1, nb, D_RNN), F32),
                   jax.ShapeDtypeStruct((nb, D_RNN), F32)),
        grid=(1,),
        in_specs=[pl.BlockSpec((rows, D_RNN), lambda i: (0, 0)),
                  pl.BlockSpec((rows, D_RNN), lambda i: (0, 1)),
                  pl.BlockSpec((CONV_W - 1, nb, D_RNN), lambda i: (0, 0, 0)),
                  pl.BlockSpec((nb, D_RNN), lambda i: (0, 0)),
                  vec(CONV_W), vec(1),
                  pl.BlockSpec((RG_BLOCKS, RG_BLOCK, 2 * RG_BLOCK), lambda i: (0, 0, 0)),
                  vec(1), vec(1), vec(1), vec(1)],
        out_specs=(pl.BlockSpec((rows, D_RNN), lambda i: (0, 0)),
                   pl.BlockSpec((CONV_W - 1, nb, D_RNN), lambda i: (0, 0, 0)),
                   pl.BlockSpec((nb, D_RNN), lambda i: (0, 0))),
        scratch_shapes=[pltpu.VMEM((seq, nb, D_RNN), F32)],
        compiler_params=_params(("arbitrary",)),
        name="rg_short",
    )(proj, proj, conv_buf_t, h0, wts["rg_conv_w"], wts["rg_conv_b"], wts["rg_wax"],
      wts["rg_ba"], wts["rg_bx"], wts["rg_lambda"], wts["rg_out_norm"])


def _gla_consts():
    er = lax.broadcasted_iota(jnp.int32, (LANES, LANES), 0)
    ec = lax.broadcasted_iota(jnp.int32, (LANES, LANES), 1)
    eye = jnp.where(er == ec, 1.0, 0.0).astype(BF16)
    ones = jnp.ones((LANES, LANES), BF16)
    row = lax.broadcasted_iota(jnp.int32, (GLA_SUB, LANES), 0)
    row8 = lax.broadcasted_iota(jnp.int32, (SUB, LANES), 0)
    pr = lax.broadcasted_iota(jnp.int32, (2 * GLA_SUB, LANES), 0)
    pick = jnp.where((pr == GLA_SUB) | (pr == GLA_SUB + 1), 1.0, 0.0).astype(BF16)
    return eye, ones, row, row8, pick


def _gla_local_many(groups, consts):
    gens = [_gla_local_levels(units, consts) for units in groups]
    results = [None] * len(gens)
    live = list(range(len(gens)))
    while live:
        for idx in list(live):
            try:
                next(gens[idx])
            except StopIteration as done:
                results[idx] = done.value
                live.remove(idx)
    return results


def _gla_local(units, consts):
    return _gla_local_many([units], consts)[0]


def _gla_local_levels(units, consts):
    eye, ones, row, row8, pick = consts
    assert len(units) == GLA_G

    def scan8(x):
        for s in (1, 2, 4):
            x = x + jnp.where(row8 >= s, pltpu.roll(x, s, 0), 0.0)
        return x

    halves = (slice(0, SUB), slice(SUB, GLA_SUB))
    nt = (((1,), (1,)), ((), ()))
    qes, xs, qas, kbs, cum_all = [], [], [], [], []
    for q, k, v, g, j_range in units:
        cum_lo = scan8(g[halves[0]])
        edge = cum_lo[SUB - 1:SUB, :]
        cum_hi = scan8(g[halves[1]]) + edge
        cums = (cum_lo, cum_hi)
        cum_all.append(cums)
        last = cum_hi[SUB - 1:SUB, :]
        qes.append(jnp.concatenate([q[h] * jnp.exp(c) for h, c in zip(halves, cums)],
                                   axis=0).astype(BF16))
        qas.append(q[halves[1]] * jnp.exp(cum_hi - edge))
        kbs.append(k[halves[0]] * jnp.exp(edge - cum_lo))
        ke = jnp.concatenate([k[h] * jnp.exp(last - c) for h, c in zip(halves, cums)], axis=0)
        el = jnp.exp(last)
        e_hi = el.astype(BF16).astype(F32)
        e_lo = el - e_hi
        extra = jnp.where(row == 0, e_hi, jnp.where(row == 1, e_lo, 0.0))
        xs.append(jnp.concatenate([ke, extra], axis=0).astype(BF16))
    crossing = [u[4][0] < SUB <= u[4][-1] for u in units]
    sc = None
    if any(crossing):
        qa = jnp.concatenate(qas, axis=0).astype(BF16)
        kb = jnp.concatenate(kbs, axis=0).astype(BF16)
        sc = lax.dot_general(qa, kb, nt, preferred_element_type=F32)
    xt = lax.dot_general(eye, jnp.concatenate(xs, axis=0), nt, preferred_element_type=F32)
    yield
    o_halves = []
    for (q, k, v, g, j_range), cums in zip(units, cum_all):
        ps = []
        for j in j_range:
            h, c, jj = halves[j // SUB], cums[j // SUB], j % SUB
            w = jnp.exp(jnp.where(row8 >= jj, c - c[jj:jj + 1, :], -jnp.inf))
            ps.append(q[h] * w * k[j:j + 1, :])
        att = jnp.dot(jnp.concatenate(ps, axis=0).astype(BF16), ones, preferred_element_type=F32)
        o_half = [None, None]
        for idx, j in enumerate(j_range):
            term = att[idx * SUB:(idx + 1) * SUB, :] * v[j:j + 1, :]
            o_half[j // SUB] = term if o_half[j // SUB] is None else o_half[j // SUB] + term
        o_halves.append(o_half)
    yield
    cross = None
    if sc is not None:
        n = GLA_G * SUB
        ur = lax.broadcasted_iota(jnp.int32, (n, n), 0) // SUB
        uc = lax.broadcasted_iota(jnp.int32, (n, n), 1) // SUB
        sc = jnp.where(ur == uc, sc, 0.0).astype(BF16)
        v_lo = jnp.concatenate([u[2][halves[0]] for u in units], axis=0).astype(BF16)
        cross = jnp.dot(sc, v_lo, preferred_element_type=F32)
    xt = xt.astype(BF16)
    zblk = jnp.zeros((2 * GLA_SUB, LANES), BF16)
    wrows = []
    for u, (_, _, v, _, _) in enumerate(units):
        vpad = jnp.concatenate([v, jnp.zeros_like(v)], axis=0).astype(BF16)
        wrows.append(jnp.concatenate([zblk] * (2 * u) + [vpad, pick]
                                     + [zblk] * (2 * (GLA_G - 1 - u)), axis=1))
    kd = jnp.dot(xt, jnp.concatenate(wrows, axis=0), preferred_element_type=F32)
    out = []
    for u, o_half in enumerate(o_halves):
        if cross is not None and crossing[u]:
            o_half[1] = o_half[1] + cross[u * SUB:(u + 1) * SUB, :]
        o = jnp.concatenate([jnp.zeros((SUB, LANES), F32) if t is None else t for t in o_half],
                            axis=0)
        kv = kd[:, (2 * u) * LANES:(2 * u + 1) * LANES]
        dec = kd[:, (2 * u + 1) * LANES:(2 * u + 2) * LANES]
        out.append((qes[u], o, kv, dec))
    return out


def _gla_log_alpha(z_ref, wa2_ref, ba2_ref):
    zz = jnp.dot(z_ref[...].astype(BF16), wa2_ref[...], preferred_element_type=F32) + ba2_ref[...]
    return (jnp.minimum(zz, 0.0) - jnp.log(1.0 + jnp.exp(-jnp.abs(zz)))) * (1.0 / GLA_GATE_TAU)


def _gla_finish(o, go, gain):
    ms = jnp.mean(o * o, axis=-1, keepdims=True)
    on = o * lax.rsqrt(ms + EPS) * gain
    return on * (go * _sigmoid(go))


def _gla_long_kernel(q_ref, k_ref, v_ref, go_ref, z_ref, wa2_ref, ba2_ref, gain_ref, s0_ref,
                     *rest, tt, n_casts):
    cast_in = rest[:n_casts]
    out_ref, sout_ref = rest[n_casts:n_casts + 2]
    cast_out = rest[n_casts + 2:2 * n_casts + 2]
    s_scr, gl_scr, o_scr, qe_scr, kv_scr, dec_scr = rest[2 * n_casts + 2:]
    t = pl.program_id(2)
    nt = pl.num_programs(2)
    nsub = tt // GLA_SUB
    consts = _gla_consts()
    scale = GLA_DK ** -0.5

    for src, dst in zip(cast_in, cast_out):
        dst[...] = src[...].astype(BF16)

    @pl.when(t == 0)
    def _():
        s_scr[...] = s0_ref[0]

    gl_scr[...] = _gla_log_alpha(z_ref, wa2_ref, ba2_ref)

    pair = min(LOCAL_PAIR, nsub)

    def local_body(ip, carry):
        heads = [slice(h * LANES, (h + 1) * LANES) for h in range(GLA_G)]
        subs = [ip * pair + p for p in range(pair)]
        rows = [pl.ds(pl.multiple_of(i * GLA_SUB, GLA_SUB), GLA_SUB) for i in subs]
        groups = [[(q_ref[r, cs] * scale, k_ref[r, cs], v_ref[r, cs], gl_scr[r, cs],
                    range(GLA_SUB)) for cs in heads] for r in rows]
        for i, r, res in zip(subs, rows, _gla_local_many(groups, consts)):
            for h, (qe, o, kv, dec) in enumerate(res):
                cs = heads[h]
                qe_scr[r, cs] = qe
                o_scr[r, cs] = o
                kv_scr[i, h] = kv
                dec_scr[i, h] = dec
        return carry

    lax.fori_loop(0, nsub // pair, local_body, 0, unroll=min(8, nsub // pair))

    def state_body(i, carry):
        rows = pl.ds(pl.multiple_of(i * GLA_SUB, GLA_SUB), GLA_SUB)
        for h in range(GLA_G):
            cs = slice(h * LANES, (h + 1) * LANES)
            s = s_scr[h]
            o_scr[rows, cs] += jnp.dot(qe_scr[rows, cs], s.astype(BF16),
                                       preferred_element_type=F32)
            s_scr[h] = dec_scr[i, h] * s + kv_scr[i, h]
        return carry

    lax.fori_loop(0, nsub, state_body, 0, unroll=min(8, nsub))
    for h in range(GLA_G):
        cs = slice(h * LANES, (h + 1) * LANES)
        out_ref[:, cs] = _gla_finish(o_scr[:, cs], go_ref[:, cs], gain_ref[:, cs]).astype(out_ref.dtype)

    @pl.when(t == nt - 1)
    def _():
        sout_ref[0] = s_scr[...]


def gla_long(proj, z, s0, wts, *, batch, seq, tt, casts=()):
    nt = seq // tt
    gw = GLA_G * LANES
    base = 2 * D_RNN // gw
    per = D_GLA // gw
    nhg = GLA_HEADS // GLA_G
    bsel = (lambda b: b) if s0.shape[0] == batch else (lambda b: 0)
    col = lambda which: pl.BlockSpec((tt, gw), lambda b, h, t: (b * nt + t, base + which * per + h))
    n_steps = batch * nhg * nt
    cast_specs = []
    for a in casts:
        rblocks = math.gcd(n_steps, a.shape[0] // SUB_BF16)
        cblocks = n_steps // rblocks
        blk = (a.shape[0] // rblocks, a.shape[1] // cblocks)
        assert blk[0] * rblocks == a.shape[0] and blk[1] * cblocks == a.shape[1]
        assert blk[0] % SUB_BF16 == 0 and blk[1] % LANES == 0
        cast_specs.append(pl.BlockSpec(
            blk, lambda b, h, t, cblocks=cblocks:
                (((b * nhg + h) * nt + t) // cblocks, ((b * nhg + h) * nt + t) % cblocks)))
    return pl.pallas_call(
        functools.partial(_gla_long_kernel, tt=tt, n_casts=len(casts)),
        out_shape=(jax.ShapeDtypeStruct((batch * seq, D_GLA), BF16),
                   jax.ShapeDtypeStruct((batch, GLA_HEADS, GLA_DK, GLA_DK), F32),
                   *[jax.ShapeDtypeStruct(a.shape, BF16) for a in casts]),
        grid=(batch, nhg, nt),
        in_specs=[col(0), col(1), col(2), col(3),
                  pl.BlockSpec((tt, LANES), lambda b, h, t: (b * nt + t, 0)),
                  pl.BlockSpec((LANES, gw), lambda b, h, t: (0, h)),
                  pl.BlockSpec((1, gw), lambda b, h, t: (0, h)),
                  pl.BlockSpec((1, gw), lambda b, h, t: (0, h)),
                  pl.BlockSpec((1, GLA_G, GLA_DK, GLA_DK), lambda b, h, t: (bsel(b), h, 0, 0)),
                  *cast_specs],
        out_specs=(pl.BlockSpec((tt, gw), lambda b, h, t: (b * nt + t, h)),
                   pl.BlockSpec((1, GLA_G, GLA_DK, GLA_DK), lambda b, h, t: (b, h, 0, 0)),
                   *cast_specs),
        scratch_shapes=[pltpu.VMEM((GLA_G, GLA_DK, GLA_DK), F32),
                        pltpu.VMEM((tt, gw), F32),
                        pltpu.VMEM((tt, gw), F32),
                        pltpu.VMEM((tt, gw), BF16),
                        pltpu.VMEM((tt // GLA_SUB, GLA_G, GLA_DK, GLA_DK), F32),
                        pltpu.VMEM((tt // GLA_SUB, GLA_G, GLA_DK, GLA_DK), F32)],
        compiler_params=_params(("parallel", "parallel", "arbitrary")),
        name="gla_long",
    )(proj, proj, proj, proj, z, wts["gla_wa2"], wts["gla_ba2"], wts["gla_head_norm"], s0, *casts)


def _gla_short_kernel(q_ref, k_ref, v_ref, go_ref, z_ref, wa2_ref, ba2_ref, gain_ref, s0_ref,
                      out_ref, sout_ref, *, seq):
    consts = _gla_consts()
    row = consts[2]
    scale = GLA_DK ** -0.5
    gl = _gla_log_alpha(z_ref, wa2_ref, ba2_ref)
    nseq = GLA_SUB // seq
    assert nseq == GLA_G
    nblk = q_ref.shape[0] // GLA_SUB
    keys = [(r, h) for r in range(nblk) for h in range(GLA_G)]
    groups = []
    for r, h in keys:
        rows = slice(r * GLA_SUB, (r + 1) * GLA_SUB)
        cs = slice(h * LANES, (h + 1) * LANES)
        q = q_ref[rows, cs] * scale
        k = k_ref[rows, cs]
        v = v_ref[rows, cs]
        g = gl[rows, cs]
        units = []
        for b in range(nseq):
            mine = (row >= b * seq) & (row < (b + 1) * seq)
            zero = lambda a, mine=mine: jnp.where(mine, a, 0.0)
            units.append((zero(q), zero(k), zero(v), zero(g), range(b * seq, (b + 1) * seq)))
        groups.append(units)
    local = dict(zip(keys, _gla_local_many(groups, consts)))
    inter = {(r, h): [jnp.dot(local[r, h][b][0], s0_ref[r * nseq + b, h].astype(BF16),
                              preferred_element_type=F32) for b in range(nseq)]
             for r, h in keys}
    for r, h in keys:
        rows = slice(r * GLA_SUB, (r + 1) * GLA_SUB)
        cs = slice(h * LANES, (h + 1) * LANES)
        o = jnp.zeros((GLA_SUB, LANES), F32)
        for b in range(nseq):
            _, o_b, kv, dec = local[r, h][b]
            o = o + o_b + inter[r, h][b]
            sout_ref[r * nseq + b, h] = dec * s0_ref[r * nseq + b, h] + kv
        out_ref[rows, cs] = _gla_finish(o, go_ref[rows, cs], gain_ref[:, cs]).astype(out_ref.dtype)


def gla_short(proj, z, s0, wts, *, nb, seq):
    gw = GLA_G * LANES
    per = D_GLA // gw
    rows = GLA_SHORT_BLOCKS * GLA_SUB
    bpb = rows // seq
    col = lambda which: pl.BlockSpec((rows, gw), lambda i, h: (i, which * per + h))
    return pl.pallas_call(
        functools.partial(_gla_short_kernel, seq=seq),
        out_shape=(jax.ShapeDtypeStruct((nb * seq, D_GLA), BF16),
                   jax.ShapeDtypeStruct((nb, GLA_HEADS, GLA_DK, GLA_DK), F32)),
        grid=(nb // bpb, GLA_HEADS // GLA_G),
        in_specs=[col(0), col(1), col(2), col(3),
                  pl.BlockSpec((rows, LANES), lambda i, h: (i, 0)),
                  pl.BlockSpec((LANES, gw), lambda i, h: (0, h)),
                  pl.BlockSpec((1, gw), lambda i, h: (0, h)),
                  pl.BlockSpec((1, gw), lambda i, h: (0, h)),
                  pl.BlockSpec((bpb, GLA_G, GLA_DK, GLA_DK), lambda i, h: (i, h, 0, 0))],
        out_specs=(pl.BlockSpec((rows, gw), lambda i, h: (i, h)),
                   pl.BlockSpec((bpb, GLA_G, GLA_DK, GLA_DK), lambda i, h: (i, h, 0, 0))),
        compiler_params=_params(("parallel", "parallel")),
        name="gla_short",
    )(proj, proj, proj, proj, z, wts["gla_wa2"], wts["gla_ba2"], wts["gla_head_norm"], s0)


def _ffn_kernel(xm_ref, xs_ref, xp_ref, wg_ref, wu_ref, cw_ref, cb_ref, bufs_ref,
                hp_ref, hs_ref, nbufp_ref, nbufs_ref, extp_scr, exts_scr, mhist_scr,
                *, steps, tiles_per_seq, nb):
    i = pl.program_id(1)
    (m_first, _), (s_first, _), (p_first, p_count) = steps
    tail = FFN_CONV_W - 1
    pad = SUB
    tm = xp_ref.shape[0]
    rows_s = xs_ref.shape[0]
    hist_s = tail * nb

    def conv(ext_ref, start, shift, rows):
        gc = cb_ref[...]
        for c in range(FFN_CONV_W):
            lo = start - (tail - c) * shift
            gc = gc + ext_ref[lo:lo + rows, :] * cw_ref[c:c + 1, :]
        return gc

    @pl.when(i == m_first)
    def _():
        g = jnp.dot(xm_ref[...], wg_ref[...], preferred_element_type=F32)
        mhist_scr[...] = g[g.shape[0] - tail:, :]
        extp_scr[0:pad - tail, :] = jnp.zeros((pad - tail, g.shape[1]), F32)

    @pl.when((i >= p_first) & (i < p_first + p_count))
    def _():
        @pl.when((i - p_first) % tiles_per_seq == 0)
        def _():
            extp_scr[pad - tail:pad, :] = mhist_scr[...]

        x = xp_ref[...]
        g = jnp.dot(x, wg_ref[...], preferred_element_type=F32)
        extp_scr[pad:pad + tm, :] = g
        up = jnp.dot(x, wu_ref[...], preferred_element_type=F32)
        hist = extp_scr[0:pad, :]
        row8 = lax.broadcasted_iota(jnp.int32, (SUB, g.shape[1]), 0)
        gc = cb_ref[...]
        for c in range(FFN_CONV_W):
            s = tail - c
            if s == 0:
                gs = g
            else:
                gs = pltpu.roll(g, s, 0)
                head = jnp.where(row8 < s, pltpu.roll(hist, s, 0), gs[:SUB])
                gs = jnp.concatenate([head, gs[SUB:]], axis=0)
            gc = gc + gs * cw_ref[c:c + 1, :]
        hp_ref[...] = (_gelu(gc) * up).astype(hp_ref.dtype)
        last = extp_scr[pad + tm - tail:pad + tm, :]
        extp_scr[pad - tail:pad, :] = last
        nbufp_ref[0] = last

    @pl.when(i == s_first)
    def _():
        x = xs_ref[...]
        exts_scr[0:hist_s, :] = bufs_ref[...]
        exts_scr[hist_s:hist_s + rows_s, :] = jnp.dot(x, wg_ref[...], preferred_element_type=F32)
        up = jnp.dot(x, wu_ref[...], preferred_element_type=F32)
        hs_ref[...] = (_gelu(conv(exts_scr, hist_s, nb, rows_s)) * up).astype(hs_ref.dtype)
        nbufs_ref[...] = exts_scr[rows_s:rows_s + hist_s, :]


def ffn_gate(xm, xs, xp, bufs, wts, *, tm, tn, seq_rows, nb):
    rows_p, rows_s = xp.shape[0], xs.shape[0]
    tail = FFN_CONV_W - 1
    steps, ni = _group_steps([(xm.shape[0], xm.shape[0]), (rows_s, rows_s), (rows_p, tm)])
    (_, _), (_, _), (p_first, p_count) = steps
    tps = seq_rows // tm
    ptile = lambda i: _tile_index(i, p_first, p_count)
    const = lambda shape: pl.BlockSpec(shape, lambda j, i: (0, 0), pipeline_mode=pl.Buffered(1))
    colblk = lambda r: pl.BlockSpec((r, tn), lambda j, i: (0, j))
    return pl.pallas_call(
        functools.partial(_ffn_kernel, steps=tuple(steps), tiles_per_seq=tps, nb=nb),
        out_shape=(jax.ShapeDtypeStruct((rows_p, D_FF), BF16),
                   jax.ShapeDtypeStruct((rows_s, D_FF), BF16),
                   jax.ShapeDtypeStruct((rows_p // seq_rows, tail, D_FF), F32),
                   jax.ShapeDtypeStruct((tail * nb, D_FF), F32)),
        grid=(D_FF // tn, ni),
        in_specs=[const(xm.shape),
                  const(xs.shape),
                  pl.BlockSpec((tm, D_MODEL), lambda j, i: (ptile(i), 0)),
                  colblk(D_MODEL), colblk(D_MODEL), colblk(FFN_CONV_W), colblk(1),
                  colblk(tail * nb)],
        out_specs=(pl.BlockSpec((tm, tn), lambda j, i: (ptile(i), j)),
                   colblk(rows_s),
                   pl.BlockSpec((1, tail, tn), lambda j, i: (ptile(i) // tps, 0, j)),
                   colblk(tail * nb)),
        scratch_shapes=[pltpu.VMEM((tm + SUB, tn), F32),
                        pltpu.VMEM((rows_s + tail * nb, tn), F32),
                        pltpu.VMEM((tail, tn), F32)],
        compiler_params=_params(("parallel", "arbitrary")),
        name="ffn_gate",
    )(xm, xs, xp, wts["w_gate"], wts["w_up"], wts["ffn_conv_w"], wts["ffn_conv_b"], bufs)


PROMPT_TM = 1024
IN_TN = 1024
OUT_TN = 512
DOWN_TM = 512
DOWN_TN = 512
FFN_TM = 1024
FFN_TN = 512
SEQ_TT = 256
NORM_TM = 512


def kernel(x_prompt, x_sample, state_rglru_conv, state_rglru_h, state_gla_S, state_ffn_conv,
           meta_tokens, norm_mix, w_in, rg_conv_w, rg_conv_b, rg_wa, rg_ba, rg_wx, rg_bx, rg_lambda,
           rg_out_norm, gla_wa2, gla_ba2, gla_head_norm, w_out, norm_ffn, w_gate, w_up,
           ffn_conv_w, ffn_conv_b, w_down, final_norm):
    batch, seq, _ = x_prompt.shape
    nb, sseq, _ = x_sample.shape
    row = lambda a: a.reshape(1, -1)
    w_in_b = w_in[0].astype(BF16)
    wts = {
        "w_in": w_in_b,
        "w_z": jnp.pad(w_in_b[:, PROJ_COLS:], ((0, 0), (0, LANES - GLA_RANK))),
        "rg_conv_w": rg_conv_w[0], "rg_conv_b": row(rg_conv_b[0]),
        "rg_wax": jnp.concatenate([rg_wa[0], rg_wx[0]], axis=-1).astype(BF16),
        "rg_ba": row(rg_ba[0]), "rg_bx": row(rg_bx[0]), "rg_lambda": row(rg_lambda[0]),
        "rg_out_norm": row(rg_out_norm[0]),
        "gla_wa2": jnp.pad(gla_wa2[0].astype(BF16), ((0, LANES - GLA_RANK), (0, 0))),
        "gla_ba2": row(gla_ba2[0]), "gla_head_norm": row(gla_head_norm[0]),
        "ffn_conv_w": ffn_conv_w[0], "ffn_conv_b": row(ffn_conv_b[0]),
    }

    def to_batch_major(a):
        return jnp.swapaxes(a.reshape(sseq, nb, -1), 0, 1).reshape(nb * sseq, -1)

    def to_time_major(a):
        return jnp.swapaxes(a.reshape(nb, sseq, -1), 0, 1).reshape(nb * sseq, -1)

    rows_p, rows_s = batch * seq, nb * sseq
    xs = [meta_tokens, to_time_major(x_sample), x_prompt.reshape(rows_p, D_MODEL)]
    tms = [N_META, rows_s, PROMPT_TM]
    norm_tms = [N_META, NORM_TM, NORM_TM]

    def norm_all(arrs, g, dtype):
        return [rmsnorm(a, g, dtype, t) for a, t in zip(arrs, norm_tms)]

    xn = norm_all(xs, norm_mix[0], BF16)
    proj_m, proj_s, proj_p = grouped_matmul(
        [dict(parts=[a], tm=t) for a, t in zip(xn, tms)], wts["w_in"], tn=IN_TN, n_cols=PROJ_COLS,
        fuse_w=True)
    z_m, z_s, z_p = grouped_matmul(
        [dict(parts=[a], tm=t) for a, t in zip(xn, tms)], wts["w_z"], tn=LANES, n_cols=LANES)

    zeros = lambda *s: jnp.zeros(s, F32)
    rnn_m, m_conv, m_h = rg_long(proj_m, zeros(1, CONV_W - 1, D_RNN), zeros(1, 1, D_RNN), wts,
                                 batch=1, seq=N_META, tt=N_META, reset_first=True)
    gla_m, m_s = gla_long(proj_m, z_m, zeros(1, GLA_HEADS, GLA_DK, GLA_DK), wts, batch=1,
                          seq=N_META, tt=N_META)
    rnn_p, p_conv, p_h = rg_long(proj_p, m_conv, m_h, wts, batch=batch, seq=seq, tt=SEQ_TT,
                                 reset_first=False)
    gla_p, p_s, wts["w_gate"], wts["w_up"], wts["w_down"] = gla_long(
        proj_p, z_p, m_s, wts, batch=batch, seq=seq, tt=SEQ_TT,
        casts=(w_gate.reshape(D_MODEL, D_FF), w_up.reshape(D_MODEL, D_FF),
               w_down.reshape(D_FF, D_MODEL)))
    rnn_s, s_conv_t, s_h = rg_short(proj_s, jnp.swapaxes(state_rglru_conv[0], 0, 1),
                                    state_rglru_h[0], wts, nb=nb, seq=sseq)
    gla_sb, s_s = gla_short(to_batch_major(proj_s[:, 2 * D_RNN:]), to_batch_major(z_s),
                            state_gla_S[0], wts, nb=nb, seq=sseq)
    gla_s = to_time_major(gla_sb)

    x1 = grouped_matmul(
        [dict(parts=[r, g], tm=t, res=x)
         for r, g, t, x in zip([rnn_m, rnn_s, rnn_p], [gla_m, gla_s, gla_p], tms, xs)],
        w_out, tn=OUT_TN, n_cols=D_MODEL, cast_w=True)
    xn2 = norm_all(x1, norm_ffn[0], BF16)
    tail = FFN_CONV_W - 1
    bufs = jnp.swapaxes(state_ffn_conv[0], 0, 1).reshape(tail * nb, D_FF)
    hid_p, hid_s, p_ffn, s_ffn_t = ffn_gate(xn2[0], xn2[1], xn2[2], bufs, wts,
                                            tm=FFN_TM, tn=FFN_TN, seq_rows=seq, nb=nb)
    (x2_s,) = grouped_matmul([dict(parts=[hid_s], tm=rows_s, res=x1[1])], wts["w_down"],
                             tn=DOWN_TN, n_cols=D_MODEL)
    (x2_p,) = grouped_matmul([dict(parts=[hid_p], tm=DOWN_TM, res=x1[2])], wts["w_down"],
                             tn=DOWN_TN, n_cols=D_MODEL)
    y_p = rmsnorm(x2_p, final_norm, F32, NORM_TM)
    y_s = to_batch_major(rmsnorm(x2_s, final_norm, F32, NORM_TM))

    return (y_p.reshape(batch, seq, D_MODEL), y_s.reshape(nb, sseq, D_MODEL),
            p_conv[None], p_h.reshape(1, batch, D_RNN), p_s[None], p_ffn[None],
            jnp.swapaxes(s_conv_t, 0, 1)[None], s_h[None], s_s[None],
            jnp.swapaxes(s_ffn_t.reshape(tail, nb, D_FF), 0, 1)[None])
```
